```python
import jax
import jax.numpy as jnp
from jax import lax
import numpy as np

D_MODEL = 4096
BATCH = 2
SEQ = 8192
DEPTH = 4

CTX_LEN = 256
GRID_W = 64
EPS = 1e-6
NEG = -1e30
ROPE_BASE = 10000.0
CHUNK = 64
N_BRANCH = 4
BRANCH_W = D_MODEL // 4
A_HEADS = 4
A_DV = BRANCH_W // A_HEADS
A_DQK = A_DV // 2
F_BIAS_LO = 3.0
F_BIAS_HI = 6.0
B_HEADS = 4
B_DV = BRANCH_W // B_HEADS
B_DK = B_DV // 2
C_HEADS = 16
C_KV_HEADS = 2
C_HD = BRANCH_W // C_HEADS
WINDOW = 128
D_HEADS = 16
D_HD = BRANCH_W // D_HEADS
NA_ROWS = 8
NA_COLS = 16
N_EXPERTS = 32
TOP_K = 4
EXPERT_FF = D_MODEL // 32
SWIGLU_LIMIT = 7.0
SWIGLU_ALPHA = 1.702
ADA_RANK = D_MODEL // 16

SPLIT_SIZES = (
    A_HEADS * A_DQK, A_HEADS * A_DQK, A_HEADS * A_DV, A_HEADS * A_DV, 4 * A_HEADS,
    B_HEADS * B_DK, B_HEADS * B_DK, B_HEADS * B_DV, B_HEADS * B_DV,
    C_HEADS * C_HD, C_KV_HEADS * C_HD, C_KV_HEADS * C_HD,
    D_HEADS * D_HD, D_HEADS * D_HD, D_HEADS * D_HD,
    N_BRANCH * D_MODEL,
)
IN_WIDTH = sum(SPLIT_SIZES)
SPLIT_POINTS = tuple(sum(SPLIT_SIZES[: i + 1]) for i in range(len(SPLIT_SIZES) - 1))

kernel_name = 'hybrid_parallel_mixer_moe_flow_block'


def rmsnorm(x, gain):
    x32 = x.astype(jnp.float32)
    y = x32 * lax.rsqrt(jnp.mean(x32 * x32, axis=-1, keepdims=True) + EPS)
    return y.astype(x.dtype) * gain


def modulate(h, shift, scale):
    return h * (1.0 + scale) + shift


def ada_modulation(cvec, down, up, bias):
    z = (jax.nn.silu(cvec) @ down) @ up + bias
    return z.reshape(*cvec.shape[:-1], 6, D_MODEL)


def split_heads(t, h):
    b, n, w = t.shape
    return t.reshape(b, n, h, w // h).transpose(0, 2, 1, 3)


def merge_heads(t):
    b, h, n, d = t.shape
    return t.transpose(0, 2, 1, 3).reshape(b, n, h * d)


def flip_t(*ts):
    return [jnp.flip(t, axis=2) for t in ts]


def head_norm(h, gain, center):
    h32 = h.astype(jnp.float32)
    if center:
        h32 = h32 - jnp.mean(h32, axis=-1, keepdims=True)
    y = h32 * lax.rsqrt(jnp.mean(h32 * h32, axis=-1, keepdims=True) + EPS)
    return merge_heads(y) * gain


def axial_rope(n, hd):
    pos = jnp.arange(n)
    row = (pos // GRID_W).astype(jnp.float32)
    col = (pos % GRID_W).astype(jnp.float32)
    n_freq = hd // 4
    inv = ROPE_BASE ** (-jnp.arange(n_freq, dtype=jnp.float32) / n_freq)
    ang = jnp.concatenate([row[:, None] * inv, col[:, None] * inv], axis=-1)
    return jnp.cos(ang), jnp.sin(ang)


def apply_rope(t, cos, sin):
    half = t.shape[-1] // 2
    t1, t2 = t[..., :half], t[..., half:]
    cos = cos.astype(t.dtype)
    sin = sin.astype(t.dtype)
    return jnp.concatenate([t1 * cos - t2 * sin, t1 * sin + t2 * cos], axis=-1)


def attend(parts, sink=None):
    m = parts[0][0].max(-1)
    for s, _, _ in parts[1:]:
        m = jnp.maximum(m, s.max(-1))
    if sink is not None:
        m = jnp.maximum(m, sink)
        den = jnp.exp(sink - m)
    else:
        den = jnp.zeros_like(m)
    out = None
    for s, v, eq in parts:
        e = jnp.exp(s - m[..., None])
        den = den + e.sum(-1)
        o = jnp.einsum(eq, e, v)
        out = o if out is None else out + o
    return out / den[..., None]


def to_chunks(t):
    b, h, n = t.shape[:3]
    return jnp.moveaxis(t.reshape(b, h, n // CHUNK, CHUNK, *t.shape[3:]), 2, 0)


def from_chunks(t):
    t = jnp.moveaxis(t, 0, 2)
    return t.reshape(t.shape[0], t.shape[1], t.shape[2] * t.shape[3], *t.shape[4:])


def mlstm_scan(q, k, v, i_pre, f_pre, state):
    log_f = jax.nn.log_sigmoid(f_pre.astype(jnp.float32))
    i_log = i_pre.astype(jnp.float32)
    lower = jnp.tril(jnp.ones((CHUNK, CHUNK), dtype=bool))

    def step(carry, xs):
        C, nvec, m = carry
        qc, kc, vc, ic, fc = xs
        b = jnp.cumsum(fc, axis=-1)
        log_d = jnp.where(lower, b[..., :, None] - b[..., None, :] + ic[..., None, :], NEG)
        m_inter = b + m[..., None]
        m_t = jnp.maximum(m_inter, jnp.max(log_d, axis=-1))
        w_intra = jnp.exp(log_d - m_t[..., None])
        w_inter = jnp.exp(m_inter - m_t)
        s = jnp.einsum('bhtd,bhsd->bhts', qc, kc, preferred_element_type=jnp.float32) * w_intra
        num = jnp.einsum('bhts,bhsv->bhtv', s, vc) + w_inter[..., None] * jnp.einsum('bhtd,bhvd->bhtv', qc, C)
        den = jnp.sum(s, axis=-1) + w_inter * jnp.einsum('bhtd,bhd->bht', qc, nvec)
        h = num / jnp.maximum(jnp.abs(den), jnp.exp(-m_t))[..., None]
        b_end = b[..., -1]
        log_w = b_end[..., None] - b + ic
        m_new = jnp.maximum(b_end + m, jnp.max(log_w, axis=-1))
        w_end = jnp.exp(log_w - m_new[..., None])
        decay = jnp.exp(b_end + m - m_new)
        C = decay[..., None, None] * C + jnp.einsum('bhs,bhsv,bhsd->bhvd', w_end, vc, kc)
        nvec = decay[..., None] * nvec + jnp.einsum('bhs,bhsd->bhd', w_end, kc)
        return (C, nvec, m_new), h

    state, h = lax.scan(step, state, tuple(to_chunks(t) for t in (q, k, v, i_log, log_f)))
    return state, from_chunks(h)


def mlstm_mixer(pc, pl, gate_bias, norm_gain, need_ctx):
    def prep(q, k, v, o, g):
        b, n, _ = q.shape
        gates = (g + gate_bias).reshape(b, n, 4, A_HEADS).transpose(2, 0, 3, 1)
        return (split_heads(q, A_HEADS) * A_DQK ** -0.5, split_heads(k, A_HEADS),
                split_heads(v, A_HEADS), o, gates)

    qc, kc, vc, oc, gc = prep(*pc)
    ql, kl, vl, ol, gl = prep(*pl)
    b = qc.shape[0]
    zero = (jnp.zeros((b, A_HEADS, A_DV, A_DQK), jnp.float32),
            jnp.zeros((b, A_HEADS, A_DQK), jnp.float32),
            jnp.zeros((b, A_HEADS), jnp.float32))
    st_f, hc_f = mlstm_scan(qc, kc, vc, gc[0], gc[1], zero)
    _, hl_f = mlstm_scan(ql, kl, vl, gl[0], gl[1], st_f)
    st_b, hc_b = mlstm_scan(*flip_t(qc, kc, vc, gc[2], gc[3]), zero)
    _, hl_b = mlstm_scan(*flip_t(ql, kl, vl, gl[2], gl[3]), st_b)
    yl = jax.nn.sigmoid(ol) * head_norm(hl_f + jnp.flip(hl_b, axis=2), norm_gain, False)
    yc = None
    if need_ctx:
        yc = jax.nn.sigmoid(oc) * head_norm(hc_f + jnp.flip(hc_b, axis=2), norm_gain, False)
    return yc, yl


def retention_scan(q, k, v, log_gamma, state):
    idx = jnp.arange(CHUNK, dtype=jnp.float32)
    diff = idx[:, None] - idx[None, :]
    decay_mat = jnp.where(diff >= 0, jnp.exp(log_gamma[:, None, None] * jnp.maximum(diff, 0.0)), 0.0)
    read_w = jnp.exp(log_gamma[:, None] * (idx + 1.0))
    write_w = jnp.exp(log_gamma[:, None] * (CHUNK - 1.0 - idx))
    chunk_decay = jnp.exp(log_gamma * CHUNK)

    def step(S, xs):
        qc, kc, vc = xs
        s = jnp.einsum('bhtd,bhsd->bhts', qc, kc, preferred_element_type=jnp.float32) * decay_mat
        o = jnp.einsum('bhts,bhsv->bhtv', s, vc) + read_w[..., None] * jnp.einsum('bhtd,bhdv->bhtv', qc, S)
        S = chunk_decay[:, None, None] * S + jnp.einsum('bhsd,hs,bhsv->bhdv', kc, write_w, vc)
        return S, o

    state, o = lax.scan(step, state, (to_chunks(q), to_chunks(k), to_chunks(v)))
    return state, from_chunks(o)


def retention_mixer(pc, pl, decay_p, norm_gain, need_ctx):
    qc, kc, vc, gc = pc
    ql, kl, vl, gl = pl
    n = ql.shape[1]
    cos, sin = axial_rope(n, B_DK)

    def prep(q, k, v, rope):
        q = split_heads(q, B_HEADS)
        k = split_heads(k, B_HEADS) * B_DK ** -0.5
        v = split_heads(v, B_HEADS)
        if rope:
            q, k = apply_rope(q, cos, sin), apply_rope(k, cos, sin)
        return q, k, v

    qc, kc, vc = prep(qc, kc, vc, False)
    ql, kl, vl = prep(ql, kl, vl, True)
    log_gamma = -jax.nn.softplus(decay_p.astype(jnp.float32))
    zero = jnp.zeros((qc.shape[0], B_HEADS, B_DK, B_DV), jnp.float32)
    st_f, oc_f = retention_scan(qc, kc, vc, log_gamma[0], zero)
    _, ol_f = retention_scan(ql, kl, vl, log_gamma[0], st_f)
    st_b, oc_b = retention_scan(*flip_t(qc, kc, vc), log_gamma[1], zero)
    _, ol_b = retention_scan(*flip_t(ql, kl, vl), log_gamma[1], st_b)
    yl = jax.nn.silu(gl) * head_norm(ol_f + jnp.flip(ol_b, axis=2), norm_gain, True)
    yc = None
    if need_ctx:
        yc = jax.nn.silu(gc) * head_norm(oc_f + jnp.flip(oc_b, axis=2), norm_gain, True)
    return yc, yl


def window_mixer(pc, pl, sink, need_ctx):
    g, r = C_KV_HEADS, C_HEADS // C_KV_HEADS
    scale = C_HD ** -0.5

    def q_heads(t):
        b, n, _ = t.shape
        return t.reshape(b, n, g, r, C_HD).transpose(0, 2, 3, 1, 4)

    def out_heads(o):
        b, n = o.shape[0], o.shape[3]
        return o.transpose(0, 3, 1, 2, 4).reshape(b, n, C_HEADS * C_HD)

    qc, kc, vc = q_heads(pc[0]), split_heads(pc[1], g), split_heads(pc[2], g)
    ql, kl, vl = q_heads(pl[0]), split_heads(pl[1], g), split_heads(pl[2], g)
    b, _, _, n, hd = ql.shape
    cos, sin = axial_rope(n, C_HD)
    ql, kl = apply_rope(ql, cos, sin), apply_rope(kl, cos, sin)
    sink = sink.reshape(g, r)
    nb = n // WINDOW
    qb = ql.reshape(b, g, r, nb, WINDOW, hd)

    def band(t):
        tp = jnp.pad(t, ((0, 0), (0, 0), (WINDOW, WINDOW), (0, 0))).reshape(b, g, nb + 2, WINDOW, hd)
        return jnp.concatenate([tp[:, :, :-2], tp[:, :, 1:-1], tp[:, :, 2:]], axis=3)

    kb, vb = band(kl), band(vl)
    qpos = jnp.arange(nb)[:, None] * WINDOW + jnp.arange(WINDOW)[None, :]
    kpos = (jnp.arange(nb)[:, None] - 1) * WINDOW + jnp.arange(3 * WINDOW)[None, :]
    valid = ((jnp.abs(qpos[:, :, None] - kpos[:, None, :]) <= WINDOW)
             & (kpos[:, None, :] >= 0) & (kpos[:, None, :] < n))
    s_loc = jnp.where(valid, jnp.einsum('bgrnqd,bgnkd->bgrnqk', qb, kb, preferred_element_type=jnp.float32) * scale, NEG)
    s_ctx = jnp.einsum('bgrnqd,bgcd->bgrnqc', qb, kc, preferred_element_type=jnp.float32) * scale
    ol = attend([(s_loc, vb, 'bgrnqk,bgnkd->bgrnqd'), (s_ctx, vc, 'bgrnqc,bgcd->bgrnqd')],
                sink[None, :, :, None, None])
    yl = out_heads(ol.reshape(b, g, r, n, hd))
    yc = None
    if need_ctx:
        s_cc = jnp.einsum('bgrqd,bgkd->bgrqk', qc, kc, preferred_element_type=jnp.float32) * scale
        yc = out_heads(attend([(s_cc, vc, 'bgrqk,bgkd->bgrqd')], sink[None, :, :, None]))
    return yc, yl


def neighbourhood_mixer(pc, pl, bias_tab, need_ctx):
    qc, kc, vc = [split_heads(t, D_HEADS) for t in pc]
    ql, kl, vl = [split_heads(t, D_HEADS) for t in pl]
    b, h, n, hd = ql.shape
    rows = n // GRID_W
    wr = min(NA_ROWS, rows)
    scale = hd ** -0.5
    r_ids = jnp.arange(rows)
    row_idx = jnp.clip(r_ids - NA_ROWS // 2, 0, rows - wr)[:, None] + jnp.arange(wr)[None, :]
    col = jnp.arange(GRID_W)
    col_start = jnp.clip(col - NA_COLS // 2, 0, GRID_W - NA_COLS)
    col_valid = (col[None, :] >= col_start[:, None]) & (col[None, :] < col_start[:, None] + NA_COLS)

    def gather_rows(t):
        gt = t.reshape(b, h, rows, GRID_W, hd)[:, :, row_idx]
        return gt.reshape(b, h, rows, wr * GRID_W, hd)

    kg, vg = gather_rows(kl), gather_rows(vl)
    qg = ql.reshape(b, h, rows, GRID_W, hd)
    dr = row_idx - r_ids[:, None] + (NA_ROWS - 1)
    dc = jnp.clip(col[None, :] - col[:, None], -(NA_COLS - 1), NA_COLS - 1) + (NA_COLS - 1)
    bias = jnp.take(bias_tab[:, dr], dc, axis=-1)
    bias = bias.transpose(0, 1, 3, 2, 4).reshape(h, rows, GRID_W, wr * GRID_W)
    mask = jnp.broadcast_to(col_valid[:, None, :], (GRID_W, wr, GRID_W)).reshape(GRID_W, wr * GRID_W)
    s_loc = jnp.einsum('bhrqd,bhrkd->bhrqk', qg, kg, preferred_element_type=jnp.float32) * scale + bias.astype(jnp.float32)
    s_loc = jnp.where(mask, s_loc, NEG)
    s_ctx = jnp.einsum('bhrqd,bhcd->bhrqc', qg, kc, preferred_element_type=jnp.float32) * scale
    ol = attend([(s_loc, vg, 'bhrqk,bhrkd->bhrqd'), (s_ctx, vc, 'bhrqc,bhcd->bhrqd')])
    yl = merge_heads(ol.reshape(b, h, n, hd))
    yc = None
    if need_ctx:
        s_cc = jnp.einsum('bhqd,bhkd->bhqk', qc, kc, preferred_element_type=jnp.float32) * scale
        yc = merge_heads(attend([(s_cc, vc, 'bhqk,bhkd->bhqd')]))
    return yc, yl


def merge_branches(branches, gate_pre, w_branch, w_out):
    acc = None
    for i, y in enumerate(branches):
        gi = jax.nn.sigmoid(gate_pre[..., i * D_MODEL:(i + 1) * D_MODEL])
        term = gi * (y.astype(gate_pre.dtype) @ w_branch[i])
        acc = term if acc is None else acc + term
    return acc @ w_out


def hybrid_mixer(hc, hl, w_in, a_gate_bias, a_norm, r_decay, r_norm, sink, na_bias, w_branch, w_out, need_ctx):
    pc = jnp.split(hc @ w_in, SPLIT_POINTS, axis=-1)
    pl = jnp.split(hl @ w_in, SPLIT_POINTS, axis=-1)
    ya_c, ya_l = mlstm_mixer(pc[0:5], pl[0:5], a_gate_bias, a_norm, need_ctx)
    yb_c, yb_l = retention_mixer(pc[5:9], pl[5:9], r_decay, r_norm, need_ctx)
    yc_c, yc_l = window_mixer(pc[9:12], pl[9:12], sink, need_ctx)
    yd_c, yd_l = neighbourhood_mixer(pc[12:15], pl[12:15], na_bias, need_ctx)
    yl = merge_branches([ya_l, yb_l, yc_l, yd_l], pl[15], w_branch, w_out)
    yc = None
    if need_ctx:
        yc = merge_branches([ya_c, yb_c, yc_c, yd_c], pc[15], w_branch, w_out)
    return yc, yl


def moe(h, router_w, router_b, w1, b1, w2, b2):
    b, n, d = h.shape
    t = h.reshape(b * n, d)
    logits = (t @ router_w + router_b).astype(jnp.float32)
    top_v, top_i = lax.top_k(logits, TOP_K)
    wts = jax.nn.softmax(top_v, axis=-1)
    gates = jnp.sum(jax.nn.one_hot(top_i, N_EXPERTS, dtype=jnp.float32) * wts[..., None], axis=1)
    hid = jnp.einsum('nd,edf->nef', t, w1) + b1
    g_h = jnp.minimum(hid[..., ::2], SWIGLU_LIMIT)
    u_h = jnp.clip(hid[..., 1::2], -SWIGLU_LIMIT, SWIGLU_LIMIT)
    act = g_h * jax.nn.sigmoid(SWIGLU_ALPHA * g_h) * (u_h + 1.0) * gates[..., None].astype(hid.dtype)
    out = jnp.einsum('nef,efd->nd', act, w2) + gates.astype(hid.dtype) @ b2
    return out.reshape(b, n, d).astype(h.dtype)


def setup_inputs(seed: int = 0) -> dict:
    key = jax.random.key(seed)
    ks = jax.random.split(key, 24)
    f32 = jnp.float32

    def nrm(k, shape, scale):
        return jax.random.normal(k, shape, f32) * scale

    lin = jnp.linspace(F_BIAS_LO, F_BIAS_HI, A_HEADS, dtype=f32)
    zeros_h = jnp.zeros((A_HEADS,), f32)
    gate_base = jnp.concatenate([zeros_h, lin, zeros_h, lin])
    heads = np.arange(B_HEADS, dtype=np.float32)
    log_g = np.log(1.0 - 2.0 ** (-5.0 - heads))
    decay_base = jnp.asarray(np.log(np.expm1(-log_g)), f32)
    return {
        'x': nrm(ks[0], (BATCH, SEQ, D_MODEL), 1.0),
        'c': nrm(ks[1], (BATCH, D_MODEL), 1.0),
        'ctx': nrm(ks[2], (BATCH, CTX_LEN, D_MODEL), 1.0),
        'c_ctx': nrm(ks[3], (D_MODEL,), 1.0),
        'ada_down': nrm(ks[4], (DEPTH, D_MODEL, ADA_RANK), D_MODEL ** -0.5),
        'ada_up': nrm(ks[5], (DEPTH, ADA_RANK, 6 * D_MODEL), 0.5 * ADA_RANK ** -0.5),
        'ada_bias': nrm(ks[6], (DEPTH, 6 * D_MODEL), 0.02),
        'norm_gain': 1.0 + nrm(ks[7], (DEPTH, 2, D_MODEL), 0.02),
        'w_in': nrm(ks[8], (DEPTH, D_MODEL, IN_WIDTH), D_MODEL ** -0.5),
        'mlstm_gate_bias': gate_base + nrm(ks[9], (DEPTH, 4 * A_HEADS), 0.1),
        'mlstm_norm_gain': 1.0 + nrm(ks[10], (DEPTH, A_HEADS * A_DV), 0.02),
        'ret_decay': decay_base + nrm(ks[11], (DEPTH, 2, B_HEADS), 0.05),
        'ret_norm_gain': 1.0 + nrm(ks[12], (DEPTH, B_HEADS * B_DV), 0.02),
        'sink': nrm(ks[13], (DEPTH, C_HEADS), 0.5),
        'na_bias': nrm(ks[14], (DEPTH, D_HEADS, 2 * NA_ROWS - 1, 2 * NA_COLS - 1), 0.1),
        'w_branch': nrm(ks[15], (DEPTH, N_BRANCH, BRANCH_W, D_MODEL), BRANCH_W ** -0.5),
        'w_out': nrm(ks[16], (DEPTH, D_MODEL, D_MODEL), D_MODEL ** -0.5),
        'router_w': nrm(ks[17], (DEPTH, D_MODEL, N_EXPERTS), D_MODEL ** -0.5),
        'router_b': nrm(ks[18], (DEPTH, N_EXPERTS), 0.01),
        'moe_w1': nrm(ks[19], (DEPTH, N_EXPERTS, D_MODEL, 2 * EXPERT_FF), D_MODEL ** -0.5),
        'moe_b1': nrm(ks[20], (DEPTH, N_EXPERTS, 2 * EXPERT_FF), 0.02),
        'moe_w2': nrm(ks[21], (DEPTH, N_EXPERTS, EXPERT_FF, D_MODEL), EXPERT_FF ** -0.5),
        'moe_b2': nrm(ks[22], (DEPTH, N_EXPERTS, D_MODEL), 0.02),
        'final_gain': 1.0 + nrm(ks[23], (D_MODEL,), 0.02),
    }


def reference(x, c, ctx, c_ctx, ada_down, ada_up, ada_bias, norm_gain, w_in, mlstm_gate_bias,
              mlstm_norm_gain, ret_decay, ret_norm_gain, sink, na_bias, w_branch, w_out,
              router_w, router_b, moe_w1, moe_b1, moe_w2, moe_b2, final_gain):
    for l in range(DEPTH):
        need_ctx = l < DEPTH - 1
        mod = ada_modulation(c, ada_down[l], ada_up[l], ada_bias[l])[:, None]
        mod_c = ada_modulation(c_ctx, ada_down[l], ada_up[l], ada_bias[l])
        hl = modulate(rmsnorm(x, norm_gain[l, 0]), mod[:, :, 0], mod[:, :, 1])
        hc = modulate(rmsnorm(ctx, norm_gain[l, 0]), mod_c[0], mod_c[1])
        yc, yl = hybrid_mixer(hc, hl, w_in[l], mlstm_gate_bias[l], mlstm_norm_gain[l], ret_decay[l],
                              ret_norm_gain[l], sink[l], na_bias[l], w_branch[l], w_out[l], need_ctx)
        x = x + mod[:, :, 2] * yl
        hl = modulate(rmsnorm(x, norm_gain[l, 1]), mod[:, :, 3], mod[:, :, 4])
        x = x + mod[:, :, 5] * moe(hl, router_w[l], router_b[l], moe_w1[l], moe_b1[l], moe_w2[l], moe_b2[l])
        if need_ctx:
            ctx = ctx + mod_c[2] * yc
            hc = modulate(rmsnorm(ctx, norm_gain[l, 1]), mod_c[3], mod_c[4])
            ctx = ctx + mod_c[5] * moe(hc, router_w[l], router_b[l], moe_w1[l], moe_b1[l], moe_w2[l], moe_b2[l])
    return rmsnorm(x, final_gain)
```

```python
import functools

import jax
import jax.numpy as jnp
import numpy as np
from jax import lax
from jax.experimental import pallas as pl
from jax.experimental.pallas import tpu as pltpu

F32 = jnp.float32
BF16 = jnp.bfloat16

GRID_W = 64
EPS = 1e-6
NEG = -1e30
ROPE_BASE = 10000.0
N_BRANCH = 4
A_HEADS = 4
B_HEADS = 4
C_HEADS = 16
C_KV_HEADS = 2
D_HEADS = 16
WINDOW = 128
NA_ROWS = 8
NA_COLS = 16
TOP_K = 4
SWIGLU_LIMIT = 7.0
SWIGLU_ALPHA = 1.702

LANES = 128
VMEM_LIMIT = 56 * 1024 * 1024
ROW_TILE = 256


def _row_tile(rows, target):
    best = ROW_TILE
    for tile in range(ROW_TILE, target + 1, ROW_TILE):
        if rows % tile == 0:
            best = tile
    assert rows % best == 0
    return best


def _params(sem, vmem=VMEM_LIMIT):
    return pltpu.CompilerParams(dimension_semantics=sem, vmem_limit_bytes=vmem)


def _dot(a, b):
    return jnp.dot(a, b, preferred_element_type=F32)


def _dot_nt(a, b):
    return lax.dot_general(a, b, (((1,), (1,)), ((), ())), preferred_element_type=F32)


def _dot_tn(a, b):
    return lax.dot_general(a, b, (((0,), (0,)), ((), ())), preferred_element_type=F32)


def _sigmoid(x):
    return 1.0 / (1.0 + jnp.exp(-x))


def _softplus(x):
    return jnp.maximum(x, 0.0) + jnp.log1p(jnp.exp(-jnp.abs(x)))


def _log_sigmoid(x):
    return -_softplus(-x)


def _ada_kernel(cv_ref, down_ref, up_ref, bias_ref, out_ref):
    cv = cv_ref[...]
    a = cv * _sigmoid(cv)
    z = _dot(a.astype(BF16), down_ref[0].astype(BF16))
    out_ref[0] = _dot(z.astype(BF16), up_ref[0].astype(BF16)) + bias_ref[0]


def _ada_modulation(cvecs, ada_down, ada_up, ada_bias):
    depth, d, r = ada_down.shape
    rows = cvecs.shape[0]
    return pl.pallas_call(
        _ada_kernel,
        grid=(depth, 6),
        in_specs=[
            pl.BlockSpec((rows, d), lambda l, j: (0, 0)),
            pl.BlockSpec((1, d, r), lambda l, j: (l, 0, 0)),
            pl.BlockSpec((1, r, d), lambda l, j: (l, 0, j)),
            pl.BlockSpec((1, 1, d), lambda l, j: (l, 0, j)),
        ],
        out_specs=pl.BlockSpec((1, rows, d), lambda l, j: (l, 0, j)),
        out_shape=jax.ShapeDtypeStruct((depth, rows, 6 * d), F32),
        compiler_params=_params(("parallel", "arbitrary")),
        name="ada_modulation",
    )(cvecs, ada_down, ada_up, ada_bias.reshape(depth, 1, 6 * d))


def _mod_row(tile, tiles_per_seq, ctx_tiles):
    b = tile // tiles_per_seq
    j = tile - b * tiles_per_seq
    return jnp.where(j < ctx_tiles, 0, 1 + b)


def _rmsnorm(x, gain):
    return x * lax.rsqrt(jnp.mean(x * x, axis=-1, keepdims=True) + EPS) * gain


def _resid_norm_kernel(*refs, has_delta, has_router, tiles_per_seq, ctx_tiles):
    refs = list(refs)
    z_ref = refs.pop(0)
    if has_delta:
        delta_ref = refs.pop(0)
        gate_ref = refs.pop(0)
    gain_ref = refs.pop(0)
    shift_ref = refs.pop(0)
    scale_ref = refs.pop(0)
    if has_router:
        rw_ref = refs.pop(0)
        rb_ref = refs.pop(0)
    if has_delta:
        znew_ref = refs.pop(0)
    h_ref = refs.pop(0)
    if has_router:
        gates_ref = refs.pop(0)

    row = _mod_row(pl.program_id(0), tiles_per_seq, ctx_tiles)
    z = z_ref[...]
    if has_delta:
        z = z + gate_ref[pl.ds(row, 1), :] * delta_ref[...].astype(F32)
        znew_ref[...] = z
    y = _rmsnorm(z, gain_ref[...])
    h = y * (1.0 + scale_ref[pl.ds(row, 1), :]) + shift_ref[pl.ds(row, 1), :]
    hb = h.astype(BF16)
    h_ref[...] = hb
    if has_router:
        logits = _dot(hb, rw_ref[...]) + rb_ref[...]
        lane = lax.broadcasted_iota(jnp.int32, logits.shape, 1).astype(F32)
        gates = jnp.zeros_like(logits)
        den = jnp.zeros((logits.shape[0], 1), F32)
        top = None
        for _ in range(TOP_K):
            m = jnp.max(logits, axis=-1, keepdims=True)
            idx = jnp.min(jnp.where(logits == m, lane, float(LANES)), axis=-1, keepdims=True)
            sel = lane == idx
            if top is None:
                top = m
            e = jnp.exp(m - top)
            den = den + e
            gates = jnp.where(sel, e, gates)
            logits = jnp.where(sel, NEG * 2.0, logits)
        gates_ref[...] = gates / den


def _resid_norm(z, gain, mod, shift_slot, scale_slot, seq_rows, ctx_rows, delta=None, gate_mod=None,
                gate_slot=None, router=None):
    t, d = z.shape
    tm = ROW_TILE
    has_delta = delta is not None
    has_router = router is not None
    row_spec = pl.BlockSpec((tm, d), lambda i: (i, 0))
    vec_spec = pl.BlockSpec((1, d), lambda i: (0, 0))

    def mod_spec(slot):
        return pl.BlockSpec((mod.shape[0], d), lambda i: (0, slot))

    operands, in_specs = [z], [row_spec]
    if has_delta:
        operands += [delta, gate_mod]
        in_specs += [row_spec, mod_spec(gate_slot)]
    operands += [gain.reshape(1, d), mod, mod]
    in_specs += [vec_spec, mod_spec(shift_slot), mod_spec(scale_slot)]
    if has_router:
        rw, rb = router
        operands += [rw, rb]
        in_specs += [pl.BlockSpec(rw.shape, lambda i: (0, 0)), pl.BlockSpec(rb.shape, lambda i: (0, 0))]
    out_shape, out_specs = [], []
    if has_delta:
        out_shape.append(jax.ShapeDtypeStruct((t, d), F32))
        out_specs.append(row_spec)
    out_shape.append(jax.ShapeDtypeStruct((t, d), BF16))
    out_specs.append(row_spec)
    if has_router:
        out_shape.append(jax.ShapeDtypeStruct((t, LANES), F32))
        out_specs.append(pl.BlockSpec((tm, LANES), lambda i: (i, 0)))
    kern = functools.partial(_resid_norm_kernel, has_delta=has_delta, has_router=has_router,
                             tiles_per_seq=seq_rows // tm, ctx_tiles=ctx_rows // tm)
    return pl.pallas_call(
        kern, grid=(t // tm,), in_specs=in_specs, out_specs=out_specs, out_shape=out_shape,
        compiler_params=_params(("parallel",)), name="resid_norm",
    )(*operands)


def _final_norm_kernel(z_ref, delta_ref, gate_ref, gain_ref, out_ref):
    row = 1 + pl.program_id(0)
    z = z_ref[...] + gate_ref[pl.ds(row, 1), :] * delta_ref[...].astype(F32)
    out_ref[...] = _rmsnorm(z, gain_ref[...])


def _final_norm(z, delta, gate_mod, gate_slot, gain, batch, seq_rows, ctx_rows):
    t, d = z.shape
    tm = ROW_TILE
    lat_tiles = (seq_rows - ctx_rows) // tm
    tiles_per_seq = seq_rows // tm
    ctx_tiles = ctx_rows // tm
    row_spec = pl.BlockSpec((tm, d), lambda b, j: (b * tiles_per_seq + ctx_tiles + j, 0))
    return pl.pallas_call(
        _final_norm_kernel,
        grid=(batch, lat_tiles),
        in_specs=[row_spec, row_spec,
                  pl.BlockSpec((gate_mod.shape[0], d), lambda b, j: (0, gate_slot)),
                  pl.BlockSpec((1, d), lambda b, j: (0, 0))],
        out_specs=pl.BlockSpec((tm, d), lambda b, j: (b * lat_tiles + j, 0)),
        out_shape=jax.ShapeDtypeStruct((batch * lat_tiles * tm, d), F32),
        compiler_params=_params(("parallel", "parallel")), name="final_norm",
    )(z, delta, gate_mod, gain.reshape(1, d))


def _matmul_kernel(a_ref, w_ref, o_ref):
    o_ref[...] = _dot(a_ref[...], w_ref[...]).astype(o_ref.dtype)


def _matmul(a, w, out_dtype, tm, tn):
    m, k = a.shape
    n = w.shape[1]
    return pl.pallas_call(
        _matmul_kernel,
        grid=(m // tm, n // tn),
        in_specs=[pl.BlockSpec((tm, k), lambda i, j: (i, 0)),
                  pl.BlockSpec((k, tn), lambda i, j: (0, j))],
        out_specs=pl.BlockSpec((tm, tn), lambda i, j: (i, j)),
        out_shape=jax.ShapeDtypeStruct((m, n), out_dtype),
        compiler_params=_params(("parallel", "arbitrary")), name="matmul",
    )(a, w)


def _tri_masks(length, reverse):
    row = lax.broadcasted_iota(jnp.int32, (length, length), 0)
    col = lax.broadcasted_iota(jnp.int32, (length, length), 1)
    if reverse:
        return col >= row, row >= col
    return col <= row, row <= col


def _mlstm_kernel(q_ref, k_ref, v_ref, gc_ref, gr_ref, bc_ref, br_ref, h_ref, ct_ref, n_ref, m_ref, *,
                  reverse, heads, dqk, dv):
    @pl.when(pl.program_id(1) == 0)
    def _():
        ct_ref[...] = jnp.zeros_like(ct_ref)
        n_ref[...] = jnp.zeros_like(n_ref)
        m_ref[...] = jnp.zeros_like(m_ref)

    length = q_ref.shape[0]
    valid, valid_t = _tri_masks(length, reverse)
    gc = gc_ref[...] + bc_ref[...]
    gr = gr_ref[...] + br_ref[...]
    d0 = 2 * heads if reverse else 0
    scale = dqk ** -0.5
    for h in range(heads):
        i_col = gc[:, d0 + h:d0 + h + 1]
        f_col = _log_sigmoid(gc[:, d0 + heads + h:d0 + heads + h + 1])
        i_row = gr[d0 + h:d0 + h + 1, :]
        f_row = _log_sigmoid(gr[d0 + heads + h:d0 + heads + h + 1, :])
        b_col = jnp.sum(jnp.where(valid, f_row, 0.0), axis=1, keepdims=True)
        b_row = jnp.sum(jnp.where(valid_t, f_col, 0.0), axis=0, keepdims=True)
        b_end = jnp.sum(f_row, axis=1, keepdims=True)
        m_prev = m_ref[h][:, :1]
        log_d = jnp.where(valid, b_col - b_row + i_row, NEG)
        m_inter = b_col + m_prev
        m_t = jnp.maximum(m_inter, jnp.max(log_d, axis=1, keepdims=True))
        w_intra = jnp.exp(log_d - m_t) * scale
        w_inter = jnp.exp(m_inter - m_t) * scale
        q = q_ref[:, h * dqk:(h + 1) * dqk]
        k = k_ref[:, h * dqk:(h + 1) * dqk]
        v = v_ref[:, h * dv:(h + 1) * dv]
        s = _dot_nt(q, k) * w_intra
        num = _dot(s.astype(BF16), v) + w_inter * _dot(q, ct_ref[h].astype(BF16))
        qn = jnp.sum(q.astype(F32) * n_ref[h], axis=1, keepdims=True)
        den = jnp.sum(s, axis=1, keepdims=True) + w_inter * qn
        h_ref[:, h * dv:(h + 1) * dv] = num / jnp.maximum(jnp.abs(den), jnp.exp(-m_t))
        log_w = b_end - b_col + i_col
        m_new = jnp.maximum(b_end + m_prev, jnp.max(log_w, axis=0, keepdims=True))
        w_end = jnp.exp(log_w - m_new)
        decay = jnp.exp(b_end + m_prev - m_new)
        wv = (w_end * v.astype(F32)).astype(BF16)
        ct_ref[h] = decay * ct_ref[h] + _dot_tn(k, wv)
        n_ref[h] = decay * n_ref[h] + jnp.sum(w_end * k.astype(F32), axis=0, keepdims=True)
        m_ref[h] = jnp.broadcast_to(m_new, (1, LANES))


def _chunk_index(i, n_chunks, reverse):
    if not reverse:
        return i
    return jnp.where(i == 0, 0, n_chunks - i)


def _mlstm(p, col0, gates_c, gates_r, bias_c, bias_r, batch, seq_rows, reverse):
    t = p.shape[0]
    length = ROW_TILE
    n_chunks = seq_rows // length
    heads = A_HEADS
    dv = 256
    dqk = 128
    qw, vw = heads * dqk, heads * dv

    def rows(b, i):
        return b * n_chunks + _chunk_index(i, n_chunks, reverse)

    kern = functools.partial(_mlstm_kernel, reverse=reverse, heads=heads, dqk=dqk, dv=dv)
    return pl.pallas_call(
        kern,
        grid=(batch, n_chunks),
        in_specs=[
            pl.BlockSpec((length, qw), lambda b, i: (rows(b, i), col0 // qw)),
            pl.BlockSpec((length, qw), lambda b, i: (rows(b, i), col0 // qw + 1)),
            pl.BlockSpec((length, vw), lambda b, i: (rows(b, i), (col0 + 2 * qw) // vw)),
            pl.BlockSpec((length, LANES), lambda b, i: (rows(b, i), 0)),
            pl.BlockSpec((4 * heads, length), lambda b, i: (0, rows(b, i))),
            pl.BlockSpec((1, LANES), lambda b, i: (0, 0)),
            pl.BlockSpec((4 * heads, 1), lambda b, i: (0, 0)),
        ],
        out_specs=pl.BlockSpec((length, vw), lambda b, i: (rows(b, i), 0)),
        out_shape=jax.ShapeDtypeStruct((t, vw), F32),
        scratch_shapes=[pltpu.VMEM((heads, dqk, dv), F32), pltpu.VMEM((heads, 1, dqk), F32),
                        pltpu.VMEM((heads, 1, LANES), F32)],
        compiler_params=_params(("parallel", "arbitrary")), name="mlstm_scan",
    )(p, p, p, gates_c, gates_r, bias_c, bias_r)


def _rope(t, cosf, sinf, half):
    if 2 * half == LANES:
        partner = pltpu.roll(t, half, axis=1)
    else:
        lane = lax.broadcasted_iota(jnp.int32, t.shape, 1)
        partner = jnp.where((lane & (2 * half - 1)) < half, pltpu.roll(t, LANES - half, axis=1),
                            pltpu.roll(t, half, axis=1))
    return t * cosf + partner * sinf


def _retention_kernel(q_ref, k_ref, v_ref, cos_ref, sin_ref, dp_ref, o_ref, s_ref, *, reverse, heads, dk, dv):
    @pl.when(pl.program_id(1) == 0)
    def _():
        s_ref[...] = jnp.zeros_like(s_ref)

    length = q_ref.shape[0]
    row = lax.broadcasted_iota(jnp.int32, (length, length), 0)
    col = lax.broadcasted_iota(jnp.int32, (length, length), 1)
    diff = (col - row if reverse else row - col).astype(F32)
    tpos = lax.broadcasted_iota(jnp.int32, (length, 1), 0).astype(F32)
    lg_all = -_softplus(dp_ref[...])
    cosf = cos_ref[...]
    sinf = sin_ref[...]
    scale = dk ** -0.5
    d = 1 if reverse else 0
    for h in range(heads):
        lg = lg_all[d:d + 1, h:h + 1]
        decay_mat = jnp.where(diff >= 0.0, jnp.exp(lg * jnp.maximum(diff, 0.0)), 0.0) * scale
        if reverse:
            read_w = jnp.exp(lg * (length - tpos))
            write_w = jnp.exp(lg * tpos)
        else:
            read_w = jnp.exp(lg * (tpos + 1.0))
            write_w = jnp.exp(lg * (length - 1.0 - tpos))
        chunk_decay = jnp.exp(lg * float(length))
        q = _rope(q_ref[:, h * dk:(h + 1) * dk].astype(F32), cosf, sinf, dk // 2).astype(BF16)
        k = _rope(k_ref[:, h * dk:(h + 1) * dk].astype(F32), cosf, sinf, dk // 2).astype(BF16)
        v = v_ref[:, h * dv:(h + 1) * dv]
        s = _dot_nt(q, k) * decay_mat
        o_ref[:, h * dv:(h + 1) * dv] = _dot(s.astype(BF16), v) + read_w * _dot(q, s_ref[h].astype(BF16))
        wv = (write_w * v.astype(F32)).astype(BF16)
        s_ref[h] = chunk_decay * s_ref[h] + scale * _dot_tn(k, wv)


def _retention(p, col0, cosf, sinf, decay_p, batch, seq_rows, reverse):
    t = p.shape[0]
    length = ROW_TILE
    n_chunks = seq_rows // length
    heads = B_HEADS
    dk, dv = 128, 256
    qw, vw = heads * dk, heads * dv

    def rows(b, i):
        return b * n_chunks + _chunk_index(i, n_chunks, reverse)

    kern = functools.partial(_retention_kernel, reverse=reverse, heads=heads, dk=dk, dv=dv)
    return pl.pallas_call(
        kern,
        grid=(batch, n_chunks),
        in_specs=[
            pl.BlockSpec((length, qw), lambda b, i: (rows(b, i), col0 // qw)),
            pl.BlockSpec((length, qw), lambda b, i: (rows(b, i), col0 // qw + 1)),
            pl.BlockSpec((length, vw), lambda b, i: (rows(b, i), (col0 + 2 * qw) // vw)),
            pl.BlockSpec((length, LANES), lambda b, i: (_chunk_index(i, n_chunks, reverse), 0)),
            pl.BlockSpec((length, LANES), lambda b, i: (_chunk_index(i, n_chunks, reverse), 0)),
            pl.BlockSpec(decay_p.shape, lambda b, i: (0, 0)),
        ],
        out_specs=pl.BlockSpec((length, vw), lambda b, i: (rows(b, i), 0)),
        out_shape=jax.ShapeDtypeStruct((t, vw), F32),
        scratch_shapes=[pltpu.VMEM((heads, dk, dv), F32)],
        compiler_params=_params(("parallel", "arbitrary")), name="retention_scan",
    )(p, p, p, cosf, sinf, decay_p)


def _headnorm_kernel(hf_ref, hb_ref, gate_ref, gain_ref, y_ref, *, heads, dv, center, silu_gate):
    g = gate_ref[...].astype(F32)
    sg = _sigmoid(g)
    gate = g * sg if silu_gate else sg
    gain = gain_ref[...]
    for h in range(heads):
        sl = slice(h * dv, (h + 1) * dv)
        x = hf_ref[:, sl] + hb_ref[:, sl]
        if center:
            x = x - jnp.mean(x, axis=-1, keepdims=True)
        y = x * lax.rsqrt(jnp.mean(x * x, axis=-1, keepdims=True) + EPS)
        y_ref[:, sl] = (gate[:, sl] * (y * gain[:, sl])).astype(y_ref.dtype)


def _headnorm(hf, hb, p, gate_col, gain, heads, center, silu_gate):
    t, w = hf.shape
    tm = ROW_TILE
    kern = functools.partial(_headnorm_kernel, heads=heads, dv=w // heads, center=center, silu_gate=silu_gate)
    row_spec = pl.BlockSpec((tm, w), lambda i: (i, 0))
    return pl.pallas_call(
        kern, grid=(t // tm,),
        in_specs=[row_spec, row_spec, pl.BlockSpec((tm, w), lambda i: (i, gate_col // w)),
                  pl.BlockSpec((1, w), lambda i: (0, 0))],
        out_specs=row_spec,
        out_shape=jax.ShapeDtypeStruct((t, w), BF16),
        compiler_params=_params(("parallel",)), name="headnorm_gate",
    )(hf, hb, p, gain.reshape(1, w))


def _window_kernel(sink_ref, q_ref, kp_ref, k0_ref, kn_ref, vp_ref, v0_ref, vn_ref, kc_ref, vc_ref,
                   cq_ref, sq_ref, cp_ref, sp_ref, cn_ref, sn_ref, o_ref, *, heads, kv_heads, hd, ctx_blocks,
                   n_latent):
    i = pl.program_id(1)
    j = i - ctx_blocks
    w = q_ref.shape[0]
    half = hd // 2
    per_slab = LANES // hd
    rep = heads // kv_heads

    def roped(x_ref, c_ref, s_ref):
        return _rope(x_ref[...].astype(F32), c_ref[...], s_ref[...], half).astype(BF16)

    kb = jnp.concatenate([roped(kp_ref, cp_ref, sp_ref), roped(k0_ref, cq_ref, sq_ref),
                          roped(kn_ref, cn_ref, sn_ref)], axis=0)
    vb = jnp.concatenate([vp_ref[...], v0_ref[...], vn_ref[...]], axis=0)
    kc = kc_ref[...]
    vc = vc_ref[...]
    t = lax.broadcasted_iota(jnp.int32, (w, 3 * w), 0)
    c = lax.broadcasted_iota(jnp.int32, (w, 3 * w), 1)
    lower = jnp.maximum(t + (w - WINDOW), (1 - j) * w)
    upper = jnp.minimum(t + (w + WINDOW), n_latent - 1 - (j - 1) * w)
    upper = jnp.where(j < 0, -1, upper)
    valid = (c >= lower) & (c <= upper)
    scale = hd ** -0.5
    cq = cq_ref[...]
    sq = sq_ref[...]
    for slab in range(heads // per_slab):
        qs = _rope(q_ref[:, slab * LANES:(slab + 1) * LANES].astype(F32), cq, sq, half)
        qs = (qs * scale).astype(BF16)
        outs = []
        for sub in range(per_slab):
            h = slab * per_slab + sub
            g = h // rep
            gs = slice(g * hd, (g + 1) * hd)
            qh = qs[:, sub * hd:(sub + 1) * hd]
            s_loc = jnp.where(valid, _dot_nt(qh, kb[:, gs]), NEG)
            s_ctx = _dot_nt(qh, kc[:, gs])
            sink = sink_ref[h]
            m = jnp.maximum(jnp.maximum(jnp.max(s_loc, axis=-1, keepdims=True),
                                        jnp.max(s_ctx, axis=-1, keepdims=True)), sink)
            e_loc = jnp.exp(s_loc - m)
            e_ctx = jnp.exp(s_ctx - m)
            den = (jnp.exp(sink - m) + jnp.sum(e_loc, axis=-1, keepdims=True)
                   + jnp.sum(e_ctx, axis=-1, keepdims=True))
            o = _dot(e_loc.astype(BF16), vb[:, gs]) + _dot(e_ctx.astype(BF16), vc[:, gs])
            outs.append(o / den)
        o_ref[:, slab * LANES:(slab + 1) * LANES] = jnp.concatenate(outs, axis=-1).astype(o_ref.dtype)


def _window_attention(p, col0, cosf, sinf, sink, batch, seq_rows, ctx_rows):
    t = p.shape[0]
    w = WINDOW
    heads, kv_heads = C_HEADS, C_KV_HEADS
    hd = 64
    qw = heads * hd
    nblk = seq_rows // w
    ctx_blocks = ctx_rows // w
    kcol = (col0 + qw) // LANES
    vcol = kcol + 1

    def cur(b, i):
        return b * nblk + i

    def prev(b, i):
        return b * nblk + jnp.maximum(i - 1, 0)

    def nxt(b, i):
        return b * nblk + jnp.minimum(i + 1, nblk - 1)

    def kv_spec(fn, colblk):
        return pl.BlockSpec((w, LANES), lambda b, i: (fn(b, i), colblk))

    def tab_spec(fn):
        return pl.BlockSpec((w, LANES), lambda b, i: (fn(0, i), 0))

    kern = functools.partial(_window_kernel, heads=heads, kv_heads=kv_heads, hd=hd, ctx_blocks=ctx_blocks,
                             n_latent=seq_rows - ctx_rows)
    ctx_spec_k = pl.BlockSpec((ctx_rows, LANES), lambda b, i: (b * (seq_rows // ctx_rows), kcol))
    ctx_spec_v = pl.BlockSpec((ctx_rows, LANES), lambda b, i: (b * (seq_rows // ctx_rows), vcol))
    return pl.pallas_call(
        kern,
        grid=(batch, nblk),
        in_specs=[
            pl.BlockSpec(memory_space=pltpu.SMEM),
            pl.BlockSpec((w, qw), lambda b, i: (cur(b, i), col0 // qw)),
            kv_spec(prev, kcol), kv_spec(cur, kcol), kv_spec(nxt, kcol),
            kv_spec(prev, vcol), kv_spec(cur, vcol), kv_spec(nxt, vcol),
            ctx_spec_k, ctx_spec_v,
            tab_spec(cur), tab_spec(cur), tab_spec(prev), tab_spec(prev), tab_spec(nxt), tab_spec(nxt),
        ],
        out_specs=pl.BlockSpec((w, qw), lambda b, i: (cur(b, i), 0)),
        out_shape=jax.ShapeDtypeStruct((t, qw), BF16),
        compiler_params=_params(("parallel", "arbitrary")), name="window_attention",
    )(sink, p, p, p, p, p, p, p, p, p, cosf, sinf, cosf, sinf, cosf, sinf)


def _nbr_kernel(q_ref, k_ref, v_ref, bias_ref, o_ref, *, hd, ctx_rows, grid_rows, rows_per_step):
    i = pl.program_id(2)
    per_slab = LANES // hd
    scale = hd ** -0.5
    win = NA_ROWS * GRID_W
    q = q_ref[...] * scale
    kc = k_ref[0:ctx_rows, :]
    vc = v_ref[0:ctx_rows, :]

    @pl.when(i == 0)
    def _():
        outs = []
        for sub in range(per_slab):
            hs = slice(sub * hd, (sub + 1) * hd)
            s = _dot_nt(q[:, hs], kc[:, hs])
            e = jnp.exp(s - jnp.max(s, axis=-1, keepdims=True))
            outs.append(_dot(e.astype(BF16), vc[:, hs]) / jnp.sum(e, axis=-1, keepdims=True))
        o_ref[...] = jnp.concatenate(outs, axis=-1).astype(o_ref.dtype)

    @pl.when(i > 0)
    def _():
        for rr in range(rows_per_step):
            r = (i - 1) * rows_per_step + rr
            start = jnp.clip(r - NA_ROWS // 2, 0, grid_rows - NA_ROWS)
            off = r - start
            base = pl.multiple_of(ctx_rows + start * GRID_W, GRID_W)
            kw = k_ref[pl.ds(base, win), :]
            vw = v_ref[pl.ds(base, win), :]
            qr = q[rr * GRID_W:(rr + 1) * GRID_W, :]
            outs = []
            for sub in range(per_slab):
                hs = slice(sub * hd, (sub + 1) * hd)
                s_loc = _dot_nt(qr[:, hs], kw[:, hs]) + bias_ref[sub, off]
                s_ctx = _dot_nt(qr[:, hs], kc[:, hs])
                m = jnp.maximum(jnp.max(s_loc, axis=-1, keepdims=True), jnp.max(s_ctx, axis=-1, keepdims=True))
                e_loc = jnp.exp(s_loc - m)
                e_ctx = jnp.exp(s_ctx - m)
                den = jnp.sum(e_loc, axis=-1, keepdims=True) + jnp.sum(e_ctx, axis=-1, keepdims=True)
                o = _dot(e_loc.astype(BF16), vw[:, hs]) + _dot(e_ctx.astype(BF16), vc[:, hs])
                outs.append(o / den)
            o_ref[rr * GRID_W:(rr + 1) * GRID_W, :] = jnp.concatenate(outs, axis=-1).astype(o_ref.dtype)


def _nbr_bias(na_bias_l, grid_rows):
    wr = min(NA_ROWS, grid_rows)
    col = jnp.arange(GRID_W)
    col_start = jnp.clip(col - NA_COLS // 2, 0, GRID_W - NA_COLS)
    col_valid = (col[None, :] >= col_start[:, None]) & (col[None, :] < col_start[:, None] + NA_COLS)
    dc = jnp.clip(col[None, :] - col[:, None], -(NA_COLS - 1), NA_COLS - 1) + (NA_COLS - 1)
    off = jnp.arange(NA_ROWS)
    dr = jnp.clip(jnp.arange(wr)[None, :] - off[:, None] + (NA_ROWS - 1), 0, 2 * NA_ROWS - 2)
    tab = na_bias_l[:, dr]
    bias = jnp.take(tab, dc, axis=-1)
    bias = jnp.where(col_valid[None, None, None], bias, NEG)
    bias = bias.transpose(0, 1, 3, 2, 4)
    return bias.reshape(na_bias_l.shape[0], NA_ROWS, GRID_W, wr * GRID_W).astype(F32)


def _nbr_attention(p, col0, bias_mat, batch, seq_rows, ctx_rows):
    t = p.shape[0]
    heads = D_HEADS
    hd = 64
    qw = heads * hd
    per_slab = LANES // hd
    slabs = heads // per_slab
    tq = ROW_TILE
    rows_per_step = tq // GRID_W
    n_latent = seq_rows - ctx_rows
    grid_rows = n_latent // GRID_W
    steps = seq_rows // tq
    q0 = col0 // LANES
    kern = functools.partial(_nbr_kernel, hd=hd, ctx_rows=ctx_rows, grid_rows=grid_rows,
                             rows_per_step=rows_per_step)
    return pl.pallas_call(
        kern,
        grid=(batch, slabs, steps),
        in_specs=[
            pl.BlockSpec((tq, LANES), lambda b, s, i: (b * steps + i, q0 + s)),
            pl.BlockSpec((seq_rows, LANES), lambda b, s, i: (b, q0 + slabs + s)),
            pl.BlockSpec((seq_rows, LANES), lambda b, s, i: (b, q0 + 2 * slabs + s)),
            pl.BlockSpec((per_slab,) + bias_mat.shape[1:], lambda b, s, i: (s, 0, 0, 0)),
        ],
        out_specs=pl.BlockSpec((tq, LANES), lambda b, s, i: (b * steps + i, s)),
        out_shape=jax.ShapeDtypeStruct((t, qw), BF16),
        compiler_params=_params(("parallel", "parallel", "arbitrary")), name="nbr_attention",
    )(p, p, p, bias_mat)


def _merge_kernel(*refs):
    ys = refs[0:N_BRANCH]
    gs = refs[N_BRANCH:2 * N_BRANCH]
    ws = refs[2 * N_BRANCH:3 * N_BRANCH]
    o_ref = refs[3 * N_BRANCH]
    acc = None
    for y_ref, g_ref, w_ref in zip(ys, gs, ws):
        term = _sigmoid(g_ref[...].astype(F32)) * _dot(y_ref[...], w_ref[0])
        acc = term if acc is None else acc + term
    o_ref[...] = acc.astype(o_ref.dtype)


def _merge(ys, p, gate_col0, w_branch, tm, tn):
    t, bw = ys[0].shape
    d = w_branch.shape[2]
    nt = d // tn
    y_spec = pl.BlockSpec((tm, bw), lambda i, j: (i, 0))
    g_specs = [pl.BlockSpec((tm, tn), functools.partial(lambda i, j, br: (i, (gate_col0 + br * d) // tn + j), br=br))
               for br in range(N_BRANCH)]
    w_specs = [pl.BlockSpec((1, bw, tn), functools.partial(lambda i, j, br: (br, 0, j), br=br))
               for br in range(N_BRANCH)]
    return pl.pallas_call(
        _merge_kernel,
        grid=(t // tm, nt),
        in_specs=[y_spec] * N_BRANCH + g_specs + w_specs,
        out_specs=pl.BlockSpec((tm, tn), lambda i, j: (i, j)),
        out_shape=jax.ShapeDtypeStruct((t, d), BF16),
        compiler_params=_params(("parallel", "arbitrary")), name="merge_branches",
    )(*ys, *([p] * N_BRANCH), *([w_branch] * N_BRANCH))


def _moe_kernel(t_ref, gates_ref, w1_ref, b1_ref, w2_ref, b2_ref, o_ref, *, ff, group):
    e = pl.program_id(1)
    gates = gates_ref[...]

    @pl.when(e == 0)
    def _():
        o_ref[...] = _dot(gates.astype(BF16), b2_ref[...])

    hid = _dot(t_ref[...], w1_ref[0]) + b1_ref[0]
    gw = group * ff
    g_h = jnp.minimum(hid[:, :gw], SWIGLU_LIMIT)
    u_h = jnp.clip(hid[:, gw:], -SWIGLU_LIMIT, SWIGLU_LIMIT)
    lane = lax.broadcasted_iota(jnp.int32, gates.shape, 1)
    cols = []
    for s in range(group):
        gsel = jnp.sum(jnp.where(lane == e * group + s, gates, 0.0), axis=-1, keepdims=True)
        cols.append(jnp.broadcast_to(gsel, (gates.shape[0], ff)))
    gate_cols = jnp.concatenate(cols, axis=-1)
    act = g_h * _sigmoid(SWIGLU_ALPHA * g_h) * (u_h + 1.0) * gate_cols
    o_ref[...] += _dot(act.astype(BF16), w2_ref[0])


def _moe(h, gates, w1p, b1p, w2p, b2p, tm):
    t, d = h.shape
    groups, _, width = w1p.shape
    group = 2
    ff = width // (2 * group)
    kern = functools.partial(_moe_kernel, ff=ff, group=group)
    return pl.pallas_call(
        kern,
        grid=(t // tm, groups),
        in_specs=[
            pl.BlockSpec((tm, d), lambda i, e: (i, 0)),
            pl.BlockSpec((tm, LANES), lambda i, e: (i, 0)),
            pl.BlockSpec((1, d, width), lambda i, e: (e, 0, 0)),
            pl.BlockSpec((1, 1, width), lambda i, e: (e, 0, 0)),
            pl.BlockSpec((1, group * ff, d), lambda i, e: (e, 0, 0)),
            pl.BlockSpec(b2p.shape, lambda i, e: (0, 0)),
        ],
        out_specs=pl.BlockSpec((tm, d), lambda i, e: (i, 0)),
        out_shape=jax.ShapeDtypeStruct((t, d), F32),
        compiler_params=_params(("parallel", "arbitrary")), name="moe_experts",
    )(h, gates, w1p, b1p, w2p, b2p)


def _rope_tables(n_latent, ctx_rows, hd):
    pos = np.arange(n_latent)
    row = (pos // GRID_W).astype(np.float32)
    col = (pos % GRID_W).astype(np.float32)
    n_freq = hd // 4
    inv = jnp.asarray(ROPE_BASE, F32) ** (-jnp.arange(n_freq, dtype=F32) / n_freq)
    ang = jnp.concatenate([jnp.asarray(row)[:, None] * inv, jnp.asarray(col)[:, None] * inv], axis=-1)
    ang = jnp.concatenate([jnp.zeros((ctx_rows, hd // 2), F32), ang], axis=0)
    cos, sin = jnp.cos(ang), jnp.sin(ang)
    reps = LANES // hd
    cosf = jnp.tile(jnp.concatenate([cos, cos], axis=-1), (1, reps))
    sinf = jnp.tile(jnp.concatenate([-sin, sin], axis=-1), (1, reps))
    return cosf, sinf


def _pad_lanes(a, value=0.0):
    pad = LANES - a.shape[-1]
    return jnp.pad(a, [(0, 0)] * (a.ndim - 1) + [(0, pad)], constant_values=value)


def kernel(x, c, ctx, c_ctx, ada_down, ada_up, ada_bias, norm_gain, w_in, mlstm_gate_bias, mlstm_norm_gain,
           ret_decay, ret_norm_gain, sink, na_bias, w_branch, w_out, router_w, router_b, moe_w1, moe_b1,
           moe_w2, moe_b2, final_gain):
    batch, n_latent, d = x.shape
    ctx_rows = ctx.shape[1]
    depth = w_in.shape[0]
    seq_rows = ctx_rows + n_latent
    t = batch * seq_rows
    bw = d // N_BRANCH
    n_experts = router_w.shape[2]
    ff = moe_w2.shape[2]

    a_sz = (A_HEADS * 128, A_HEADS * 128, bw, bw)
    n_gate = 4 * A_HEADS
    b_sz = (B_HEADS * 128, B_HEADS * 128, bw, bw)
    c_sz = (bw, C_KV_HEADS * 64, C_KV_HEADS * 64)
    d_sz = (bw, bw, bw)
    sizes = a_sz + (n_gate,) + b_sz + c_sz + d_sz + (N_BRANCH * d,)
    offs = np.concatenate([[0], np.cumsum(sizes)])
    a0, g0, b0, c0, d0, bg0 = offs[0], offs[4], offs[5], offs[9], offs[12], offs[15]
    col_gate = 0
    col_a = N_BRANCH * d
    col_b = col_a + sum(a_sz)
    col_d = col_b + sum(b_sz)
    col_c = col_d + sum(d_sz)

    mod_all = _ada_modulation(jnp.pad(jnp.concatenate([c_ctx[None], c], axis=0), ((0, 8 - 1 - batch), (0, 0))),
                              ada_down, ada_up, ada_bias)
    cos_b, sin_b = _rope_tables(n_latent, ctx_rows, 128)
    cos_c, sin_c = _rope_tables(n_latent, ctx_rows, 64)
    decay_p = jnp.pad(ret_decay, ((0, 0), (0, 8 - ret_decay.shape[1]), (0, LANES - ret_decay.shape[2])))

    z = jnp.concatenate([ctx, x], axis=1).reshape(t, d)
    tm_proj = _row_tile(t, 768)
    tm_moe = _row_tile(t, 512)
    delta = None
    for l in range(depth):
        wl = w_in[l]
        w_cat = jnp.concatenate([wl[:, bg0:], wl[:, a0:g0], wl[:, b0:c0], wl[:, d0:bg0], wl[:, c0:d0]],
                                axis=1).astype(BF16)
        w_g = _pad_lanes(wl[:, g0:b0]).astype(BF16)
        mod = mod_all[l]
        if l == 0:
            h = _resid_norm(z, norm_gain[l, 0], mod, 0, 1, seq_rows, ctx_rows)[0]
        else:
            z, h = _resid_norm(z, norm_gain[l, 0], mod, 0, 1, seq_rows, ctx_rows, delta=delta,
                               gate_mod=mod_all[l - 1], gate_slot=5)
        p = _matmul(h, w_cat, BF16, tm=tm_proj, tn=768)
        gates_c = _matmul(h, w_g, F32, tm=tm_proj, tn=LANES)
        gates_r = gates_c[:, :n_gate].T
        bias_c = _pad_lanes(mlstm_gate_bias[l][None])
        bias_r = mlstm_gate_bias[l][:, None]

        hf = _mlstm(p, col_a, gates_c, gates_r, bias_c, bias_r, batch, seq_rows, False)
        hb = _mlstm(p, col_a, gates_c, gates_r, bias_c, bias_r, batch, seq_rows, True)
        y_a = _headnorm(hf, hb, p, col_a + 2 * A_HEADS * 128 + bw, mlstm_norm_gain[l], A_HEADS, False, False)
        of = _retention(p, col_b, cos_b, sin_b, decay_p[l], batch, seq_rows, False)
        ob = _retention(p, col_b, cos_b, sin_b, decay_p[l], batch, seq_rows, True)
        y_b = _headnorm(of, ob, p, col_b + 2 * B_HEADS * 128 + bw, ret_norm_gain[l], B_HEADS, True, True)
        y_c = _window_attention(p, col_c, cos_c, sin_c, sink[l], batch, seq_rows, ctx_rows)
        y_d = _nbr_attention(p, col_d, _nbr_bias(na_bias[l], n_latent // GRID_W), batch, seq_rows, ctx_rows)

        acc = _merge([y_a, y_b, y_c, y_d], p, col_gate, w_branch[l].astype(BF16), tm=tm_moe, tn=1024)
        delta1 = _matmul(acc, w_out[l].astype(BF16), F32, tm=tm_proj, tn=1024)

        rw = _pad_lanes(router_w[l]).astype(BF16)
        rb = _pad_lanes(router_b[l][None], NEG)
        z, h2, gates = _resid_norm(z, norm_gain[l, 1], mod, 3, 4, seq_rows, ctx_rows, delta=delta1,
                                   gate_mod=mod, gate_slot=2, router=(rw, rb))

        w1 = moe_w1[l]
        pair = lambda a: a.reshape(n_experts // 2, 2, d, ff).transpose(0, 2, 1, 3).reshape(n_experts // 2, d, 2 * ff)
        w1p = jnp.concatenate([pair(w1[..., 0::2]), pair(w1[..., 1::2])], axis=-1).astype(BF16)
        b1 = moe_b1[l]
        b1p = jnp.concatenate([b1[:, 0::2].reshape(n_experts // 2, 1, 2 * ff),
                               b1[:, 1::2].reshape(n_experts // 2, 1, 2 * ff)], axis=-1)
        w2p = moe_w2[l].reshape(n_experts // 2, 2 * ff, d).astype(BF16)
        b2p = jnp.pad(moe_b2[l], ((0, LANES - n_experts), (0, 0))).astype(BF16)
        delta = _moe(h2, gates, w1p, b1p, w2p, b2p, tm=tm_moe)

    out = _final_norm(z, delta, mod_all[depth - 1], 5, final_gain, batch, seq_rows, ctx_rows)
    return out.reshape(batch, n_latent, d)
```

```python
import functools

import jax
import jax.numpy as jnp
import numpy as np
from jax import lax
from jax.experimental import pallas as pl
from jax.experimental.pallas import tpu as pltpu

F32 = jnp.float32
BF16 = jnp.bfloat16

GRID_W = 64
EPS = 1e-6
NEG = -1e30
ROPE_BASE = 10000.0
N_BRANCH = 4
A_HEADS = 4
B_HEADS = 4
C_HEADS = 16
C_KV_HEADS = 2
D_HEADS = 16
WINDOW = 128
NA_ROWS = 8
NA_COLS = 16
TOP_K = 4
SWIGLU_LIMIT = 7.0
SWIGLU_ALPHA = 1.702

LANES = 128
VMEM_LIMIT = 56 * 1024 * 1024
ROW_TILE = 256
NBR_ROWS = 4
NBR_UNION = NBR_ROWS + NA_ROWS - 1


def _row_tile(rows, target):
    best = ROW_TILE
    for tile in range(ROW_TILE, target + 1, ROW_TILE):
        if rows % tile == 0:
            best = tile
    assert rows % best == 0
    return best


def _params(sem, vmem=VMEM_LIMIT):
    return pltpu.CompilerParams(dimension_semantics=sem, vmem_limit_bytes=vmem)


def _dot(a, b):
    return jnp.dot(a, b, preferred_element_type=F32)


def _dot_nt(a, b):
    return lax.dot_general(a, b, (((1,), (1,)), ((), ())), preferred_element_type=F32)


def _dot_tn(a, b):
    return lax.dot_general(a, b, (((0,), (0,)), ((), ())), preferred_element_type=F32)


def _sigmoid(x):
    return 1.0 / (1.0 + jnp.exp(-x))


def _softplus(x):
    return jnp.maximum(x, 0.0) + jnp.log1p(jnp.exp(-jnp.abs(x)))


def _log_sigmoid(x):
    return -_softplus(-x)


def _win_prep_kernel(w_ref, cat_ref, g_ref, gt_ref, *, segments, gate_seg):
    col = 0
    for a, b in segments:
        cat_ref[0, :, col:col + (b - a)] = w_ref[0, :, a:b].astype(BF16)
        col += b - a
    ga, gb = gate_seg
    g = w_ref[0, :, ga:gb]
    gpad = jnp.concatenate([g, jnp.zeros((g.shape[0], LANES - (gb - ga)), F32)], axis=1)
    g_ref[0] = gpad.astype(BF16)
    gt_ref[0] = gpad.T.astype(BF16)


def _win_prep(w_in, segments, gate_seg):
    depth, d, width = w_in.shape
    rows = LANES
    n_cat = sum(b - a for a, b in segments)
    kern = functools.partial(_win_prep_kernel, segments=segments, gate_seg=gate_seg)
    return pl.pallas_call(
        kern, grid=(depth, d // rows),
        in_specs=[pl.BlockSpec((1, rows, width), lambda l, i: (l, i, 0))],
        out_specs=[pl.BlockSpec((1, rows, n_cat), lambda l, i: (l, i, 0)),
                   pl.BlockSpec((1, rows, LANES), lambda l, i: (l, i, 0)),
                   pl.BlockSpec((1, LANES, rows), lambda l, i: (l, 0, i))],
        out_shape=[jax.ShapeDtypeStruct((depth, d, n_cat), BF16),
                   jax.ShapeDtypeStruct((depth, d, LANES), BF16),
                   jax.ShapeDtypeStruct((depth, LANES, d), BF16)],
        compiler_params=_params(("parallel", "parallel")), name="win_prep",
    )(w_in)


def _moe_w1_prep_kernel(w_ref, perm_ref, o_ref, *, ff):
    parts = [_dot(w_ref[0, s].astype(BF16), perm_ref[...]) for s in range(2)]
    o_ref[0] = jnp.concatenate([parts[0][:, :ff], parts[1][:, :ff], parts[0][:, ff:], parts[1][:, ff:]],
                               axis=1).astype(BF16)


def _moe_w1_prep(moe_w1):
    depth, n_exp, d, two_ff = moe_w1.shape
    ff = two_ff // 2
    rows = 1024
    src = np.concatenate([np.arange(0, two_ff, 2), np.arange(1, two_ff, 2)])
    perm = jnp.asarray(np.arange(two_ff)[:, None] == src[None, :], BF16)
    kern = functools.partial(_moe_w1_prep_kernel, ff=ff)
    return pl.pallas_call(
        kern, grid=(depth * n_exp // 2, d // rows),
        in_specs=[pl.BlockSpec((1, 2, rows, two_ff), lambda e, i: (e, 0, i, 0)),
                  pl.BlockSpec((two_ff, two_ff), lambda e, i: (0, 0))],
        out_specs=pl.BlockSpec((1, rows, 2 * two_ff), lambda e, i: (e, i, 0)),
        out_shape=jax.ShapeDtypeStruct((depth * n_exp // 2, d, 2 * two_ff), BF16),
        compiler_params=_params(("parallel", "parallel")), name="moe_w1_prep",
    )(moe_w1.reshape(depth * n_exp // 2, 2, d, two_ff), perm)


def _cast_kernel(*refs):
    n = len(refs) // 2
    for src, dst in zip(refs[:n], refs[n:]):
        dst[...] = src[...].astype(dst.dtype)


def _cast_bf16(*arrays):
    rows, cols = arrays[0].shape
    tm = ROW_TILE
    spec = pl.BlockSpec((tm, cols), lambda i: (i, 0))
    return pl.pallas_call(
        _cast_kernel, grid=(rows // tm,),
        in_specs=[spec] * len(arrays), out_specs=[spec] * len(arrays),
        out_shape=[jax.ShapeDtypeStruct((rows, cols), BF16)] * len(arrays),
        compiler_params=_params(("parallel",)), name="cast_bf16",
    )(*arrays)


def _ada_kernel(cv_ref, down_ref, up_ref, bias_ref, out_ref):
    cv = cv_ref[...]
    a = cv * _sigmoid(cv)
    z = _dot(a.astype(BF16), down_ref[0].astype(BF16))
    out_ref[0] = _dot(z.astype(BF16), up_ref[0].astype(BF16)) + bias_ref[0]


def _ada_modulation(cvecs, ada_down, ada_up, ada_bias):
    depth, d, r = ada_down.shape
    rows = cvecs.shape[0]
    return pl.pallas_call(
        _ada_kernel,
        grid=(depth, 6),
        in_specs=[
            pl.BlockSpec((rows, d), lambda l, j: (0, 0)),
            pl.BlockSpec((1, d, r), lambda l, j: (l, 0, 0)),
            pl.BlockSpec((1, r, d), lambda l, j: (l, 0, j)),
            pl.BlockSpec((1, 1, d), lambda l, j: (l, 0, j)),
        ],
        out_specs=pl.BlockSpec((1, rows, d), lambda l, j: (l, 0, j)),
        out_shape=jax.ShapeDtypeStruct((depth, rows, 6 * d), F32),
        compiler_params=_params(("parallel", "arbitrary")),
        name="ada_modulation",
    )(cvecs, ada_down, ada_up, ada_bias.reshape(depth, 1, 6 * d))


def _rmsnorm(x, gain):
    return x * lax.rsqrt(jnp.mean(x * x, axis=-1, keepdims=True) + EPS) * gain


def _resid_norm_kernel(*refs, split_src, has_delta, has_router, tiles_per_seq, ctx_tiles):
    refs = list(refs)
    tile = pl.program_id(0)
    b = tile // tiles_per_seq
    j = tile - b * tiles_per_seq
    is_ctx = j < ctx_tiles
    row = jnp.where(is_ctx, 0, 1 + b)
    if split_src:
        c_ref = refs.pop(0)
        x_ref = refs.pop(0)
        z = jnp.where(is_ctx, c_ref[...], x_ref[...])
    else:
        z = refs.pop(0)[...]
    if has_delta:
        delta_ref = refs.pop(0)
        gate_ref = refs.pop(0)
    gain_ref = refs.pop(0)
    shift_ref = refs.pop(0)
    scale_ref = refs.pop(0)
    if has_router:
        rw_ref = refs.pop(0)
        rb_ref = refs.pop(0)
    if has_delta:
        znew_ref = refs.pop(0)
    h_ref = refs.pop(0)
    if has_router:
        gates_ref = refs.pop(0)

    if has_delta:
        z = z + gate_ref[pl.ds(row, 1), :] * delta_ref[...].astype(F32)
        znew_ref[...] = z
    y = _rmsnorm(z, gain_ref[...])
    h = y * (1.0 + scale_ref[pl.ds(row, 1), :]) + shift_ref[pl.ds(row, 1), :]
    hb = h.astype(BF16)
    h_ref[...] = hb
    if has_router:
        logits = _dot(hb, rw_ref[...]) + rb_ref[...]
        lane = lax.broadcasted_iota(jnp.int32, logits.shape, 1).astype(F32)
        gates = jnp.zeros_like(logits)
        den = jnp.zeros((logits.shape[0], 1), F32)
        top = None
        for _ in range(TOP_K):
            m = jnp.max(logits, axis=-1, keepdims=True)
            idx = jnp.min(jnp.where(logits == m, lane, float(LANES)), axis=-1, keepdims=True)
            sel = lane == idx
            if top is None:
                top = m
            e = jnp.exp(m - top)
            den = den + e
            gates = jnp.where(sel, e, gates)
            logits = jnp.where(sel, NEG * 2.0, logits)
        gates_ref[...] = gates / den


def _resid_norm(src, gain, mod, shift_slot, scale_slot, batch, seq_rows, ctx_rows, delta=None, gate_mod=None,
                gate_slot=None, router=None):
    split_src = isinstance(src, tuple)
    d = src[0].shape[1] if split_src else src.shape[1]
    t = batch * seq_rows
    tm = ROW_TILE
    tiles_per_seq = seq_rows // tm
    ctx_tiles = ctx_rows // tm
    lat_tiles = tiles_per_seq - ctx_tiles
    has_delta = delta is not None
    has_router = router is not None
    row_spec = pl.BlockSpec((tm, d), lambda i: (i, 0))
    vec_spec = pl.BlockSpec((1, d), lambda i: (0, 0))

    def mod_spec(slot):
        return pl.BlockSpec((mod.shape[0], d), lambda i: (0, slot))

    def ctx_map(i):
        b = i // tiles_per_seq
        return b * ctx_tiles + jnp.minimum(i - b * tiles_per_seq, ctx_tiles - 1), 0

    def lat_map(i):
        b = i // tiles_per_seq
        return b * lat_tiles + jnp.maximum(i - b * tiles_per_seq - ctx_tiles, 0), 0

    if split_src:
        operands, in_specs = list(src), [pl.BlockSpec((tm, d), ctx_map), pl.BlockSpec((tm, d), lat_map)]
    else:
        operands, in_specs = [src], [row_spec]
    if has_delta:
        operands += [delta, gate_mod]
        in_specs += [row_spec, mod_spec(gate_slot)]
    operands += [gain.reshape(1, d), mod, mod]
    in_specs += [vec_spec, mod_spec(shift_slot), mod_spec(scale_slot)]
    if has_router:
        rw, rb = router
        operands += [rw, rb]
        in_specs += [pl.BlockSpec(rw.shape, lambda i: (0, 0)), pl.BlockSpec(rb.shape, lambda i: (0, 0))]
    out_shape, out_specs = [], []
    if has_delta:
        out_shape.append(jax.ShapeDtypeStruct((t, d), F32))
        out_specs.append(row_spec)
    out_shape.append(jax.ShapeDtypeStruct((t, d), BF16))
    out_specs.append(row_spec)
    if has_router:
        out_shape.append(jax.ShapeDtypeStruct((t, LANES), F32))
        out_specs.append(pl.BlockSpec((tm, LANES), lambda i: (i, 0)))
    kern = functools.partial(_resid_norm_kernel, split_src=split_src, has_delta=has_delta, has_router=has_router,
                             tiles_per_seq=tiles_per_seq, ctx_tiles=ctx_tiles)
    return pl.pallas_call(
        kern, grid=(t // tm,), in_specs=in_specs, out_specs=out_specs, out_shape=out_shape,
        compiler_params=_params(("parallel",)), name="resid_norm",
    )(*operands)


def _final_norm_kernel(z_ref, delta_ref, gate_ref, gain_ref, out_ref):
    row = 1 + pl.program_id(0)
    z = z_ref[...] + gate_ref[pl.ds(row, 1), :] * delta_ref[...].astype(F32)
    out_ref[...] = _rmsnorm(z, gain_ref[...])


def _final_norm(z, delta, gate_mod, gate_slot, gain, batch, seq_rows, ctx_rows):
    t, d = z.shape
    tm = ROW_TILE
    lat_tiles = (seq_rows - ctx_rows) // tm
    tiles_per_seq = seq_rows // tm
    ctx_tiles = ctx_rows // tm
    row_spec = pl.BlockSpec((tm, d), lambda b, j: (b * tiles_per_seq + ctx_tiles + j, 0))
    return pl.pallas_call(
        _final_norm_kernel,
        grid=(batch, lat_tiles),
        in_specs=[row_spec, row_spec,
                  pl.BlockSpec((gate_mod.shape[0], d), lambda b, j: (0, gate_slot)),
                  pl.BlockSpec((1, d), lambda b, j: (0, 0))],
        out_specs=pl.BlockSpec((tm, d), lambda b, j: (b * lat_tiles + j, 0)),
        out_shape=jax.ShapeDtypeStruct((batch * lat_tiles * tm, d), F32),
        compiler_params=_params(("parallel", "parallel")), name="final_norm",
    )(z, delta, gate_mod, gain.reshape(1, d))


def _matmul_kernel(a_ref, w_ref, o_ref):
    o_ref[...] = _dot(a_ref[...], w_ref[...]).astype(o_ref.dtype)


def _matmul(a, w, layer, out_dtype, tm, tn):
    m, k = a.shape
    n = w.shape[2]
    return pl.pallas_call(
        _matmul_kernel,
        grid=(m // tm, n // tn),
        in_specs=[pl.BlockSpec((tm, k), lambda i, j: (i, 0)),
                  pl.BlockSpec((None, k, tn), lambda i, j: (layer, 0, j))],
        out_specs=pl.BlockSpec((tm, tn), lambda i, j: (i, j)),
        out_shape=jax.ShapeDtypeStruct((m, n), out_dtype),
        compiler_params=_params(("parallel", "arbitrary")), name="matmul",
    )(a, w)


def _gate_proj_kernel(a_ref, w_ref, wt_ref, oc_ref, or_ref):
    a = a_ref[...]
    oc_ref[...] = _dot(a, w_ref[...])
    or_ref[...] = _dot_nt(wt_ref[...], a)


def _gate_proj(a, w, wt, layer, tm):
    m, k = a.shape
    return pl.pallas_call(
        _gate_proj_kernel,
        grid=(m // tm,),
        in_specs=[pl.BlockSpec((tm, k), lambda i: (i, 0)),
                  pl.BlockSpec((None, k, LANES), lambda i: (layer, 0, 0)),
                  pl.BlockSpec((None, LANES, k), lambda i: (layer, 0, 0))],
        out_specs=[pl.BlockSpec((tm, LANES), lambda i: (i, 0)), pl.BlockSpec((LANES, tm), lambda i: (0, i))],
        out_shape=[jax.ShapeDtypeStruct((m, LANES), F32), jax.ShapeDtypeStruct((LANES, m), F32)],
        compiler_params=_params(("parallel",)), name="gate_proj",
    )(a, w, wt)


def _tri_masks(length, reverse):
    row = lax.broadcasted_iota(jnp.int32, (length, length), 0)
    col = lax.broadcasted_iota(jnp.int32, (length, length), 1)
    if reverse:
        return col >= row, row >= col
    return col <= row, row <= col


def _mlstm_kernel(q_ref, k_ref, v_ref, gc_ref, gr_ref, bc_ref, br_ref, h_ref, ct_ref, n_ref, m_ref, *,
                  reverse, heads, dqk, dv):
    @pl.when(pl.program_id(1) == 0)
    def _():
        ct_ref[...] = jnp.zeros_like(ct_ref)
        n_ref[...] = jnp.zeros_like(n_ref)
        m_ref[...] = jnp.zeros_like(m_ref)

    length = q_ref.shape[0]
    valid, valid_t = _tri_masks(length, reverse)
    gc = gc_ref[...] + bc_ref[...]
    gr = gr_ref[...] + br_ref[...]
    d0 = 2 * heads if reverse else 0
    scale = dqk ** -0.5
    for h in range(heads):
        i_col = gc[:, d0 + h:d0 + h + 1]
        f_col = _log_sigmoid(gc[:, d0 + heads + h:d0 + heads + h + 1])
        i_row = gr[d0 + h:d0 + h + 1, :]
        f_row = _log_sigmoid(gr[d0 + heads + h:d0 + heads + h + 1, :])
        b_col = jnp.sum(jnp.where(valid, f_row, 0.0), axis=1, keepdims=True)
        b_row = jnp.sum(jnp.where(valid_t, f_col, 0.0), axis=0, keepdims=True)
        b_end = jnp.sum(f_row, axis=1, keepdims=True)
        m_prev = m_ref[h][:, :1]
        log_d = jnp.where(valid, b_col - b_row + i_row, NEG)
        m_inter = b_col + m_prev
        m_t = jnp.maximum(m_inter, jnp.max(log_d, axis=1, keepdims=True))
        w_intra = jnp.exp(log_d - m_t) * scale
        w_inter = jnp.exp(m_inter - m_t) * scale
        q = q_ref[:, h * dqk:(h + 1) * dqk]
        k = k_ref[:, h * dqk:(h + 1) * dqk]
        v = v_ref[:, h * dv:(h + 1) * dv]
        s = _dot_nt(q, k) * w_intra
        num = _dot(s.astype(BF16), v) + w_inter * _dot(q, ct_ref[h].astype(BF16))
        qn = jnp.sum(q.astype(F32) * n_ref[h], axis=1, keepdims=True)
        den = jnp.sum(s, axis=1, keepdims=True) + w_inter * qn
        h_ref[:, h * dv:(h + 1) * dv] = num / jnp.maximum(jnp.abs(den), jnp.exp(-m_t))
        log_w = b_end - b_col + i_col
        m_new = jnp.maximum(b_end + m_prev, jnp.max(log_w, axis=0, keepdims=True))
        w_end = jnp.exp(log_w - m_new)
        decay = jnp.exp(b_end + m_prev - m_new)
        wv = (w_end * v.astype(F32)).astype(BF16)
        ct_ref[h] = decay * ct_ref[h] + _dot_tn(k, wv)
        n_ref[h] = decay * n_ref[h] + jnp.sum(w_end * k.astype(F32), axis=0, keepdims=True)
        m_ref[h] = jnp.broadcast_to(m_new, (1, LANES))


def _chunk_index(i, n_chunks, reverse):
    if not reverse:
        return i
    return jnp.where(i == 0, 0, n_chunks - i)


def _mlstm(p, col0, gates_c, gates_r, bias_c, bias_r, batch, seq_rows, reverse):
    t = p.shape[0]
    length = ROW_TILE
    n_chunks = seq_rows // length
    heads = A_HEADS
    dv = 256
    dqk = 128
    qw, vw = heads * dqk, heads * dv

    def rows(b, i):
        return b * n_chunks + _chunk_index(i, n_chunks, reverse)

    kern = functools.partial(_mlstm_kernel, reverse=reverse, heads=heads, dqk=dqk, dv=dv)
    return pl.pallas_call(
        kern,
        grid=(batch, n_chunks),
        in_specs=[
            pl.BlockSpec((length, qw), lambda b, i: (rows(b, i), col0 // qw)),
            pl.BlockSpec((length, qw), lambda b, i: (rows(b, i), col0 // qw + 1)),
            pl.BlockSpec((length, vw), lambda b, i: (rows(b, i), (col0 + 2 * qw) // vw)),
            pl.BlockSpec((length, LANES), lambda b, i: (rows(b, i), 0)),
            pl.BlockSpec((4 * heads, length), lambda b, i: (0, rows(b, i))),
            pl.BlockSpec((1, LANES), lambda b, i: (0, 0)),
            pl.BlockSpec((4 * heads, 1), lambda b, i: (0, 0)),
        ],
        out_specs=pl.BlockSpec((length, vw), lambda b, i: (rows(b, i), 0)),
        out_shape=jax.ShapeDtypeStruct((t, vw), F32),
        scratch_shapes=[pltpu.VMEM((heads, dqk, dv), F32), pltpu.VMEM((heads, 1, dqk), F32),
                        pltpu.VMEM((heads, 1, LANES), F32)],
        compiler_params=_params(("parallel", "arbitrary")), name="mlstm_scan",
    )(p, p, p, gates_c, gates_r, bias_c, bias_r)


def _rope(t, cosf, sinf, half):
    if 2 * half == LANES:
        partner = pltpu.roll(t, half, axis=1)
    else:
        lane = lax.broadcasted_iota(jnp.int32, t.shape, 1)
        partner = jnp.where((lane & (2 * half - 1)) < half, pltpu.roll(t, LANES - half, axis=1),
                            pltpu.roll(t, half, axis=1))
    return t * cosf + partner * sinf


def _retention_kernel(q_ref, k_ref, v_ref, cos_ref, sin_ref, dp_ref, o_ref, s_ref, *, reverse, heads, dk, dv):
    @pl.when(pl.program_id(1) == 0)
    def _():
        s_ref[...] = jnp.zeros_like(s_ref)

    length = q_ref.shape[0]
    row = lax.broadcasted_iota(jnp.int32, (length, length), 0)
    col = lax.broadcasted_iota(jnp.int32, (length, length), 1)
    diff = (col - row if reverse else row - col).astype(F32)
    tpos = lax.broadcasted_iota(jnp.int32, (length, 1), 0).astype(F32)
    lg_all = -_softplus(dp_ref[...])
    cosf = cos_ref[...]
    sinf = sin_ref[...]
    scale = dk ** -0.5
    d = 1 if reverse else 0
    for h in range(heads):
        lg = lg_all[d:d + 1, h:h + 1]
        decay_mat = jnp.where(diff >= 0.0, jnp.exp(lg * jnp.maximum(diff, 0.0)), 0.0) * scale
        if reverse:
            read_w = jnp.exp(lg * (length - tpos))
            write_w = jnp.exp(lg * tpos)
        else:
            read_w = jnp.exp(lg * (tpos + 1.0))
            write_w = jnp.exp(lg * (length - 1.0 - tpos))
        chunk_decay = jnp.exp(lg * float(length))
        q = _rope(q_ref[:, h * dk:(h + 1) * dk].astype(F32), cosf, sinf, dk // 2).astype(BF16)
        k = _rope(k_ref[:, h * dk:(h + 1) * dk].astype(F32), cosf, sinf, dk // 2).astype(BF16)
        v = v_ref[:, h * dv:(h + 1) * dv]
        s = _dot_nt(q, k) * decay_mat
        o_ref[:, h * dv:(h + 1) * dv] = _dot(s.astype(BF16), v) + read_w * _dot(q, s_ref[h].astype(BF16))
        wv = (write_w * v.astype(F32)).astype(BF16)
        s_ref[h] = chunk_decay * s_ref[h] + scale * _dot_tn(k, wv)


def _retention(p, col0, cosf, sinf, decay_p, batch, seq_rows, reverse):
    t = p.shape[0]
    length = ROW_TILE
    n_chunks = seq_rows // length
    heads = B_HEADS
    dk, dv = 128, 256
    qw, vw = heads * dk, heads * dv

    def rows(b, i):
        return b * n_chunks + _chunk_index(i, n_chunks, reverse)

    kern = functools.partial(_retention_kernel, reverse=reverse, heads=heads, dk=dk, dv=dv)
    return pl.pallas_call(
        kern,
        grid=(batch, n_chunks),
        in_specs=[
            pl.BlockSpec((length, qw), lambda b, i: (rows(b, i), col0 // qw)),
            pl.BlockSpec((length, qw), lambda b, i: (rows(b, i), col0 // qw + 1)),
            pl.BlockSpec((length, vw), lambda b, i: (rows(b, i), (col0 + 2 * qw) // vw)),
            pl.BlockSpec((length, LANES), lambda b, i: (_chunk_index(i, n_chunks, reverse), 0)),
            pl.BlockSpec((length, LANES), lambda b, i: (_chunk_index(i, n_chunks, reverse), 0)),
            pl.BlockSpec(decay_p.shape, lambda b, i: (0, 0)),
        ],
        out_specs=pl.BlockSpec((length, vw), lambda b, i: (rows(b, i), 0)),
        out_shape=jax.ShapeDtypeStruct((t, vw), F32),
        scratch_shapes=[pltpu.VMEM((heads, dk, dv), F32)],
        compiler_params=_params(("parallel", "arbitrary")), name="retention_scan",
    )(p, p, p, cosf, sinf, decay_p)


def _headnorm_kernel(hf_ref, hb_ref, gate_ref, gain_ref, y_ref, *, heads, dv, center, silu_gate):
    g = gate_ref[...].astype(F32)
    sg = _sigmoid(g)
    gate = g * sg if silu_gate else sg
    gain = gain_ref[...]
    for h in range(heads):
        sl = slice(h * dv, (h + 1) * dv)
        x = hf_ref[:, sl] + hb_ref[:, sl]
        if center:
            x = x - jnp.mean(x, axis=-1, keepdims=True)
        y = x * lax.rsqrt(jnp.mean(x * x, axis=-1, keepdims=True) + EPS)
        y_ref[:, sl] = (gate[:, sl] * (y * gain[:, sl])).astype(y_ref.dtype)


def _headnorm(hf, hb, p, gate_col, gain, heads, center, silu_gate):
    t, w = hf.shape
    tm = ROW_TILE
    kern = functools.partial(_headnorm_kernel, heads=heads, dv=w // heads, center=center, silu_gate=silu_gate)
    row_spec = pl.BlockSpec((tm, w), lambda i: (i, 0))
    return pl.pallas_call(
        kern, grid=(t // tm,),
        in_specs=[row_spec, row_spec, pl.BlockSpec((tm, w), lambda i: (i, gate_col // w)),
                  pl.BlockSpec((1, w), lambda i: (0, 0))],
        out_specs=row_spec,
        out_shape=jax.ShapeDtypeStruct((t, w), BF16),
        compiler_params=_params(("parallel",)), name="headnorm_gate",
    )(hf, hb, p, gain.reshape(1, w))


def _lane_low(shape, hd):
    return lax.broadcasted_iota(jnp.int32, shape, 1) < hd


def _stack_heads(q, hd):
    low = _lane_low(q.shape, hd)
    zero = jnp.zeros_like(q)
    return jnp.concatenate([jnp.where(low, q, zero), jnp.where(low, zero, q)], axis=0)


def _unstack_heads(o, hd):
    rows = o.shape[0] // 2
    return jnp.where(_lane_low((rows, LANES), hd), o[:rows], o[rows:])


def _window_kernel(sink_ref, q_ref, kp_ref, k0_ref, kn_ref, vp_ref, v0_ref, vn_ref, kc_ref, vc_ref,
                   cq_ref, sq_ref, cp_ref, sp_ref, cn_ref, sn_ref, o_ref, *, heads, kv_heads, hd, ctx_blocks,
                   n_latent, slabs_per_dot):
    i = pl.program_id(1)
    j = i - ctx_blocks
    w = q_ref.shape[0]
    half = hd // 2
    slabs_per_group = heads // kv_heads // 2

    def roped(x_ref, c_ref, s_ref):
        return _rope(x_ref[...].astype(F32), c_ref[...], s_ref[...], half)

    kb = jnp.concatenate([roped(kp_ref, cp_ref, sp_ref), roped(k0_ref, cq_ref, sq_ref),
                          roped(kn_ref, cn_ref, sn_ref)], axis=0)
    vb = jnp.concatenate([vp_ref[...], v0_ref[...], vn_ref[...]], axis=0).astype(F32)
    kc = kc_ref[...].astype(F32)
    vc = vc_ref[...].astype(F32)

    def group_copy(x, g):
        keep = _lane_low(x.shape, hd) if g == 0 else jnp.logical_not(_lane_low(x.shape, hd))
        return jnp.where(keep, x, pltpu.roll(x, hd, axis=1)).astype(BF16)

    t = lax.broadcasted_iota(jnp.int32, (w, 3 * w), 0)
    c = lax.broadcasted_iota(jnp.int32, (w, 3 * w), 1)
    lower = jnp.maximum(t + (w - WINDOW), (1 - j) * w)
    upper = jnp.minimum(t + (w + WINDOW), n_latent - 1 - (j - 1) * w)
    upper = jnp.where(j < 0, -1, upper)
    mask_bias = jnp.where((c >= lower) & (c <= upper), 0.0, NEG)
    mask_bias = jnp.concatenate([mask_bias] * (2 * slabs_per_dot), axis=0)
    scale = hd ** -0.5
    cq = cq_ref[...]
    sq = sq_ref[...]
    for g in range(kv_heads):
        kk, vv, kkc, vvc = group_copy(kb, g), group_copy(vb, g), group_copy(kc, g), group_copy(vc, g)
        for s0 in range(0, slabs_per_group, slabs_per_dot):
            slabs = [g * slabs_per_group + s0 + s for s in range(slabs_per_dot)]
            qs = []
            sinks = []
            for slab in slabs:
                qr = _rope(q_ref[:, slab * LANES:(slab + 1) * LANES].astype(F32), cq, sq, half)
                qs.append(_stack_heads((qr * scale).astype(BF16), hd))
                sinks += [jnp.full((w, 1), sink_ref[2 * slab], F32), jnp.full((w, 1), sink_ref[2 * slab + 1], F32)]
            qq = jnp.concatenate(qs, axis=0)
            sink = jnp.concatenate(sinks, axis=0)
            s_loc = _dot_nt(qq, kk) + mask_bias
            s_ctx = _dot_nt(qq, kkc)
            m = jnp.maximum(jnp.maximum(jnp.max(s_loc, axis=-1, keepdims=True),
                                        jnp.max(s_ctx, axis=-1, keepdims=True)), sink)
            e_loc = jnp.exp(s_loc - m)
            e_ctx = jnp.exp(s_ctx - m)
            den = (jnp.exp(sink - m) + jnp.sum(e_loc, axis=-1, keepdims=True)
                   + jnp.sum(e_ctx, axis=-1, keepdims=True))
            o = (_dot(e_loc.astype(BF16), vv) + _dot(e_ctx.astype(BF16), vvc)) / den
            for n, slab in enumerate(slabs):
                o_ref[:, slab * LANES:(slab + 1) * LANES] = _unstack_heads(
                    o[2 * n * w:2 * (n + 1) * w], hd).astype(o_ref.dtype)


def _window_attention(p, col0, cosf, sinf, sink, batch, seq_rows, ctx_rows):
    t = p.shape[0]
    w = WINDOW
    heads, kv_heads = C_HEADS, C_KV_HEADS
    hd = LANES // 2
    assert kv_heads * hd == LANES
    qw = heads * hd
    nblk = seq_rows // w
    ctx_blocks = ctx_rows // w
    kcol = (col0 + qw) // LANES
    vcol = kcol + 1

    def cur(b, i):
        return b * nblk + i

    def prev(b, i):
        return b * nblk + jnp.maximum(i - 1, 0)

    def nxt(b, i):
        return b * nblk + jnp.minimum(i + 1, nblk - 1)

    def kv_spec(fn, colblk):
        return pl.BlockSpec((w, LANES), lambda b, i: (fn(b, i), colblk))

    def tab_spec(fn):
        return pl.BlockSpec((w, LANES), lambda b, i: (fn(0, i), 0))

    kern = functools.partial(_window_kernel, heads=heads, kv_heads=kv_heads, hd=hd, ctx_blocks=ctx_blocks,
                             n_latent=seq_rows - ctx_rows, slabs_per_dot=2)
    ctx_spec_k = pl.BlockSpec((ctx_rows, LANES), lambda b, i: (b * (seq_rows // ctx_rows), kcol))
    ctx_spec_v = pl.BlockSpec((ctx_rows, LANES), lambda b, i: (b * (seq_rows // ctx_rows), vcol))
    return pl.pallas_call(
        kern,
        grid=(batch, nblk),
        in_specs=[
            pl.BlockSpec(memory_space=pltpu.SMEM),
            pl.BlockSpec((w, qw), lambda b, i: (cur(b, i), col0 // qw)),
            kv_spec(prev, kcol), kv_spec(cur, kcol), kv_spec(nxt, kcol),
            kv_spec(prev, vcol), kv_spec(cur, vcol), kv_spec(nxt, vcol),
            ctx_spec_k, ctx_spec_v,
            tab_spec(cur), tab_spec(cur), tab_spec(prev), tab_spec(prev), tab_spec(nxt), tab_spec(nxt),
        ],
        out_specs=pl.BlockSpec((w, qw), lambda b, i: (cur(b, i), 0)),
        out_shape=jax.ShapeDtypeStruct((t, qw), BF16),
        compiler_params=_params(("parallel", "arbitrary")), name="window_attention",
    )(sink, p, p, p, p, p, p, p, p, p, cosf, sinf, cosf, sinf, cosf, sinf)


def _nbr_kernel(q_ref, k_ref, v_ref, bias_ref, o_ref, *, hd, ctx_rows, grid_rows):
    i = pl.program_id(2)
    scale = hd ** -0.5
    qq = _stack_heads(q_ref[...] * scale, hd)
    kc = k_ref[0:ctx_rows, :]
    vc = v_ref[0:ctx_rows, :]
    s_ctx = _dot_nt(qq, kc)
    m_ctx = jnp.max(s_ctx, axis=-1, keepdims=True)

    @pl.when(i == 0)
    def _():
        e = jnp.exp(s_ctx - m_ctx)
        o = _dot(e.astype(BF16), vc) / jnp.sum(e, axis=-1, keepdims=True)
        o_ref[...] = _unstack_heads(o, hd).astype(o_ref.dtype)

    @pl.when(i > 0)
    def _():
        blk = i - 1
        n_blk = grid_rows // NBR_ROWS
        u0 = jnp.clip(blk * NBR_ROWS - NA_ROWS // 2, 0, grid_rows - NBR_UNION)
        variant = jnp.where(blk == 0, 0, jnp.where(blk == n_blk - 1, 2, 1))
        base = pl.multiple_of(ctx_rows + u0 * GRID_W, GRID_W)
        kw = k_ref[pl.ds(base, NBR_UNION * GRID_W), :]
        vw = v_ref[pl.ds(base, NBR_UNION * GRID_W), :]
        bias = jnp.concatenate([bias_ref[0, variant], bias_ref[1, variant]], axis=0)
        s_loc = _dot_nt(qq, kw) + bias
        m = jnp.maximum(jnp.max(s_loc, axis=-1, keepdims=True), m_ctx)
        e_loc = jnp.exp(s_loc - m)
        e_ctx = jnp.exp(s_ctx - m)
        den = jnp.sum(e_loc, axis=-1, keepdims=True) + jnp.sum(e_ctx, axis=-1, keepdims=True)
        o = (_dot(e_loc.astype(BF16), vw) + _dot(e_ctx.astype(BF16), vc)) / den
        o_ref[...] = _unstack_heads(o, hd).astype(o_ref.dtype)


def _nbr_bias(na_bias_l):
    col = jnp.arange(GRID_W)
    col_start = jnp.clip(col - NA_COLS // 2, 0, GRID_W - NA_COLS)
    col_valid = (col[None, :] >= col_start[:, None]) & (col[None, :] < col_start[:, None] + NA_COLS)
    dc = jnp.clip(col[None, :] - col[:, None], -(NA_COLS - 1), NA_COLS - 1) + (NA_COLS - 1)
    rr = np.arange(NBR_ROWS)
    kr = np.arange(NBR_UNION)
    layouts = [(0, np.zeros(NBR_ROWS, int)), (NA_ROWS // 2, rr), (NBR_UNION - NBR_ROWS,
                                                                 np.full(NBR_ROWS, NBR_UNION - NA_ROWS))]
    mats = []
    for delta, rel_start in layouts:
        dr = kr[None, :] - (delta + rr[:, None]) + (NA_ROWS - 1)
        row_valid = (kr[None, :] >= rel_start[:, None]) & (kr[None, :] < rel_start[:, None] + NA_ROWS)
        tab = na_bias_l[:, np.clip(dr, 0, 2 * NA_ROWS - 2)]
        bias = jnp.take(tab, dc, axis=-1)
        ok = jnp.asarray(row_valid)[None, :, :, None, None] & col_valid[None, None, None]
        bias = jnp.where(ok, bias, NEG).transpose(0, 1, 3, 2, 4)
        mats.append(bias.reshape(na_bias_l.shape[0], NBR_ROWS * GRID_W, NBR_UNION * GRID_W))
    return jnp.stack(mats, axis=1).astype(F32)


def _nbr_attention(p, col0, bias_mat, batch, seq_rows, ctx_rows):
    t = p.shape[0]
    heads = D_HEADS
    hd = LANES // 2
    qw = heads * hd
    slabs = heads // 2
    tq = NBR_ROWS * GRID_W
    n_latent = seq_rows - ctx_rows
    grid_rows = n_latent // GRID_W
    assert tq == ctx_rows and grid_rows % NBR_ROWS == 0 and grid_rows >= NBR_UNION + 1
    steps = seq_rows // tq
    q0 = col0 // LANES
    kern = functools.partial(_nbr_kernel, hd=hd, ctx_rows=ctx_rows, grid_rows=grid_rows)
    return pl.pallas_call(
        kern,
        grid=(batch, slabs, steps),
        in_specs=[
            pl.BlockSpec((tq, LANES), lambda b, s, i: (b * steps + i, q0 + s)),
            pl.BlockSpec((seq_rows, LANES), lambda b, s, i: (b, q0 + slabs + s)),
            pl.BlockSpec((seq_rows, LANES), lambda b, s, i: (b, q0 + 2 * slabs + s)),
            pl.BlockSpec((2,) + bias_mat.shape[1:], lambda b, s, i: (s, 0, 0, 0)),
        ],
        out_specs=pl.BlockSpec((tq, LANES), lambda b, s, i: (b * steps + i, s)),
        out_shape=jax.ShapeDtypeStruct((t, qw), BF16),
        compiler_params=_params(("parallel", "parallel", "arbitrary")), name="nbr_attention",
    )(p, p, p, bias_mat)


def _merge_kernel(*refs):
    ys = refs[0:N_BRANCH]
    gs = refs[N_BRANCH:2 * N_BRANCH]
    ws = refs[2 * N_BRANCH:3 * N_BRANCH]
    o_ref = refs[3 * N_BRANCH]
    acc = None
    for y_ref, g_ref, w_ref in zip(ys, gs, ws):
        term = _sigmoid(g_ref[...].astype(F32)) * _dot(y_ref[...], w_ref[...])
        acc = term if acc is None else acc + term
    o_ref[...] = acc.astype(o_ref.dtype)


def _merge(ys, p, gate_col0, w_branch, row0, tm, tn):
    t, bw = ys[0].shape
    d = w_branch.shape[1]
    nt = d // tn
    y_spec = pl.BlockSpec((tm, bw), lambda i, j: (i, 0))
    g_specs = [pl.BlockSpec((tm, tn), functools.partial(lambda i, j, br: (i, (gate_col0 + br * d) // tn + j), br=br))
               for br in range(N_BRANCH)]
    w_specs = [pl.BlockSpec((bw, tn), functools.partial(lambda i, j, br: (row0 // bw + br, j), br=br))
               for br in range(N_BRANCH)]
    return pl.pallas_call(
        _merge_kernel,
        grid=(t // tm, nt),
        in_specs=[y_spec] * N_BRANCH + g_specs + w_specs,
        out_specs=pl.BlockSpec((tm, tn), lambda i, j: (i, j)),
        out_shape=jax.ShapeDtypeStruct((t, d), BF16),
        compiler_params=_params(("parallel", "arbitrary")), name="merge_branches",
    )(*ys, *([p] * N_BRANCH), *([w_branch] * N_BRANCH))


def _moe_kernel(t_ref, gates_ref, w1_ref, b1_ref, w2_ref, b2_ref, o_ref, acc_ref, *, ff, group):
    e = pl.program_id(1)
    gates = gates_ref[...]

    @pl.when(e == 0)
    def _():
        acc_ref[...] = _dot(gates.astype(BF16), b2_ref[...])

    hid = _dot(t_ref[...], w1_ref[0]) + b1_ref[0]
    gw = group * ff
    g_h = jnp.minimum(hid[:, :gw], SWIGLU_LIMIT)
    u_h = jnp.clip(hid[:, gw:], -SWIGLU_LIMIT, SWIGLU_LIMIT)
    lane = lax.broadcasted_iota(jnp.int32, gates.shape, 1)
    cols = []
    for s in range(group):
        gsel = jnp.sum(jnp.where(lane == e * group + s, gates, 0.0), axis=-1, keepdims=True)
        cols.append(jnp.broadcast_to(gsel, (gates.shape[0], ff)))
    gate_cols = jnp.concatenate(cols, axis=-1)
    act = g_h * _sigmoid(SWIGLU_ALPHA * g_h) * (u_h + 1.0) * gate_cols
    acc_ref[...] += _dot(act.astype(BF16), w2_ref[...])

    @pl.when(e == pl.num_programs(1) - 1)
    def _():
        o_ref[...] = acc_ref[...].astype(o_ref.dtype)


def _moe(h, gates, w1p, b1p, w2, b2p, group0, tm):
    t, d = h.shape
    width = w1p.shape[2]
    group = 2
    ff = width // (2 * group)
    groups = b1p.shape[0]
    kern = functools.partial(_moe_kernel, ff=ff, group=group)
    return pl.pallas_call(
        kern,
        grid=(t // tm, groups),
        in_specs=[
            pl.BlockSpec((tm, d), lambda i, e: (i, 0)),
            pl.BlockSpec((tm, LANES), lambda i, e: (i, 0)),
            pl.BlockSpec((1, d, width), lambda i, e: (group0 + e, 0, 0)),
            pl.BlockSpec((1, 1, width), lambda i, e: (e, 0, 0)),
            pl.BlockSpec((group * ff, d), lambda i, e: (group0 + e, 0)),
            pl.BlockSpec(b2p.shape, lambda i, e: (0, 0)),
        ],
        out_specs=pl.BlockSpec((tm, d), lambda i, e: (i, 0)),
        out_shape=jax.ShapeDtypeStruct((t, d), BF16),
        scratch_shapes=[pltpu.VMEM((tm, d), F32)],
        compiler_params=_params(("parallel", "arbitrary")), name="moe_experts",
    )(h, gates, w1p, b1p, w2, b2p)


def _rope_tables(n_latent, ctx_rows, hd):
    pos = np.arange(n_latent)
    row = (pos // GRID_W).astype(np.float32)
    col = (pos % GRID_W).astype(np.float32)
    n_freq = hd // 4
    inv = jnp.asarray(ROPE_BASE, F32) ** (-jnp.arange(n_freq, dtype=F32) / n_freq)
    ang = jnp.concatenate([jnp.asarray(row)[:, None] * inv, jnp.asarray(col)[:, None] * inv], axis=-1)
    ang = jnp.concatenate([jnp.zeros((ctx_rows, hd // 2), F32), ang], axis=0)
    cos, sin = jnp.cos(ang), jnp.sin(ang)
    reps = LANES // hd
    cosf = jnp.tile(jnp.concatenate([cos, cos], axis=-1), (1, reps))
    sinf = jnp.tile(jnp.concatenate([-sin, sin], axis=-1), (1, reps))
    return cosf, sinf


def _pad_lanes(a, value=0.0):
    pad = LANES - a.shape[-1]
    return jnp.pad(a, [(0, 0)] * (a.ndim - 1) + [(0, pad)], constant_values=value)


def kernel(x, c, ctx, c_ctx, ada_down, ada_up, ada_bias, norm_gain, w_in, mlstm_gate_bias, mlstm_norm_gain,
           ret_decay, ret_norm_gain, sink, na_bias, w_branch, w_out, router_w, router_b, moe_w1, moe_b1,
           moe_w2, moe_b2, final_gain):
    batch, n_latent, d = x.shape
    ctx_rows = ctx.shape[1]
    depth = w_in.shape[0]
    seq_rows = ctx_rows + n_latent
    t = batch * seq_rows
    bw = d // N_BRANCH
    n_experts = router_w.shape[2]
    ff = moe_w2.shape[2]

    a_sz = (A_HEADS * 128, A_HEADS * 128, bw, bw)
    n_gate = 4 * A_HEADS
    b_sz = (B_HEADS * 128, B_HEADS * 128, bw, bw)
    c_sz = (bw, C_KV_HEADS * 64, C_KV_HEADS * 64)
    d_sz = (bw, bw, bw)
    sizes = a_sz + (n_gate,) + b_sz + c_sz + d_sz + (N_BRANCH * d,)
    offs = [int(o) for o in np.concatenate([[0], np.cumsum(sizes)])]
    a0, g0, b0, c0, d0, bg0, end = offs[0], offs[4], offs[5], offs[9], offs[12], offs[15], offs[16]
    col_gate = 0
    col_a = N_BRANCH * d
    col_b = col_a + sum(a_sz)
    col_d = col_b + sum(b_sz)
    col_c = col_d + sum(d_sz)

    w_cat, w_g, w_gt = _win_prep(w_in, [(bg0, end), (a0, g0), (b0, c0), (d0, bg0), (c0, d0)], (g0, b0))
    w1p = _moe_w1_prep(moe_w1)
    wb_bf, wo_bf, w2_bf = _cast_bf16(w_branch.reshape(depth * N_BRANCH * bw, d), w_out.reshape(depth * d, d),
                                     moe_w2.reshape(depth * n_experts * ff, d))
    b1p_all = jnp.concatenate([moe_b1[..., 0::2].reshape(depth, n_experts // 2, 1, 2 * ff),
                               moe_b1[..., 1::2].reshape(depth, n_experts // 2, 1, 2 * ff)], axis=-1)
    b2p_all = jnp.pad(moe_b2, ((0, 0), (0, LANES - n_experts), (0, 0))).astype(BF16)

    mod_all = _ada_modulation(jnp.pad(jnp.concatenate([c_ctx[None], c], axis=0), ((0, 8 - 1 - batch), (0, 0))),
                              ada_down, ada_up, ada_bias)
    cos_b, sin_b = _rope_tables(n_latent, ctx_rows, 128)
    cos_c, sin_c = _rope_tables(n_latent, ctx_rows, 64)
    decay_p = jnp.pad(ret_decay, ((0, 0), (0, 8 - ret_decay.shape[1]), (0, LANES - ret_decay.shape[2])))

    tm_proj = _row_tile(t, 768)
    tm_moe = _row_tile(t, 512)
    z = (ctx.reshape(batch * ctx_rows, d), x.reshape(batch * n_latent, d))
    delta = None
    for l in range(depth):
        mod = mod_all[l]
        if l == 0:
            h = _resid_norm(z, norm_gain[l, 0], mod, 0, 1, batch, seq_rows, ctx_rows)[0]
        else:
            z, h = _resid_norm(z, norm_gain[l, 0], mod, 0, 1, batch, seq_rows, ctx_rows, delta=delta,
                               gate_mod=mod_all[l - 1], gate_slot=5)
        p = _matmul(h, w_cat, l, BF16, tm=tm_proj, tn=768)
        gates_c, gates_r = _gate_proj(h, w_g, w_gt, l, tm_proj)
        gates_r = gates_r[:n_gate]
        bias_c = _pad_lanes(mlstm_gate_bias[l][None])
        bias_r = mlstm_gate_bias[l][:, None]

        hf = _mlstm(p, col_a, gates_c, gates_r, bias_c, bias_r, batch, seq_rows, False)
        hb = _mlstm(p, col_a, gates_c, gates_r, bias_c, bias_r, batch, seq_rows, True)
        y_a = _headnorm(hf, hb, p, col_a + 2 * A_HEADS * 128 + bw, mlstm_norm_gain[l], A_HEADS, False, False)
        of = _retention(p, col_b, cos_b, sin_b, decay_p[l], batch, seq_rows, False)
        ob = _retention(p, col_b, cos_b, sin_b, decay_p[l], batch, seq_rows, True)
        y_b = _headnorm(of, ob, p, col_b + 2 * B_HEADS * 128 + bw, ret_norm_gain[l], B_HEADS, True, True)
        y_c = _window_attention(p, col_c, cos_c, sin_c, sink[l], batch, seq_rows, ctx_rows)
        y_d = _nbr_attention(p, col_d, _nbr_bias(na_bias[l]), batch, seq_rows, ctx_rows)

        acc = _merge([y_a, y_b, y_c, y_d], p, col_gate, wb_bf, l * N_BRANCH * bw, tm=tm_moe, tn=1024)
        delta1 = _matmul(acc, wo_bf.reshape(depth, d, d), l, BF16, tm=tm_proj, tn=1024)

        rw = _pad_lanes(router_w[l]).astype(BF16)
        rb = _pad_lanes(router_b[l][None], NEG)
        z, h2, gates = _resid_norm(z, norm_gain[l, 1], mod, 3, 4, batch, seq_rows, ctx_rows, delta=delta1,
                                   gate_mod=mod, gate_slot=2, router=(rw, rb))
        delta = _moe(h2, gates, w1p, b1p_all[l], w2_bf, b2p_all[l], l * (n_experts // 2), tm=tm_moe)

    out = _final_norm(z, delta, mod_all[depth - 1], 5, final_gain, batch, seq_rows, ctx_rows)
    return out.reshape(batch, n_latent, d)
```

```python
import functools

import jax
import jax.numpy as jnp
import numpy as np
from jax import lax
from jax.experimental import pallas as pl
from jax.experimental.pallas import tpu as pltpu

F32 = jnp.float32
BF16 = jnp.bfloat16

GRID_W = 64
EPS = 1e-6
NEG = -1e30
ROPE_BASE = 10000.0
N_BRANCH = 4
A_HEADS = 4
B_HEADS = 4
C_HEADS = 16
C_KV_HEADS = 2
D_HEADS = 16
WINDOW = 128
NA_ROWS = 8
NA_COLS = 16
TOP_K = 4
SWIGLU_LIMIT = 7.0
SWIGLU_ALPHA = 1.702

LANES = 128
VMEM_LIMIT = 56 * 1024 * 1024
ROW_TILE = 256
NBR_ROWS = 4
NBR_UNION = NBR_ROWS + NA_ROWS - 1


def _row_tile(rows, target):
    best = ROW_TILE
    for tile in range(ROW_TILE, target + 1, ROW_TILE):
        if rows % tile == 0:
            best = tile
    assert rows % best == 0
    return best


def _params(sem, vmem=VMEM_LIMIT):
    return pltpu.CompilerParams(dimension_semantics=sem, vmem_limit_bytes=vmem)


def _dot(a, b):
    return jnp.dot(a, b, preferred_element_type=F32)


def _dot_nt(a, b):
    return lax.dot_general(a, b, (((1,), (1,)), ((), ())), preferred_element_type=F32)


def _dot_tn(a, b):
    return lax.dot_general(a, b, (((0,), (0,)), ((), ())), preferred_element_type=F32)


def _sigmoid(x):
    return 1.0 / (1.0 + jnp.exp(-x))


def _softplus(x):
    return jnp.maximum(x, 0.0) + jnp.log1p(jnp.exp(-jnp.abs(x)))


def _log_sigmoid(x):
    return -_softplus(-x)


def _win_prep_kernel(w_ref, cat_ref, g_ref, gt_ref, *, segments, gate_seg):
    col = 0
    for a, b in segments:
        cat_ref[0, :, col:col + (b - a)] = w_ref[0, :, a:b].astype(BF16)
        col += b - a
    ga, gb = gate_seg
    g = w_ref[0, :, ga:gb]
    gpad = jnp.concatenate([g, jnp.zeros((g.shape[0], LANES - (gb - ga)), F32)], axis=1)
    g_ref[0] = gpad.astype(BF16)
    gt_ref[0] = gpad.T.astype(BF16)


def _win_prep(w_in, segments, gate_seg):
    depth, d, width = w_in.shape
    rows = LANES
    n_cat = sum(b - a for a, b in segments)
    kern = functools.partial(_win_prep_kernel, segments=segments, gate_seg=gate_seg)
    return pl.pallas_call(
        kern, grid=(depth, d // rows),
        in_specs=[pl.BlockSpec((1, rows, width), lambda l, i: (l, i, 0))],
        out_specs=[pl.BlockSpec((1, rows, n_cat), lambda l, i: (l, i, 0)),
                   pl.BlockSpec((1, rows, LANES), lambda l, i: (l, i, 0)),
                   pl.BlockSpec((1, LANES, rows), lambda l, i: (l, 0, i))],
        out_shape=[jax.ShapeDtypeStruct((depth, d, n_cat), BF16),
                   jax.ShapeDtypeStruct((depth, d, LANES), BF16),
                   jax.ShapeDtypeStruct((depth, LANES, d), BF16)],
        compiler_params=_params(("parallel", "parallel")), name="win_prep",
    )(w_in)


def _moe_w1_prep_kernel(w_ref, perm_ref, o_ref, *, ff):
    parts = [_dot(w_ref[0, s].astype(BF16), perm_ref[...]) for s in range(2)]
    o_ref[0] = jnp.concatenate([parts[0][:, :ff], parts[1][:, :ff], parts[0][:, ff:], parts[1][:, ff:]],
                               axis=1).astype(BF16)


def _moe_w1_prep(moe_w1):
    depth, n_exp, d, two_ff = moe_w1.shape
    ff = two_ff // 2
    rows = 1024
    src = np.concatenate([np.arange(0, two_ff, 2), np.arange(1, two_ff, 2)])
    perm = jnp.asarray(np.arange(two_ff)[:, None] == src[None, :], BF16)
    kern = functools.partial(_moe_w1_prep_kernel, ff=ff)
    return pl.pallas_call(
        kern, grid=(depth * n_exp // 2, d // rows),
        in_specs=[pl.BlockSpec((1, 2, rows, two_ff), lambda e, i: (e, 0, i, 0)),
                  pl.BlockSpec((two_ff, two_ff), lambda e, i: (0, 0))],
        out_specs=pl.BlockSpec((1, rows, 2 * two_ff), lambda e, i: (e, i, 0)),
        out_shape=jax.ShapeDtypeStruct((depth * n_exp // 2, d, 2 * two_ff), BF16),
        compiler_params=_params(("parallel", "parallel")), name="moe_w1_prep",
    )(moe_w1.reshape(depth * n_exp // 2, 2, d, two_ff), perm)


def _cast_kernel(*refs):
    n = len(refs) // 2
    for src, dst in zip(refs[:n], refs[n:]):
        dst[...] = src[...].astype(dst.dtype)


def _cast_bf16(*arrays):
    rows, cols = arrays[0].shape
    tm = ROW_TILE
    spec = pl.BlockSpec((tm, cols), lambda i: (i, 0))
    return pl.pallas_call(
        _cast_kernel, grid=(rows // tm,),
        in_specs=[spec] * len(arrays), out_specs=[spec] * len(arrays),
        out_shape=[jax.ShapeDtypeStruct((rows, cols), BF16)] * len(arrays),
        compiler_params=_params(("parallel",)), name="cast_bf16",
    )(*arrays)


def _ada_kernel(cv_ref, down_ref, up_ref, bias_ref, out_ref):
    cv = cv_ref[...]
    a = cv * _sigmoid(cv)
    z = _dot(a.astype(BF16), down_ref[0].astype(BF16))
    out_ref[0] = _dot(z.astype(BF16), up_ref[0].astype(BF16)) + bias_ref[0]


def _ada_modulation(cvecs, ada_down, ada_up, ada_bias):
    depth, d, r = ada_down.shape
    rows = cvecs.shape[0]
    return pl.pallas_call(
        _ada_kernel,
        grid=(depth, 6),
        in_specs=[
            pl.BlockSpec((rows, d), lambda l, j: (0, 0)),
            pl.BlockSpec((1, d, r), lambda l, j: (l, 0, 0)),
            pl.BlockSpec((1, r, d), lambda l, j: (l, 0, j)),
            pl.BlockSpec((1, 1, d), lambda l, j: (l, 0, j)),
        ],
        out_specs=pl.BlockSpec((1, rows, d), lambda l, j: (l, 0, j)),
        out_shape=jax.ShapeDtypeStruct((depth, rows, 6 * d), F32),
        compiler_params=_params(("parallel", "arbitrary")),
        name="ada_modulation",
    )(cvecs, ada_down, ada_up, ada_bias.reshape(depth, 1, 6 * d))


def _rmsnorm(x, gain):
    return x * lax.rsqrt(jnp.mean(x * x, axis=-1, keepdims=True) + EPS) * gain


def _resid_norm_kernel(*refs, split_src, has_delta, has_router, tiles_per_seq, ctx_tiles):
    refs = list(refs)
    tile = pl.program_id(0)
    b = tile // tiles_per_seq
    j = tile - b * tiles_per_seq
    is_ctx = j < ctx_tiles
    row = jnp.where(is_ctx, 0, 1 + b)
    if split_src:
        c_ref = refs.pop(0)
        x_ref = refs.pop(0)
        z = jnp.where(is_ctx, c_ref[...], x_ref[...])
    else:
        z = refs.pop(0)[...]
    if has_delta:
        delta_ref = refs.pop(0)
        gate_ref = refs.pop(0)
    gain_ref = refs.pop(0)
    shift_ref = refs.pop(0)
    scale_ref = refs.pop(0)
    if has_router:
        rw_ref = refs.pop(0)
        rb_ref = refs.pop(0)
    if has_delta:
        znew_ref = refs.pop(0)
    h_ref = refs.pop(0)
    if has_router:
        gates_ref = refs.pop(0)

    if has_delta:
        z = z + gate_ref[pl.ds(row, 1), :] * delta_ref[...].astype(F32)
        znew_ref[...] = z
    y = _rmsnorm(z, gain_ref[...])
    h = y * (1.0 + scale_ref[pl.ds(row, 1), :]) + shift_ref[pl.ds(row, 1), :]
    hb = h.astype(BF16)
    h_ref[...] = hb
    if has_router:
        logits = _dot(hb, rw_ref[...]) + rb_ref[...]
        lane = lax.broadcasted_iota(jnp.int32, logits.shape, 1).astype(F32)
        gates = jnp.zeros_like(logits)
        den = jnp.zeros((logits.shape[0], 1), F32)
        top = None
        for _ in range(TOP_K):
            m = jnp.max(logits, axis=-1, keepdims=True)
            idx = jnp.min(jnp.where(logits == m, lane, float(LANES)), axis=-1, keepdims=True)
            sel = lane == idx
            if top is None:
                top = m
            e = jnp.exp(m - top)
            den = den + e
            gates = jnp.where(sel, e, gates)
            logits = jnp.where(sel, NEG * 2.0, logits)
        gates_ref[...] = gates / den


def _resid_norm(src, gain, mod, shift_slot, scale_slot, batch, seq_rows, ctx_rows, delta=None, gate_mod=None,
                gate_slot=None, router=None):
    split_src = isinstance(src, tuple)
    d = src[0].shape[1] if split_src else src.shape[1]
    t = batch * seq_rows
    tm = ROW_TILE
    tiles_per_seq = seq_rows // tm
    ctx_tiles = ctx_rows // tm
    lat_tiles = tiles_per_seq - ctx_tiles
    has_delta = delta is not None
    has_router = router is not None
    row_spec = pl.BlockSpec((tm, d), lambda i: (i, 0))
    vec_spec = pl.BlockSpec((1, d), lambda i: (0, 0))

    def mod_spec(slot):
        return pl.BlockSpec((mod.shape[0], d), lambda i: (0, slot))

    def ctx_map(i):
        b = i // tiles_per_seq
        return b * ctx_tiles + jnp.minimum(i - b * tiles_per_seq, ctx_tiles - 1), 0

    def lat_map(i):
        b = i // tiles_per_seq
        return b * lat_tiles + jnp.maximum(i - b * tiles_per_seq - ctx_tiles, 0), 0

    if split_src:
        operands, in_specs = list(src), [pl.BlockSpec((tm, d), ctx_map), pl.BlockSpec((tm, d), lat_map)]
    else:
        operands, in_specs = [src], [row_spec]
    if has_delta:
        operands += [delta, gate_mod]
        in_specs += [row_spec, mod_spec(gate_slot)]
    operands += [gain.reshape(1, d), mod, mod]
    in_specs += [vec_spec, mod_spec(shift_slot), mod_spec(scale_slot)]
    if has_router:
        rw, rb = router
        operands += [rw, rb]
        in_specs += [pl.BlockSpec(rw.shape, lambda i: (0, 0)), pl.BlockSpec(rb.shape, lambda i: (0, 0))]
    out_shape, out_specs = [], []
    if has_delta:
        out_shape.append(jax.ShapeDtypeStruct((t, d), F32))
        out_specs.append(row_spec)
    out_shape.append(jax.ShapeDtypeStruct((t, d), BF16))
    out_specs.append(row_spec)
    if has_router:
        out_shape.append(jax.ShapeDtypeStruct((t, LANES), F32))
        out_specs.append(pl.BlockSpec((tm, LANES), lambda i: (i, 0)))
    kern = functools.partial(_resid_norm_kernel, split_src=split_src, has_delta=has_delta, has_router=has_router,
                             tiles_per_seq=tiles_per_seq, ctx_tiles=ctx_tiles)
    return pl.pallas_call(
        kern, grid=(t // tm,), in_specs=in_specs, out_specs=out_specs, out_shape=out_shape,
        compiler_params=_params(("parallel",)), name="resid_norm",
    )(*operands)


def _final_norm_kernel(z_ref, delta_ref, gate_ref, gain_ref, out_ref):
    row = 1 + pl.program_id(0)
    z = z_ref[...] + gate_ref[pl.ds(row, 1), :] * delta_ref[...].astype(F32)
    out_ref[...] = _rmsnorm(z, gain_ref[...])


def _final_norm(z, delta, gate_mod, gate_slot, gain, batch, seq_rows, ctx_rows):
    t, d = z.shape
    tm = ROW_TILE
    lat_tiles = (seq_rows - ctx_rows) // tm
    tiles_per_seq = seq_rows // tm
    ctx_tiles = ctx_rows // tm
    row_spec = pl.BlockSpec((tm, d), lambda b, j: (b * tiles_per_seq + ctx_tiles + j, 0))
    return pl.pallas_call(
        _final_norm_kernel,
        grid=(batch, lat_tiles),
        in_specs=[row_spec, row_spec,
                  pl.BlockSpec((gate_mod.shape[0], d), lambda b, j: (0, gate_slot)),
                  pl.BlockSpec((1, d), lambda b, j: (0, 0))],
        out_specs=pl.BlockSpec((tm, d), lambda b, j: (b * lat_tiles + j, 0)),
        out_shape=jax.ShapeDtypeStruct((batch * lat_tiles * tm, d), F32),
        compiler_params=_params(("parallel", "parallel")), name="final_norm",
    )(z, delta, gate_mod, gain.reshape(1, d))


def _matmul_kernel(a_ref, w_ref, o_ref):
    o_ref[...] = _dot(a_ref[...], w_ref[...]).astype(o_ref.dtype)


def _matmul(a, w, layer, out_dtype, tm, tn):
    m, k = a.shape
    n = w.shape[2]
    return pl.pallas_call(
        _matmul_kernel,
        grid=(m // tm, n // tn),
        in_specs=[pl.BlockSpec((tm, k), lambda i, j: (i, 0)),
                  pl.BlockSpec((None, k, tn), lambda i, j: (layer, 0, j))],
        out_specs=pl.BlockSpec((tm, tn), lambda i, j: (i, j)),
        out_shape=jax.ShapeDtypeStruct((m, n), out_dtype),
        compiler_params=_params(("parallel", "arbitrary")), name="matmul",
    )(a, w)


def _gate_proj_kernel(a_ref, w_ref, wt_ref, oc_ref, or_ref):
    a = a_ref[...]
    oc_ref[...] = _dot(a, w_ref[...])
    or_ref[...] = _dot_nt(wt_ref[...], a)


def _gate_proj(a, w, wt, layer, tm):
    m, k = a.shape
    return pl.pallas_call(
        _gate_proj_kernel,
        grid=(m // tm,),
        in_specs=[pl.BlockSpec((tm, k), lambda i: (i, 0)),
                  pl.BlockSpec((None, k, LANES), lambda i: (layer, 0, 0)),
                  pl.BlockSpec((None, LANES, k), lambda i: (layer, 0, 0))],
        out_specs=[pl.BlockSpec((tm, LANES), lambda i: (i, 0)), pl.BlockSpec((LANES, tm), lambda i: (0, i))],
        out_shape=[jax.ShapeDtypeStruct((m, LANES), F32), jax.ShapeDtypeStruct((LANES, m), F32)],
        compiler_params=_params(("parallel",)), name="gate_proj",
    )(a, w, wt)


def _tri_masks(length, reverse):
    row = lax.broadcasted_iota(jnp.int32, (length, length), 0)
    col = lax.broadcasted_iota(jnp.int32, (length, length), 1)
    if reverse:
        return col >= row, row >= col
    return col <= row, row <= col


def _mlstm_kernel(q_ref, k_ref, v_ref, gc_ref, gr_ref, bc_ref, br_ref, h_ref, ct_ref, n_ref, m_ref, *,
                  reverse, heads, dqk, dv):
    @pl.when(pl.program_id(1) == 0)
    def _():
        ct_ref[...] = jnp.zeros_like(ct_ref)
        n_ref[...] = jnp.zeros_like(n_ref)
        m_ref[...] = jnp.zeros_like(m_ref)

    length = q_ref.shape[0]
    valid, valid_t = _tri_masks(length, reverse)
    gc = gc_ref[...] + bc_ref[...]
    gr = gr_ref[...] + br_ref[...]
    d0 = 2 * heads if reverse else 0
    scale = dqk ** -0.5
    for h in range(heads):
        i_col = gc[:, d0 + h:d0 + h + 1]
        f_col = _log_sigmoid(gc[:, d0 + heads + h:d0 + heads + h + 1])
        i_row = gr[d0 + h:d0 + h + 1, :]
        f_row = _log_sigmoid(gr[d0 + heads + h:d0 + heads + h + 1, :])
        b_col = jnp.sum(jnp.where(valid, f_row, 0.0), axis=1, keepdims=True)
        b_row = jnp.sum(jnp.where(valid_t, f_col, 0.0), axis=0, keepdims=True)
        b_end = jnp.sum(f_row, axis=1, keepdims=True)
        m_prev = m_ref[h][:, :1]
        log_d = jnp.where(valid, b_col - b_row + i_row, NEG)
        m_inter = b_col + m_prev
        m_t = jnp.maximum(m_inter, jnp.max(log_d, axis=1, keepdims=True))
        w_intra = jnp.exp(log_d - m_t) * scale
        w_inter = jnp.exp(m_inter - m_t) * scale
        q = q_ref[:, h * dqk:(h + 1) * dqk]
        k = k_ref[:, h * dqk:(h + 1) * dqk]
        v = v_ref[:, h * dv:(h + 1) * dv]
        s = _dot_nt(q, k) * w_intra
        num = _dot(s.astype(BF16), v) + w_inter * _dot(q, ct_ref[h].astype(BF16))
        qn = jnp.sum(q.astype(F32) * n_ref[h], axis=1, keepdims=True)
        den = jnp.sum(s, axis=1, keepdims=True) + w_inter * qn
        h_ref[:, h * dv:(h + 1) * dv] = num / jnp.maximum(jnp.abs(den), jnp.exp(-m_t))
        log_w = b_end - b_col + i_col
        m_new = jnp.maximum(b_end + m_prev, jnp.max(log_w, axis=0, keepdims=True))
        w_end = jnp.exp(log_w - m_new)
        decay = jnp.exp(b_end + m_prev - m_new)
        wv = (w_end * v.astype(F32)).astype(BF16)
        ct_ref[h] = decay * ct_ref[h] + _dot_tn(k, wv)
        n_ref[h] = decay * n_ref[h] + jnp.sum(w_end * k.astype(F32), axis=0, keepdims=True)
        m_ref[h] = jnp.broadcast_to(m_new, (1, LANES))


def _chunk_index(i, n_chunks, reverse):
    if not reverse:
        return i
    return jnp.where(i == 0, 0, n_chunks - i)


def _mlstm(p, col0, gates_c, gates_r, bias_c, bias_r, batch, seq_rows, reverse):
    t = p.shape[0]
    length = ROW_TILE
    n_chunks = seq_rows // length
    heads = A_HEADS
    dv = 256
    dqk = 128
    qw, vw = heads * dqk, heads * dv

    def rows(b, i):
        return b * n_chunks + _chunk_index(i, n_chunks, reverse)

    kern = functools.partial(_mlstm_kernel, reverse=reverse, heads=heads, dqk=dqk, dv=dv)
    return pl.pallas_call(
        kern,
        grid=(batch, n_chunks),
        in_specs=[
            pl.BlockSpec((length, qw), lambda b, i: (rows(b, i), col0 // qw)),
            pl.BlockSpec((length, qw), lambda b, i: (rows(b, i), col0 // qw + 1)),
            pl.BlockSpec((length, vw), lambda b, i: (rows(b, i), (col0 + 2 * qw) // vw)),
            pl.BlockSpec((length, LANES), lambda b, i: (rows(b, i), 0)),
            pl.BlockSpec((4 * heads, length), lambda b, i: (0, rows(b, i))),
            pl.BlockSpec((1, LANES), lambda b, i: (0, 0)),
            pl.BlockSpec((4 * heads, 1), lambda b, i: (0, 0)),
        ],
        out_specs=pl.BlockSpec((length, vw), lambda b, i: (rows(b, i), 0)),
        out_shape=jax.ShapeDtypeStruct((t, vw), F32),
        scratch_shapes=[pltpu.VMEM((heads, dqk, dv), F32), pltpu.VMEM((heads, 1, dqk), F32),
                        pltpu.VMEM((heads, 1, LANES), F32)],
        compiler_params=_params(("parallel", "arbitrary")), name="mlstm_scan",
    )(p, p, p, gates_c, gates_r, bias_c, bias_r)


def _rope(t, cosf, sinf, half):
    if 2 * half == LANES:
        partner = pltpu.roll(t, half, axis=1)
    else:
        lane = lax.broadcasted_iota(jnp.int32, t.shape, 1)
        partner = jnp.where((lane & (2 * half - 1)) < half, pltpu.roll(t, LANES - half, axis=1),
                            pltpu.roll(t, half, axis=1))
    return t * cosf + partner * sinf


def _retention_kernel(q_ref, k_ref, v_ref, cos_ref, sin_ref, dp_ref, o_ref, s_ref, *, reverse, heads, dk, dv):
    @pl.when(pl.program_id(1) == 0)
    def _():
        s_ref[...] = jnp.zeros_like(s_ref)

    length = q_ref.shape[0]
    row = lax.broadcasted_iota(jnp.int32, (length, length), 0)
    col = lax.broadcasted_iota(jnp.int32, (length, length), 1)
    diff = (col - row if reverse else row - col).astype(F32)
    tpos = lax.broadcasted_iota(jnp.int32, (length, 1), 0).astype(F32)
    lg_all = -_softplus(dp_ref[...])
    cosf = cos_ref[...]
    sinf = sin_ref[...]
    scale = dk ** -0.5
    d = 1 if reverse else 0
    for h in range(heads):
        lg = lg_all[d:d + 1, h:h + 1]
        decay_mat = jnp.where(diff >= 0.0, jnp.exp(lg * jnp.maximum(diff, 0.0)), 0.0) * scale
        if reverse:
            read_w = jnp.exp(lg * (length - tpos))
            write_w = jnp.exp(lg * tpos)
        else:
            read_w = jnp.exp(lg * (tpos + 1.0))
            write_w = jnp.exp(lg * (length - 1.0 - tpos))
        chunk_decay = jnp.exp(lg * float(length))
        q = _rope(q_ref[:, h * dk:(h + 1) * dk].astype(F32), cosf, sinf, dk // 2).astype(BF16)
        k = _rope(k_ref[:, h * dk:(h + 1) * dk].astype(F32), cosf, sinf, dk // 2).astype(BF16)
        v = v_ref[:, h * dv:(h + 1) * dv]
        s = _dot_nt(q, k) * decay_mat
        o_ref[:, h * dv:(h + 1) * dv] = _dot(s.astype(BF16), v) + read_w * _dot(q, s_ref[h].astype(BF16))
        wv = (write_w * v.astype(F32)).astype(BF16)
        s_ref[h] = chunk_decay * s_ref[h] + scale * _dot_tn(k, wv)


def _retention(p, col0, cosf, sinf, decay_p, batch, seq_rows, reverse):
    t = p.shape[0]
    length = ROW_TILE
    n_chunks = seq_rows // length
    heads = B_HEADS
    dk, dv = 128, 256
    qw, vw = heads * dk, heads * dv

    def rows(b, i):
        return b * n_chunks + _chunk_index(i, n_chunks, reverse)

    kern = functools.partial(_retention_kernel, reverse=reverse, heads=heads, dk=dk, dv=dv)
    return pl.pallas_call(
        kern,
        grid=(batch, n_chunks),
        in_specs=[
            pl.BlockSpec((length, qw), lambda b, i: (rows(b, i), col0 // qw)),
            pl.BlockSpec((length, qw), lambda b, i: (rows(b, i), col0 // qw + 1)),
            pl.BlockSpec((length, vw), lambda b, i: (rows(b, i), (col0 + 2 * qw) // vw)),
            pl.BlockSpec((length, LANES), lambda b, i: (_chunk_index(i, n_chunks, reverse), 0)),
            pl.BlockSpec((length, LANES), lambda b, i: (_chunk_index(i, n_chunks, reverse), 0)),
            pl.BlockSpec(decay_p.shape, lambda b, i: (0, 0)),
        ],
        out_specs=pl.BlockSpec((length, vw), lambda b, i: (rows(b, i), 0)),
        out_shape=jax.ShapeDtypeStruct((t, vw), F32),
        scratch_shapes=[pltpu.VMEM((heads, dk, dv), F32)],
        compiler_params=_params(("parallel", "arbitrary")), name="retention_scan",
    )(p, p, p, cosf, sinf, decay_p)


def _headnorm_kernel(hf_ref, hb_ref, gate_ref, gain_ref, y_ref, *, heads, dv, center, silu_gate):
    g = gate_ref[...].astype(F32)
    sg = _sigmoid(g)
    gate = g * sg if silu_gate else sg
    gain = gain_ref[...]
    for h in range(heads):
        sl = slice(h * dv, (h + 1) * dv)
        x = hf_ref[:, sl] + hb_ref[:, sl]
        if center:
            x = x - jnp.mean(x, axis=-1, keepdims=True)
        y = x * lax.rsqrt(jnp.mean(x * x, axis=-1, keepdims=True) + EPS)
        y_ref[:, sl] = (gate[:, sl] * (y * gain[:, sl])).astype(y_ref.dtype)


def _headnorm(hf, hb, p, gate_col, gain, heads, center, silu_gate):
    t, w = hf.shape
    tm = ROW_TILE
    kern = functools.partial(_headnorm_kernel, heads=heads, dv=w // heads, center=center, silu_gate=silu_gate)
    row_spec = pl.BlockSpec((tm, w), lambda i: (i, 0))
    return pl.pallas_call(
        kern, grid=(t // tm,),
        in_specs=[row_spec, row_spec, pl.BlockSpec((tm, w), lambda i: (i, gate_col // w)),
                  pl.BlockSpec((1, w), lambda i: (0, 0))],
        out_specs=row_spec,
        out_shape=jax.ShapeDtypeStruct((t, w), BF16),
        compiler_params=_params(("parallel",)), name="headnorm_gate",
    )(hf, hb, p, gain.reshape(1, w))


def _lane_low(shape, hd):
    return lax.broadcasted_iota(jnp.int32, shape, 1) < hd


def _stack_heads(q, hd):
    low = _lane_low(q.shape, hd)
    zero = jnp.zeros_like(q)
    return jnp.concatenate([jnp.where(low, q, zero), jnp.where(low, zero, q)], axis=0)


def _unstack_heads(o, hd):
    rows = o.shape[0] // 2
    return jnp.where(_lane_low((rows, LANES), hd), o[:rows], o[rows:])


def _window_kernel(sink_ref, q_ref, kp_ref, k0_ref, kn_ref, vp_ref, v0_ref, vn_ref, kc_ref, vc_ref,
                   cq_ref, sq_ref, cp_ref, sp_ref, cn_ref, sn_ref, o_ref, *, heads, kv_heads, hd, ctx_blocks,
                   n_latent, slabs_per_dot):
    i = pl.program_id(1)
    j = i - ctx_blocks
    w = q_ref.shape[0]
    half = hd // 2
    slabs_per_group = heads // kv_heads // 2

    def roped(x_ref, c_ref, s_ref):
        return _rope(x_ref[...].astype(F32), c_ref[...], s_ref[...], half)

    kb = jnp.concatenate([roped(kp_ref, cp_ref, sp_ref), roped(k0_ref, cq_ref, sq_ref),
                          roped(kn_ref, cn_ref, sn_ref)], axis=0)
    vb = jnp.concatenate([vp_ref[...], v0_ref[...], vn_ref[...]], axis=0).astype(F32)
    kc = kc_ref[...].astype(F32)
    vc = vc_ref[...].astype(F32)

    def group_copy(x, g):
        keep = _lane_low(x.shape, hd) if g == 0 else jnp.logical_not(_lane_low(x.shape, hd))
        return jnp.where(keep, x, pltpu.roll(x, hd, axis=1)).astype(BF16)

    c = lax.broadcasted_iota(jnp.int32, (3 * w, w), 0)
    t = lax.broadcasted_iota(jnp.int32, (3 * w, w), 1)
    lower = jnp.maximum(t + (w - WINDOW), (1 - j) * w)
    upper = jnp.minimum(t + (w + WINDOW), n_latent - 1 - (j - 1) * w)
    upper = jnp.where(j < 0, -1, upper)
    mask_bias = jnp.where((c >= lower) & (c <= upper), 0.0, NEG)
    mask_bias = jnp.concatenate([mask_bias] * (2 * slabs_per_dot), axis=1)
    scale = hd ** -0.5
    cq = cq_ref[...]
    sq = sq_ref[...]
    for g in range(kv_heads):
        kk, vv, kkc, vvc = group_copy(kb, g), group_copy(vb, g), group_copy(kc, g), group_copy(vc, g)
        for s0 in range(0, slabs_per_group, slabs_per_dot):
            slabs = [g * slabs_per_group + s0 + s for s in range(slabs_per_dot)]
            qs = []
            sinks = []
            for slab in slabs:
                qr = _rope(q_ref[:, slab * LANES:(slab + 1) * LANES].astype(F32), cq, sq, half)
                qs.append(_stack_heads((qr * scale).astype(BF16), hd))
                sinks += [jnp.full((1, w), sink_ref[2 * slab], F32), jnp.full((1, w), sink_ref[2 * slab + 1], F32)]
            qq = jnp.concatenate(qs, axis=0)
            sink = jnp.concatenate(sinks, axis=1)
            s_loc = _dot_nt(kk, qq) + mask_bias
            s_ctx = _dot_nt(kkc, qq)
            m = jnp.maximum(jnp.maximum(jnp.max(s_loc, axis=0, keepdims=True),
                                        jnp.max(s_ctx, axis=0, keepdims=True)), sink)
            e_loc = jnp.exp(s_loc - m)
            e_ctx = jnp.exp(s_ctx - m)
            den = (jnp.exp(sink - m) + jnp.sum(e_loc, axis=0, keepdims=True)
                   + jnp.sum(e_ctx, axis=0, keepdims=True))
            o_t = (_dot_tn(vv, e_loc.astype(BF16)) + _dot_tn(vvc, e_ctx.astype(BF16))) / den
            o = o_t.T
            for n, slab in enumerate(slabs):
                o_ref[:, slab * LANES:(slab + 1) * LANES] = _unstack_heads(
                    o[2 * n * w:2 * (n + 1) * w], hd).astype(o_ref.dtype)


def _window_attention(p, col0, cosf, sinf, sink, batch, seq_rows, ctx_rows):
    t = p.shape[0]
    w = WINDOW
    heads, kv_heads = C_HEADS, C_KV_HEADS
    hd = LANES // 2
    assert kv_heads * hd == LANES
    qw = heads * hd
    nblk = seq_rows // w
    ctx_blocks = ctx_rows // w
    kcol = (col0 + qw) // LANES
    vcol = kcol + 1

    def cur(b, i):
        return b * nblk + i

    def prev(b, i):
        return b * nblk + jnp.maximum(i - 1, 0)

    def nxt(b, i):
        return b * nblk + jnp.minimum(i + 1, nblk - 1)

    def kv_spec(fn, colblk):
        return pl.BlockSpec((w, LANES), lambda b, i: (fn(b, i), colblk))

    def tab_spec(fn):
        return pl.BlockSpec((w, LANES), lambda b, i: (fn(0, i), 0))

    kern = functools.partial(_window_kernel, heads=heads, kv_heads=kv_heads, hd=hd, ctx_blocks=ctx_blocks,
                             n_latent=seq_rows - ctx_rows, slabs_per_dot=4)
    ctx_spec_k = pl.BlockSpec((ctx_rows, LANES), lambda b, i: (b * (seq_rows // ctx_rows), kcol))
    ctx_spec_v = pl.BlockSpec((ctx_rows, LANES), lambda b, i: (b * (seq_rows // ctx_rows), vcol))
    return pl.pallas_call(
        kern,
        grid=(batch, nblk),
        in_specs=[
            pl.BlockSpec(memory_space=pltpu.SMEM),
            pl.BlockSpec((w, qw), lambda b, i: (cur(b, i), col0 // qw)),
            kv_spec(prev, kcol), kv_spec(cur, kcol), kv_spec(nxt, kcol),
            kv_spec(prev, vcol), kv_spec(cur, vcol), kv_spec(nxt, vcol),
            ctx_spec_k, ctx_spec_v,
            tab_spec(cur), tab_spec(cur), tab_spec(prev), tab_spec(prev), tab_spec(nxt), tab_spec(nxt),
        ],
        out_specs=pl.BlockSpec((w, qw), lambda b, i: (cur(b, i), 0)),
        out_shape=jax.ShapeDtypeStruct((t, qw), BF16),
        compiler_params=_params(("parallel", "arbitrary")), name="window_attention",
    )(sink, p, p, p, p, p, p, p, p, p, cosf, sinf, cosf, sinf, cosf, sinf)


def _nbr_layout():
    masked = 2 * NA_ROWS - 1
    variants = [(0, [0] * NBR_ROWS), (NA_ROWS // 2, list(range(NBR_ROWS))),
                (NBR_UNION - NBR_ROWS, [NBR_UNION - NA_ROWS] * NBR_ROWS)]
    pairs, index = [], []
    for delta, rel_start in variants:
        per_kr = []
        for kr in range(NBR_UNION):
            codes = [kr - (delta + rr) + NA_ROWS - 1 if rel_start[rr] <= kr < rel_start[rr] + NA_ROWS else masked
                     for rr in range(NBR_ROWS)]
            row = []
            for a in range(NBR_ROWS // 2):
                pair = (codes[2 * a], codes[2 * a + 1])
                if pair not in pairs:
                    pairs.append(pair)
                row.append(pairs.index(pair))
            per_kr.append(row)
        index.append(per_kr)
    return pairs, index


def _nbr_kernel(q_ref, k_ref, v_ref, tab_ref, o_ref, bias_ref, *, hd, ctx_rows, grid_rows, index):
    i = pl.program_id(2)
    scale = hd ** -0.5
    tq = q_ref.shape[0]
    qq = _stack_heads(q_ref[...] * scale, hd)
    kc = k_ref[0:ctx_rows, :]
    vc = v_ref[0:ctx_rows, :]
    s_ctx = _dot_nt(kc, qq)
    m_ctx = jnp.max(s_ctx, axis=0, keepdims=True)

    @pl.when(i == 0)
    def _():
        for variant, per_kr in enumerate(index):
            for kr, row in enumerate(per_kr):
                for sub in range(2):
                    for a, u in enumerate(row):
                        bias_ref[variant, kr * GRID_W:(kr + 1) * GRID_W,
                                 sub * tq + a * LANES:sub * tq + (a + 1) * LANES] = tab_ref[sub, u]
        e = jnp.exp(s_ctx - m_ctx)
        o_t = _dot_tn(vc, e.astype(BF16)) / jnp.sum(e, axis=0, keepdims=True)
        o_ref[...] = _unstack_heads(o_t.T, hd).astype(o_ref.dtype)

    @pl.when(i > 0)
    def _():
        blk = i - 1
        n_blk = grid_rows // NBR_ROWS
        u0 = jnp.clip(blk * NBR_ROWS - NA_ROWS // 2, 0, grid_rows - NBR_UNION)
        variant = jnp.where(blk == 0, 0, jnp.where(blk == n_blk - 1, 2, 1))
        base = pl.multiple_of(ctx_rows + u0 * GRID_W, GRID_W)
        kw = k_ref[pl.ds(base, NBR_UNION * GRID_W), :]
        vw = v_ref[pl.ds(base, NBR_UNION * GRID_W), :]
        s_loc = _dot_nt(kw, qq) + bias_ref[variant]
        m = jnp.maximum(jnp.max(s_loc, axis=0, keepdims=True), m_ctx)
        e_loc = jnp.exp(s_loc - m)
        e_ctx = jnp.exp(s_ctx - m)
        den = jnp.sum(e_loc, axis=0, keepdims=True) + jnp.sum(e_ctx, axis=0, keepdims=True)
        o_t = (_dot_tn(vw, e_loc.astype(BF16)) + _dot_tn(vc, e_ctx.astype(BF16))) / den
        o_ref[...] = _unstack_heads(o_t.T, hd).astype(o_ref.dtype)


def _nbr_table(na_bias_l, pairs):
    col = jnp.arange(GRID_W)
    col_start = jnp.clip(col - NA_COLS // 2, 0, GRID_W - NA_COLS)
    col_valid = (col[None, :] >= col_start[:, None]) & (col[None, :] < col_start[:, None] + NA_COLS)
    dc = jnp.clip(col[None, :] - col[:, None], -(NA_COLS - 1), NA_COLS - 1) + (NA_COLS - 1)
    blocks = jnp.where(col_valid.T[None, None], na_bias_l[:, :, dc.T], NEG)
    blocks = jnp.concatenate([blocks, jnp.full_like(blocks[:, :1], NEG)], axis=1)
    left = blocks[:, np.array([p[0] for p in pairs])]
    right = blocks[:, np.array([p[1] for p in pairs])]
    return jnp.concatenate([left, right], axis=-1).astype(F32)


def _nbr_attention(p, col0, na_bias_l, batch, seq_rows, ctx_rows):
    t = p.shape[0]
    heads = D_HEADS
    hd = LANES // 2
    qw = heads * hd
    slabs = heads // 2
    tq = NBR_ROWS * GRID_W
    n_latent = seq_rows - ctx_rows
    grid_rows = n_latent // GRID_W
    assert tq == ctx_rows and grid_rows % NBR_ROWS == 0 and grid_rows >= NBR_UNION + 1
    steps = seq_rows // tq
    q0 = col0 // LANES
    pairs, index = _nbr_layout()
    table = _nbr_table(na_bias_l, pairs)
    kern = functools.partial(_nbr_kernel, hd=hd, ctx_rows=ctx_rows, grid_rows=grid_rows, index=index)
    return pl.pallas_call(
        kern,
        grid=(batch, slabs, steps),
        in_specs=[
            pl.BlockSpec((tq, LANES), lambda b, s, i: (b * steps + i, q0 + s)),
            pl.BlockSpec((seq_rows, LANES), lambda b, s, i: (b, q0 + slabs + s)),
            pl.BlockSpec((seq_rows, LANES), lambda b, s, i: (b, q0 + 2 * slabs + s)),
            pl.BlockSpec((2,) + table.shape[1:], lambda b, s, i: (s, 0, 0, 0)),
        ],
        out_specs=pl.BlockSpec((tq, LANES), lambda b, s, i: (b * steps + i, s)),
        out_shape=jax.ShapeDtypeStruct((t, qw), BF16),
        scratch_shapes=[pltpu.VMEM((len(index), NBR_UNION * GRID_W, 2 * tq), F32)],
        compiler_params=_params(("parallel", "parallel", "arbitrary")), name="nbr_attention",
    )(p, p, p, table)


def _merge_kernel(*refs):
    ys = refs[0:N_BRANCH]
    gs = refs[N_BRANCH:2 * N_BRANCH]
    ws = refs[2 * N_BRANCH:3 * N_BRANCH]
    o_ref = refs[3 * N_BRANCH]
    acc = None
    for y_ref, g_ref, w_ref in zip(ys, gs, ws):
        term = _sigmoid(g_ref[...].astype(F32)) * _dot(y_ref[...], w_ref[...])
        acc = term if acc is None else acc + term
    o_ref[...] = acc.astype(o_ref.dtype)


def _merge(ys, p, gate_col0, w_branch, row0, tm, tn):
    t, bw = ys[0].shape
    d = w_branch.shape[1]
    nt = d // tn
    y_spec = pl.BlockSpec((tm, bw), lambda i, j: (i, 0))
    g_specs = [pl.BlockSpec((tm, tn), functools.partial(lambda i, j, br: (i, (gate_col0 + br * d) // tn + j), br=br))
               for br in range(N_BRANCH)]
    w_specs = [pl.BlockSpec((bw, tn), functools.partial(lambda i, j, br: (row0 // bw + br, j), br=br))
               for br in range(N_BRANCH)]
    return pl.pallas_call(
        _merge_kernel,
        grid=(t // tm, nt),
        in_specs=[y_spec] * N_BRANCH + g_specs + w_specs,
        out_specs=pl.BlockSpec((tm, tn), lambda i, j: (i, j)),
        out_shape=jax.ShapeDtypeStruct((t, d), BF16),
        compiler_params=_params(("parallel", "arbitrary")), name="merge_branches",
    )(*ys, *([p] * N_BRANCH), *([w_branch] * N_BRANCH))


def _moe_kernel(t_ref, gates_ref, w1_ref, b1_ref, w2_ref, b2_ref, o_ref, acc_ref, *, ff, group):
    e = pl.program_id(1)
    gates = gates_ref[...]

    @pl.when(e == 0)
    def _():
        acc_ref[...] = _dot(gates.astype(BF16), b2_ref[...])

    hid = _dot(t_ref[...], w1_ref[0]) + b1_ref[0]
    gw = group * ff
    g_h = jnp.minimum(hid[:, :gw], SWIGLU_LIMIT)
    u_h = jnp.clip(hid[:, gw:], -SWIGLU_LIMIT, SWIGLU_LIMIT)
    lane = lax.broadcasted_iota(jnp.int32, gates.shape, 1)
    cols = []
    for s in range(group):
        gsel = jnp.sum(jnp.where(lane == e * group + s, gates, 0.0), axis=-1, keepdims=True)
        cols.append(jnp.broadcast_to(gsel, (gates.shape[0], ff)))
    gate_cols = jnp.concatenate(cols, axis=-1)
    act = g_h * _sigmoid(SWIGLU_ALPHA * g_h) * (u_h + 1.0) * gate_cols
    acc_ref[...] += _dot(act.astype(BF16), w2_ref[...])

    @pl.when(e == pl.num_programs(1) - 1)
    def _():
        o_ref[...] = acc_ref[...].astype(o_ref.dtype)


def _moe(h, gates, w1p, b1p, w2, b2p, group0, tm):
    t, d = h.shape
    width = w1p.shape[2]
    group = 2
    ff = width // (2 * group)
    groups = b1p.shape[0]
    kern = functools.partial(_moe_kernel, ff=ff, group=group)
    return pl.pallas_call(
        kern,
        grid=(t // tm, groups),
        in_specs=[
            pl.BlockSpec((tm, d), lambda i, e: (i, 0)),
            pl.BlockSpec((tm, LANES), lambda i, e: (i, 0)),
            pl.BlockSpec((1, d, width), lambda i, e: (group0 + e, 0, 0)),
            pl.BlockSpec((1, 1, width), lambda i, e: (e, 0, 0)),
            pl.BlockSpec((group * ff, d), lambda i, e: (group0 + e, 0)),
            pl.BlockSpec(b2p.shape, lambda i, e: (0, 0)),
        ],
        out_specs=pl.BlockSpec((tm, d), lambda i, e: (i, 0)),
        out_shape=jax.ShapeDtypeStruct((t, d), BF16),
        scratch_shapes=[pltpu.VMEM((tm, d), F32)],
        compiler_params=_params(("parallel", "arbitrary")), name="moe_experts",
    )(h, gates, w1p, b1p, w2, b2p)


def _rope_tables(n_latent, ctx_rows, hd):
    pos = np.arange(n_latent)
    row = (pos // GRID_W).astype(np.float32)
    col = (pos % GRID_W).astype(np.float32)
    n_freq = hd // 4
    inv = jnp.asarray(ROPE_BASE, F32) ** (-jnp.arange(n_freq, dtype=F32) / n_freq)
    ang = jnp.concatenate([jnp.asarray(row)[:, None] * inv, jnp.asarray(col)[:, None] * inv], axis=-1)
    ang = jnp.concatenate([jnp.zeros((ctx_rows, hd // 2), F32), ang], axis=0)
    cos, sin = jnp.cos(ang), jnp.sin(ang)
    reps = LANES // hd
    cosf = jnp.tile(jnp.concatenate([cos, cos], axis=-1), (1, reps))
    sinf = jnp.tile(jnp.concatenate([-sin, sin], axis=-1), (1, reps))
    return cosf, sinf


def _pad_lanes(a, value=0.0):
    pad = LANES - a.shape[-1]
    return jnp.pad(a, [(0, 0)] * (a.ndim - 1) + [(0, pad)], constant_values=value)


def kernel(x, c, ctx, c_ctx, ada_down, ada_up, ada_bias, norm_gain, w_in, mlstm_gate_bias, mlstm_norm_gain,
           ret_decay, ret_norm_gain, sink, na_bias, w_branch, w_out, router_w, router_b, moe_w1, moe_b1,
           moe_w2, moe_b2, final_gain):
    batch, n_latent, d = x.shape
    ctx_rows = ctx.shape[1]
    depth = w_in.shape[0]
    seq_rows = ctx_rows + n_latent
    t = batch * seq_rows
    bw = d // N_BRANCH
    n_experts = router_w.shape[2]
    ff = moe_w2.shape[2]

    a_sz = (A_HEADS * 128, A_HEADS * 128, bw, bw)
    n_gate = 4 * A_HEADS
    b_sz = (B_HEADS * 128, B_HEADS * 128, bw, bw)
    c_sz = (bw, C_KV_HEADS * 64, C_KV_HEADS * 64)
    d_sz = (bw, bw, bw)
    sizes = a_sz + (n_gate,) + b_sz + c_sz + d_sz + (N_BRANCH * d,)
    offs = [int(o) for o in np.concatenate([[0], np.cumsum(sizes)])]
    a0, g0, b0, c0, d0, bg0, end = offs[0], offs[4], offs[5], offs[9], offs[12], offs[15], offs[16]
    col_gate = 0
    col_a = N_BRANCH * d
    col_b = col_a + sum(a_sz)
    col_d = col_b + sum(b_sz)
    col_c = col_d + sum(d_sz)

    w_cat, w_g, w_gt = _win_prep(w_in, [(bg0, end), (a0, g0), (b0, c0), (d0, bg0), (c0, d0)], (g0, b0))
    w1p = _moe_w1_prep(moe_w1)
    wb_bf, wo_bf, w2_bf = _cast_bf16(w_branch.reshape(depth * N_BRANCH * bw, d), w_out.reshape(depth * d, d),
                                     moe_w2.reshape(depth * n_experts * ff, d))
    b1p_all = jnp.concatenate([moe_b1[..., 0::2].reshape(depth, n_experts // 2, 1, 2 * ff),
                               moe_b1[..., 1::2].reshape(depth, n_experts // 2, 1, 2 * ff)], axis=-1)
    b2p_all = jnp.pad(moe_b2, ((0, 0), (0, LANES - n_experts), (0, 0))).astype(BF16)

    mod_all = _ada_modulation(jnp.pad(jnp.concatenate([c_ctx[None], c], axis=0), ((0, 8 - 1 - batch), (0, 0))),
                              ada_down, ada_up, ada_bias)
    cos_b, sin_b = _rope_tables(n_latent, ctx_rows, 128)
    cos_c, sin_c = _rope_tables(n_latent, ctx_rows, 64)
    decay_p = jnp.pad(ret_decay, ((0, 0), (0, 8 - ret_decay.shape[1]), (0, LANES - ret_decay.shape[2])))

    tm_proj = _row_tile(t, 768)
    tm_moe = _row_tile(t, 512)
    z = (ctx.reshape(batch * ctx_rows, d), x.reshape(batch * n_latent, d))
    delta = None
    for l in range(depth):
        mod = mod_all[l]
        if l == 0:
            h = _resid_norm(z, norm_gain[l, 0], mod, 0, 1, batch, seq_rows, ctx_rows)[0]
        else:
            z, h = _resid_norm(z, norm_gain[l, 0], mod, 0, 1, batch, seq_rows, ctx_rows, delta=delta,
                               gate_mod=mod_all[l - 1], gate_slot=5)
        p = _matmul(h, w_cat, l, BF16, tm=tm_proj, tn=768)
        gates_c, gates_r = _gate_proj(h, w_g, w_gt, l, tm_proj)
        gates_r = gates_r[:n_gate]
        bias_c = _pad_lanes(mlstm_gate_bias[l][None])
        bias_r = mlstm_gate_bias[l][:, None]

        hf = _mlstm(p, col_a, gates_c, gates_r, bias_c, bias_r, batch, seq_rows, False)
        hb = _mlstm(p, col_a, gates_c, gates_r, bias_c, bias_r, batch, seq_rows, True)
        y_a = _headnorm(hf, hb, p, col_a + 2 * A_HEADS * 128 + bw, mlstm_norm_gain[l], A_HEADS, False, False)
        of = _retention(p, col_b, cos_b, sin_b, decay_p[l], batch, seq_rows, False)
        ob = _retention(p, col_b, cos_b, sin_b, decay_p[l], batch, seq_rows, True)
        y_b = _headnorm(of, ob, p, col_b + 2 * B_HEADS * 128 + bw, ret_norm_gain[l], B_HEADS, True, True)
        y_c = _window_attention(p, col_c, cos_c, sin_c, sink[l], batch, seq_rows, ctx_rows)
        y_d = _nbr_attention(p, col_d, na_bias[l], batch, seq_rows, ctx_rows)

        acc = _merge([y_a, y_b, y_c, y_d], p, col_gate, wb_bf, l * N_BRANCH * bw, tm=tm_moe, tn=1024)
        delta1 = _matmul(acc, wo_bf.reshape(depth, d, d), l, BF16, tm=tm_proj, tn=1024)

        rw = _pad_lanes(router_w[l]).astype(BF16)
        rb = _pad_lanes(router_b[l][None], NEG)
        z, h2, gates = _resid_norm(z, norm_gain[l, 1], mod, 3, 4, batch, seq_rows, ctx_rows, delta=delta1,
                                   gate_mod=mod, gate_slot=2, router=(rw, rb))
        delta = _moe(h2, gates, w1p, b1p_all[l], w2_bf, b2p_all[l], l * (n_experts // 2), tm=tm_moe)

    out = _final_norm(z, delta, mod_all[depth - 1], 5, final_gain, batch, seq_rows, ctx_rows)
    return out.reshape(batch, n_latent, d)
```

```python
import functools

import jax
import jax.numpy as jnp
import numpy as np
from jax import lax
from jax.experimental import pallas as pl
from jax.experimental.pallas import tpu as pltpu

F32 = jnp.float32
BF16 = jnp.bfloat16

GRID_W = 64
EPS = 1e-6
NEG = -1e30
ROPE_BASE = 10000.0
N_BRANCH = 4
A_HEADS = 4
B_HEADS = 4
C_HEADS = 16
C_KV_HEADS = 2
D_HEADS = 16
WINDOW = 128
NA_ROWS = 8
NA_COLS = 16
TOP_K = 4
SWIGLU_LIMIT = 7.0
SWIGLU_ALPHA = 1.702

LANES = 128
VMEM_LIMIT = 56 * 1024 * 1024
ROW_TILE = 256
NBR_ROWS = 4
NBR_UNION = NBR_ROWS + NA_ROWS - 1


def _row_tile(rows, target):
    best = ROW_TILE
    for tile in range(ROW_TILE, target + 1, ROW_TILE):
        if rows % tile == 0:
            best = tile
    assert rows % best == 0
    return best


def _params(sem, vmem=VMEM_LIMIT):
    return pltpu.CompilerParams(dimension_semantics=sem, vmem_limit_bytes=vmem)


def _dot(a, b):
    return jnp.dot(a, b, preferred_element_type=F32)


def _dot_nt(a, b):
    return lax.dot_general(a, b, (((1,), (1,)), ((), ())), preferred_element_type=F32)


def _dot_tn(a, b):
    return lax.dot_general(a, b, (((0,), (0,)), ((), ())), preferred_element_type=F32)


def _sigmoid(x):
    return 1.0 / (1.0 + jnp.exp(-x))


def _softplus(x):
    return jnp.maximum(x, 0.0) + jnp.log1p(jnp.exp(-jnp.abs(x)))


def _log_sigmoid(x):
    return -_softplus(-x)


def _win_prep_kernel(w_ref, g_ref, cat_ref, gt_ref):
    cat_ref[...] = w_ref[0].astype(BF16)

    @pl.when(pl.program_id(1) == 0)
    def _():
        g = g_ref[0].astype(BF16)
        gt_ref[...] = jnp.concatenate([g, jnp.zeros((LANES - g.shape[0], g.shape[1]), BF16)], axis=0)


def _win_prep(w_in_t, segments, gate_seg):
    depth, _, d = w_in_t.shape
    blk = ROW_TILE
    sub = 8
    assert all(a % sub == 0 and (b - a) % blk == 0 for a, b in segments)
    n_blk = sum((b - a) // blk for a, b in segments)
    ga, gb = gate_seg

    def src_row(j):
        s = jnp.int32(0)
        base = 0
        for a, b in segments:
            n = (b - a) // blk
            s = jnp.where((j >= base) & (j < base + n), a // sub + (j - base) * (blk // sub), s)
            base += n
        return s * sub

    return pl.pallas_call(
        _win_prep_kernel, grid=(depth, n_blk),
        in_specs=[pl.BlockSpec((pl.Element(1), pl.Element(blk), pl.Element(d)), lambda l, j: (l, src_row(j), 0)),
                  pl.BlockSpec((pl.Element(1), pl.Element(gb - ga), pl.Element(d)), lambda l, j: (l, ga, 0))],
        out_specs=[pl.BlockSpec((None, blk, d), lambda l, j: (l, j, 0)),
                   pl.BlockSpec((None, LANES, d), lambda l, j: (l, 0, 0))],
        out_shape=[jax.ShapeDtypeStruct((depth, n_blk * blk, d), BF16),
                   jax.ShapeDtypeStruct((depth, LANES, d), BF16)],
        compiler_params=_params(("parallel", "arbitrary")), name="win_prep",
    )(w_in_t, w_in_t)


def _moe_w1_prep_kernel(w_ref, perm_ref, o_ref, *, ff):
    parts = [_dot(w_ref[0, s].astype(BF16), perm_ref[...]) for s in range(2)]
    o_ref[0] = jnp.concatenate([parts[0][:, :ff], parts[1][:, :ff], parts[0][:, ff:], parts[1][:, ff:]],
                               axis=1).astype(BF16)


def _moe_w1_prep(moe_w1):
    depth, n_exp, d, two_ff = moe_w1.shape
    ff = two_ff // 2
    rows = 1024
    src = np.concatenate([np.arange(0, two_ff, 2), np.arange(1, two_ff, 2)])
    perm = jnp.asarray(np.arange(two_ff)[:, None] == src[None, :], BF16)
    kern = functools.partial(_moe_w1_prep_kernel, ff=ff)
    return pl.pallas_call(
        kern, grid=(depth * n_exp // 2, d // rows),
        in_specs=[pl.BlockSpec((1, 2, rows, two_ff), lambda e, i: (e, 0, i, 0)),
                  pl.BlockSpec((two_ff, two_ff), lambda e, i: (0, 0))],
        out_specs=pl.BlockSpec((1, rows, 2 * two_ff), lambda e, i: (e, i, 0)),
        out_shape=jax.ShapeDtypeStruct((depth * n_exp // 2, d, 2 * two_ff), BF16),
        compiler_params=_params(("parallel", "parallel")), name="moe_w1_prep",
    )(moe_w1.reshape(depth * n_exp // 2, 2, d, two_ff), perm)


def _cast_kernel(*refs):
    n = len(refs) // 2
    for src, dst in zip(refs[:n], refs[n:]):
        dst[...] = src[...].astype(dst.dtype)


def _cast_bf16(*arrays):
    rows, cols = arrays[0].shape
    tm = ROW_TILE
    spec = pl.BlockSpec((tm, cols), lambda i: (i, 0))
    return pl.pallas_call(
        _cast_kernel, grid=(rows // tm,),
        in_specs=[spec] * len(arrays), out_specs=[spec] * len(arrays),
        out_shape=[jax.ShapeDtypeStruct((rows, cols), BF16)] * len(arrays),
        compiler_params=_params(("parallel",)), name="cast_bf16",
    )(*arrays)


def _ada_kernel(cv_ref, down_ref, up_ref, bias_ref, out_ref):
    cv = cv_ref[...]
    a = cv * _sigmoid(cv)
    z = _dot(a.astype(BF16), down_ref[0].astype(BF16))
    out_ref[0] = _dot(z.astype(BF16), up_ref[0].astype(BF16)) + bias_ref[0]


def _ada_modulation(cvecs, ada_down, ada_up, ada_bias):
    depth, d, r = ada_down.shape
    rows = cvecs.shape[0]
    return pl.pallas_call(
        _ada_kernel,
        grid=(depth, 6),
        in_specs=[
            pl.BlockSpec((rows, d), lambda l, j: (0, 0)),
            pl.BlockSpec((1, d, r), lambda l, j: (l, 0, 0)),
            pl.BlockSpec((1, r, d), lambda l, j: (l, 0, j)),
            pl.BlockSpec((1, 1, d), lambda l, j: (l, 0, j)),
        ],
        out_specs=pl.BlockSpec((1, rows, d), lambda l, j: (l, 0, j)),
        out_shape=jax.ShapeDtypeStruct((depth, rows, 6 * d), F32),
        compiler_params=_params(("parallel", "arbitrary")),
        name="ada_modulation",
    )(cvecs, ada_down, ada_up, ada_bias.reshape(depth, 1, 6 * d))


def _rmsnorm(x, gain):
    return x * lax.rsqrt(jnp.mean(x * x, axis=-1, keepdims=True) + EPS) * gain


def _resid_norm_kernel(*refs, split_src, has_delta, has_router, tiles_per_seq, ctx_tiles):
    refs = list(refs)
    tile = pl.program_id(0)
    b = tile // tiles_per_seq
    j = tile - b * tiles_per_seq
    is_ctx = j < ctx_tiles
    row = jnp.where(is_ctx, 0, 1 + b)
    if split_src:
        c_ref = refs.pop(0)
        x_ref = refs.pop(0)
        z = jnp.where(is_ctx, c_ref[...], x_ref[...])
    else:
        z = refs.pop(0)[...]
    if has_delta:
        delta_ref = refs.pop(0)
        gate_ref = refs.pop(0)
    gain_ref = refs.pop(0)
    shift_ref = refs.pop(0)
    scale_ref = refs.pop(0)
    if has_router:
        rw_ref = refs.pop(0)
        rb_ref = refs.pop(0)
    if has_delta:
        znew_ref = refs.pop(0)
    h_ref = refs.pop(0)
    if has_router:
        gates_ref = refs.pop(0)

    if has_delta:
        z = z + gate_ref[pl.ds(row, 1), :] * delta_ref[...].astype(F32)
        znew_ref[...] = z
    y = _rmsnorm(z, gain_ref[...])
    h = y * (1.0 + scale_ref[pl.ds(row, 1), :]) + shift_ref[pl.ds(row, 1), :]
    hb = h.astype(BF16)
    h_ref[...] = hb
    if has_router:
        logits = _dot(hb, rw_ref[...]) + rb_ref[...]
        lane = lax.broadcasted_iota(jnp.int32, logits.shape, 1).astype(F32)
        gates = jnp.zeros_like(logits)
        den = jnp.zeros((logits.shape[0], 1), F32)
        top = None
        for _ in range(TOP_K):
            m = jnp.max(logits, axis=-1, keepdims=True)
            idx = jnp.min(jnp.where(logits == m, lane, float(LANES)), axis=-1, keepdims=True)
            sel = lane == idx
            if top is None:
                top = m
            e = jnp.exp(m - top)
            den = den + e
            gates = jnp.where(sel, e, gates)
            logits = jnp.where(sel, NEG * 2.0, logits)
        gates_ref[...] = gates / den


def _resid_norm(src, gain, mod, shift_slot, scale_slot, batch, seq_rows, ctx_rows, delta=None, gate_mod=None,
                gate_slot=None, router=None):
    split_src = isinstance(src, tuple)
    d = src[0].shape[1] if split_src else src.shape[1]
    t = batch * seq_rows
    tm = ROW_TILE
    tiles_per_seq = seq_rows // tm
    ctx_tiles = ctx_rows // tm
    lat_tiles = tiles_per_seq - ctx_tiles
    has_delta = delta is not None
    has_router = router is not None
    row_spec = pl.BlockSpec((tm, d), lambda i: (i, 0))
    vec_spec = pl.BlockSpec((1, d), lambda i: (0, 0))

    def mod_spec(slot):
        return pl.BlockSpec((mod.shape[0], d), lambda i: (0, slot))

    def ctx_map(i):
        b = i // tiles_per_seq
        return b * ctx_tiles + jnp.minimum(i - b * tiles_per_seq, ctx_tiles - 1), 0

    def lat_map(i):
        b = i // tiles_per_seq
        return b * lat_tiles + jnp.maximum(i - b * tiles_per_seq - ctx_tiles, 0), 0

    if split_src:
        operands, in_specs = list(src), [pl.BlockSpec((tm, d), ctx_map), pl.BlockSpec((tm, d), lat_map)]
    else:
        operands, in_specs = [src], [row_spec]
    if has_delta:
        operands += [delta, gate_mod]
        in_specs += [row_spec, mod_spec(gate_slot)]
    operands += [gain.reshape(1, d), mod, mod]
    in_specs += [vec_spec, mod_spec(shift_slot), mod_spec(scale_slot)]
    if has_router:
        rw, rb = router
        operands += [rw, rb]
        in_specs += [pl.BlockSpec(rw.shape, lambda i: (0, 0)), pl.BlockSpec(rb.shape, lambda i: (0, 0))]
    out_shape, out_specs = [], []
    if has_delta:
        out_shape.append(jax.ShapeDtypeStruct((t, d), F32))
        out_specs.append(row_spec)
    out_shape.append(jax.ShapeDtypeStruct((t, d), BF16))
    out_specs.append(row_spec)
    if has_router:
        out_shape.append(jax.ShapeDtypeStruct((t, LANES), F32))
        out_specs.append(pl.BlockSpec((tm, LANES), lambda i: (i, 0)))
    kern = functools.partial(_resid_norm_kernel, split_src=split_src, has_delta=has_delta, has_router=has_router,
                             tiles_per_seq=tiles_per_seq, ctx_tiles=ctx_tiles)
    return pl.pallas_call(
        kern, grid=(t // tm,), in_specs=in_specs, out_specs=out_specs, out_shape=out_shape,
        compiler_params=_params(("parallel",)), name="resid_norm",
    )(*operands)


def _final_norm_kernel(z_ref, delta_ref, gate_ref, gain_ref, out_ref):
    row = 1 + pl.program_id(0)
    z = z_ref[...] + gate_ref[pl.ds(row, 1), :] * delta_ref[...].astype(F32)
    out_ref[...] = _rmsnorm(z, gain_ref[...])


def _final_norm(z, delta, gate_mod, gate_slot, gain, batch, seq_rows, ctx_rows):
    t, d = z.shape
    tm = ROW_TILE
    lat_tiles = (seq_rows - ctx_rows) // tm
    tiles_per_seq = seq_rows // tm
    ctx_tiles = ctx_rows // tm
    row_spec = pl.BlockSpec((tm, d), lambda b, j: (b * tiles_per_seq + ctx_tiles + j, 0))
    return pl.pallas_call(
        _final_norm_kernel,
        grid=(batch, lat_tiles),
        in_specs=[row_spec, row_spec,
                  pl.BlockSpec((gate_mod.shape[0], d), lambda b, j: (0, gate_slot)),
                  pl.BlockSpec((1, d), lambda b, j: (0, 0))],
        out_specs=pl.BlockSpec((tm, d), lambda b, j: (b * lat_tiles + j, 0)),
        out_shape=jax.ShapeDtypeStruct((batch * lat_tiles * tm, d), F32),
        compiler_params=_params(("parallel", "parallel")), name="final_norm",
    )(z, delta, gate_mod, gain.reshape(1, d))


def _matmul_kernel(a_ref, w_ref, o_ref, *, w_transposed):
    dot = _dot_nt if w_transposed else _dot
    o_ref[...] = dot(a_ref[...], w_ref[...]).astype(o_ref.dtype)


def _matmul(a, w, layer, out_dtype, tm, tn, w_transposed=False):
    m, k = a.shape
    if w_transposed:
        n = w.shape[1]
        w_spec = pl.BlockSpec((None, tn, k), lambda i, j: (layer, j, 0))
    else:
        n = w.shape[2]
        w_spec = pl.BlockSpec((None, k, tn), lambda i, j: (layer, 0, j))
    return pl.pallas_call(
        functools.partial(_matmul_kernel, w_transposed=w_transposed),
        grid=(m // tm, n // tn),
        in_specs=[pl.BlockSpec((tm, k), lambda i, j: (i, 0)), w_spec],
        out_specs=pl.BlockSpec((tm, tn), lambda i, j: (i, j)),
        out_shape=jax.ShapeDtypeStruct((m, n), out_dtype),
        compiler_params=_params(("parallel", "arbitrary")), name="matmul",
    )(a, w)


def _gate_proj_kernel(a_ref, wt_ref, oc_ref, or_ref):
    a = a_ref[...]
    wt = wt_ref[...]
    oc_ref[...] = _dot_nt(a, wt)
    or_ref[...] = _dot_nt(wt, a)


def _gate_proj(a, wt, layer, tm):
    m, k = a.shape
    return pl.pallas_call(
        _gate_proj_kernel,
        grid=(m // tm,),
        in_specs=[pl.BlockSpec((tm, k), lambda i: (i, 0)),
                  pl.BlockSpec((None, LANES, k), lambda i: (layer, 0, 0))],
        out_specs=[pl.BlockSpec((tm, LANES), lambda i: (i, 0)), pl.BlockSpec((LANES, tm), lambda i: (0, i))],
        out_shape=[jax.ShapeDtypeStruct((m, LANES), F32), jax.ShapeDtypeStruct((LANES, m), F32)],
        compiler_params=_params(("parallel",)), name="gate_proj",
    )(a, wt)


def _tri_masks(length, reverse):
    row = lax.broadcasted_iota(jnp.int32, (length, length), 0)
    col = lax.broadcasted_iota(jnp.int32, (length, length), 1)
    if reverse:
        return col >= row, row >= col
    return col <= row, row <= col


def _mlstm_kernel(q_ref, k_ref, v_ref, gc_ref, gr_ref, bc_ref, br_ref, h_ref, ct_ref, n_ref, m_ref, *,
                  reverse, heads, dqk, dv):
    @pl.when(pl.program_id(1) == 0)
    def _():
        ct_ref[...] = jnp.zeros_like(ct_ref)
        n_ref[...] = jnp.zeros_like(n_ref)
        m_ref[...] = jnp.zeros_like(m_ref)

    length = q_ref.shape[0]
    valid, valid_t = _tri_masks(length, reverse)
    gc = gc_ref[...] + bc_ref[...]
    gr = gr_ref[...] + br_ref[...]
    d0 = 2 * heads if reverse else 0
    scale = dqk ** -0.5
    for h in range(heads):
        i_col = gc[:, d0 + h:d0 + h + 1]
        f_col = _log_sigmoid(gc[:, d0 + heads + h:d0 + heads + h + 1])
        i_row = gr[d0 + h:d0 + h + 1, :]
        f_row = _log_sigmoid(gr[d0 + heads + h:d0 + heads + h + 1, :])
        b_col = jnp.sum(jnp.where(valid, f_row, 0.0), axis=1, keepdims=True)
        b_row = jnp.sum(jnp.where(valid_t, f_col, 0.0), axis=0, keepdims=True)
        b_end = jnp.sum(f_row, axis=1, keepdims=True)
        m_prev = m_ref[h][:, :1]
        log_d = jnp.where(valid, b_col - b_row + i_row, NEG)
        m_inter = b_col + m_prev
        m_t = jnp.maximum(m_inter, jnp.max(log_d, axis=1, keepdims=True))
        w_intra = jnp.exp(log_d - m_t) * scale
        w_inter = jnp.exp(m_inter - m_t) * scale
        q = q_ref[:, h * dqk:(h + 1) * dqk]
        k = k_ref[:, h * dqk:(h + 1) * dqk]
        v = v_ref[:, h * dv:(h + 1) * dv]
        s = _dot_nt(q, k) * w_intra
        num = _dot(s.astype(BF16), v) + w_inter * _dot(q, ct_ref[h].astype(BF16))
        qn = jnp.sum(q.astype(F32) * n_ref[h], axis=1, keepdims=True)
        den = jnp.sum(s, axis=1, keepdims=True) + w_inter * qn
        h_ref[:, h * dv:(h + 1) * dv] = num / jnp.maximum(jnp.abs(den), jnp.exp(-m_t))
        log_w = b_end - b_col + i_col
        m_new = jnp.maximum(b_end + m_prev, jnp.max(log_w, axis=0, keepdims=True))
        w_end = jnp.exp(log_w - m_new)
        decay = jnp.exp(b_end + m_prev - m_new)
        wv = (w_end * v.astype(F32)).astype(BF16)
        ct_ref[h] = decay * ct_ref[h] + _dot_tn(k, wv)
        n_ref[h] = decay * n_ref[h] + jnp.sum(w_end * k.astype(F32), axis=0, keepdims=True)
        m_ref[h] = jnp.broadcast_to(m_new, (1, LANES))


def _chunk_index(i, n_chunks, reverse):
    if not reverse:
        return i
    return jnp.where(i == 0, 0, n_chunks - i)


def _mlstm(p, col0, gates_c, gates_r, bias_c, bias_r, batch, seq_rows, reverse):
    t = p.shape[0]
    length = ROW_TILE
    n_chunks = seq_rows // length
    heads = A_HEADS
    dv = 256
    dqk = 128
    qw, vw = heads * dqk, heads * dv

    def rows(b, i):
        return b * n_chunks + _chunk_index(i, n_chunks, reverse)

    kern = functools.partial(_mlstm_kernel, reverse=reverse, heads=heads, dqk=dqk, dv=dv)
    return pl.pallas_call(
        kern,
        grid=(batch, n_chunks),
        in_specs=[
            pl.BlockSpec((length, qw), lambda b, i: (rows(b, i), col0 // qw)),
            pl.BlockSpec((length, qw), lambda b, i: (rows(b, i), col0 // qw + 1)),
            pl.BlockSpec((length, vw), lambda b, i: (rows(b, i), (col0 + 2 * qw) // vw)),
            pl.BlockSpec((length, LANES), lambda b, i: (rows(b, i), 0)),
            pl.BlockSpec((4 * heads, length), lambda b, i: (0, rows(b, i))),
            pl.BlockSpec((1, LANES), lambda b, i: (0, 0)),
            pl.BlockSpec((4 * heads, 1), lambda b, i: (0, 0)),
        ],
        out_specs=pl.BlockSpec((length, vw), lambda b, i: (rows(b, i), 0)),
        out_shape=jax.ShapeDtypeStruct((t, vw), F32),
        scratch_shapes=[pltpu.VMEM((heads, dqk, dv), F32), pltpu.VMEM((heads, 1, dqk), F32),
                        pltpu.VMEM((heads, 1, LANES), F32)],
        compiler_params=_params(("parallel", "arbitrary")), name="mlstm_scan",
    )(p, p, p, gates_c, gates_r, bias_c, bias_r)


def _rope(t, cosf, sinf, half):
    if 2 * half == LANES:
        partner = pltpu.roll(t, half, axis=1)
    else:
        lane = lax.broadcasted_iota(jnp.int32, t.shape, 1)
        partner = jnp.where((lane & (2 * half - 1)) < half, pltpu.roll(t, LANES - half, axis=1),
                            pltpu.roll(t, half, axis=1))
    return t * cosf + partner * sinf


def _retention_kernel(q_ref, k_ref, v_ref, cos_ref, sin_ref, dp_ref, o_ref, s_ref, *, reverse, heads, dk, dv):
    @pl.when(pl.program_id(1) == 0)
    def _():
        s_ref[...] = jnp.zeros_like(s_ref)

    length = q_ref.shape[0]
    row = lax.broadcasted_iota(jnp.int32, (length, length), 0)
    col = lax.broadcasted_iota(jnp.int32, (length, length), 1)
    diff = (col - row if reverse else row - col).astype(F32)
    tpos = lax.broadcasted_iota(jnp.int32, (length, 1), 0).astype(F32)
    lg_all = -_softplus(dp_ref[...])
    cosf = cos_ref[...]
    sinf = sin_ref[...]
    scale = dk ** -0.5
    d = 1 if reverse else 0
    for h in range(heads):
        lg = lg_all[d:d + 1, h:h + 1]
        decay_mat = jnp.where(diff >= 0.0, jnp.exp(lg * jnp.maximum(diff, 0.0)), 0.0) * scale
        if reverse:
            read_w = jnp.exp(lg * (length - tpos))
            write_w = jnp.exp(lg * tpos)
        else:
            read_w = jnp.exp(lg * (tpos + 1.0))
            write_w = jnp.exp(lg * (length - 1.0 - tpos))
        chunk_decay = jnp.exp(lg * float(length))
        q = _rope(q_ref[:, h * dk:(h + 1) * dk].astype(F32), cosf, sinf, dk // 2).astype(BF16)
        k = _rope(k_ref[:, h * dk:(h + 1) * dk].astype(F32), cosf, sinf, dk // 2).astype(BF16)
        v = v_ref[:, h * dv:(h + 1) * dv]
        s = _dot_nt(q, k) * decay_mat
        o_ref[:, h * dv:(h + 1) * dv] = _dot(s.astype(BF16), v) + read_w * _dot(q, s_ref[h].astype(BF16))
        wv = (write_w * v.astype(F32)).astype(BF16)
        s_ref[h] = chunk_decay * s_ref[h] + scale * _dot_tn(k, wv)


def _retention(p, col0, cosf, sinf, decay_p, batch, seq_rows, reverse):
    t = p.shape[0]
    length = ROW_TILE
    n_chunks = seq_rows // length
    heads = B_HEADS
    dk, dv = 128, 256
    qw, vw = heads * dk, heads * dv

    def rows(b, i):
        return b * n_chunks + _chunk_index(i, n_chunks, reverse)

    kern = functools.partial(_retention_kernel, reverse=reverse, heads=heads, dk=dk, dv=dv)
    return pl.pallas_call(
        kern,
        grid=(batch, n_chunks),
        in_specs=[
            pl.BlockSpec((length, qw), lambda b, i: (rows(b, i), col0 // qw)),
            pl.BlockSpec((length, qw), lambda b, i: (rows(b, i), col0 // qw + 1)),
            pl.BlockSpec((length, vw), lambda b, i: (rows(b, i), (col0 + 2 * qw) // vw)),
            pl.BlockSpec((length, LANES), lambda b, i: (_chunk_index(i, n_chunks, reverse), 0)),
            pl.BlockSpec((length, LANES), lambda b, i: (_chunk_index(i, n_chunks, reverse), 0)),
            pl.BlockSpec(decay_p.shape, lambda b, i: (0, 0)),
        ],
        out_specs=pl.BlockSpec((length, vw), lambda b, i: (rows(b, i), 0)),
        out_shape=jax.ShapeDtypeStruct((t, vw), F32),
        scratch_shapes=[pltpu.VMEM((heads, dk, dv), F32)],
        compiler_params=_params(("parallel", "arbitrary")), name="retention_scan",
    )(p, p, p, cosf, sinf, decay_p)


def _headnorm_kernel(hf_ref, hb_ref, gate_ref, gain_ref, y_ref, *, heads, dv, center, silu_gate):
    g = gate_ref[...].astype(F32)
    sg = _sigmoid(g)
    gate = g * sg if silu_gate else sg
    gain = gain_ref[...]
    for h in range(heads):
        sl = slice(h * dv, (h + 1) * dv)
        x = hf_ref[:, sl] + hb_ref[:, sl]
        if center:
            x = x - jnp.mean(x, axis=-1, keepdims=True)
        y = x * lax.rsqrt(jnp.mean(x * x, axis=-1, keepdims=True) + EPS)
        y_ref[:, sl] = (gate[:, sl] * (y * gain[:, sl])).astype(y_ref.dtype)


def _headnorm(hf, hb, p, gate_col, gain, heads, center, silu_gate):
    t, w = hf.shape
    tm = ROW_TILE
    kern = functools.partial(_headnorm_kernel, heads=heads, dv=w // heads, center=center, silu_gate=silu_gate)
    row_spec = pl.BlockSpec((tm, w), lambda i: (i, 0))
    return pl.pallas_call(
        kern, grid=(t // tm,),
        in_specs=[row_spec, row_spec, pl.BlockSpec((tm, w), lambda i: (i, gate_col // w)),
                  pl.BlockSpec((1, w), lambda i: (0, 0))],
        out_specs=row_spec,
        out_shape=jax.ShapeDtypeStruct((t, w), BF16),
        compiler_params=_params(("parallel",)), name="headnorm_gate",
    )(hf, hb, p, gain.reshape(1, w))


def _lane_low(shape, hd):
    return lax.broadcasted_iota(jnp.int32, shape, 1) < hd


def _stack_heads(q, hd):
    low = _lane_low(q.shape, hd)
    zero = jnp.zeros_like(q)
    return jnp.concatenate([jnp.where(low, q, zero), jnp.where(low, zero, q)], axis=0)


def _unstack_heads(o, hd):
    rows = o.shape[0] // 2
    return jnp.where(_lane_low((rows, LANES), hd), o[:rows], o[rows:])


def _window_kernel(sink_ref, q_ref, kp_ref, k0_ref, kn_ref, vp_ref, v0_ref, vn_ref, kc_ref, vc_ref,
                   cq_ref, sq_ref, cp_ref, sp_ref, cn_ref, sn_ref, o_ref, *, heads, kv_heads, hd, ctx_blocks,
                   n_latent, slabs_per_dot):
    i = pl.program_id(1)
    j = i - ctx_blocks
    w = q_ref.shape[0]
    half = hd // 2
    slabs_per_group = heads // kv_heads // 2

    def roped(x_ref, c_ref, s_ref):
        return _rope(x_ref[...].astype(F32), c_ref[...], s_ref[...], half)

    kb = jnp.concatenate([roped(kp_ref, cp_ref, sp_ref), roped(k0_ref, cq_ref, sq_ref),
                          roped(kn_ref, cn_ref, sn_ref)], axis=0)
    vb = jnp.concatenate([vp_ref[...], v0_ref[...], vn_ref[...]], axis=0).astype(F32)
    kc = kc_ref[...].astype(F32)
    vc = vc_ref[...].astype(F32)

    def group_copy(x, g):
        keep = _lane_low(x.shape, hd) if g == 0 else jnp.logical_not(_lane_low(x.shape, hd))
        return jnp.where(keep, x, pltpu.roll(x, hd, axis=1)).astype(BF16)

    c = lax.broadcasted_iota(jnp.int32, (3 * w, w), 0)
    t = lax.broadcasted_iota(jnp.int32, (3 * w, w), 1)
    lower = jnp.maximum(t + (w - WINDOW), (1 - j) * w)
    upper = jnp.minimum(t + (w + WINDOW), n_latent - 1 - (j - 1) * w)
    upper = jnp.where(j < 0, -1, upper)
    mask_bias = jnp.where((c >= lower) & (c <= upper), 0.0, NEG)
    mask_bias = jnp.concatenate([mask_bias] * (2 * slabs_per_dot), axis=1)
    scale = hd ** -0.5
    cq = cq_ref[...]
    sq = sq_ref[...]
    for g in range(kv_heads):
        kk, vv, kkc, vvc = group_copy(kb, g), group_copy(vb, g), group_copy(kc, g), group_copy(vc, g)
        for s0 in range(0, slabs_per_group, slabs_per_dot):
            slabs = [g * slabs_per_group + s0 + s for s in range(slabs_per_dot)]
            qs = []
            sinks = []
            for slab in slabs:
                qr = _rope(q_ref[:, slab * LANES:(slab + 1) * LANES].astype(F32), cq, sq, half)
                qs.append(_stack_heads((qr * scale).astype(BF16), hd))
                sinks += [jnp.full((1, w), sink_ref[2 * slab], F32), jnp.full((1, w), sink_ref[2 * slab + 1], F32)]
            qq = jnp.concatenate(qs, axis=0)
            sink = jnp.concatenate(sinks, axis=1)
            s_loc = _dot_nt(kk, qq) + mask_bias
            s_ctx = _dot_nt(kkc, qq)
            m = jnp.maximum(jnp.maximum(jnp.max(s_loc, axis=0, keepdims=True),
                                        jnp.max(s_ctx, axis=0, keepdims=True)), sink)
            e_loc = jnp.exp(s_loc - m)
            e_ctx = jnp.exp(s_ctx - m)
            den = (jnp.exp(sink - m) + jnp.sum(e_loc, axis=0, keepdims=True)
                   + jnp.sum(e_ctx, axis=0, keepdims=True))
            o_t = (_dot_tn(vv, e_loc.astype(BF16)) + _dot_tn(vvc, e_ctx.astype(BF16))) / den
            o = o_t.T
            for n, slab in enumerate(slabs):
                o_ref[:, slab * LANES:(slab + 1) * LANES] = _unstack_heads(
                    o[2 * n * w:2 * (n + 1) * w], hd).astype(o_ref.dtype)


def _window_attention(p, col0, cosf, sinf, sink, batch, seq_rows, ctx_rows):
    t = p.shape[0]
    w = WINDOW
    heads, kv_heads = C_HEADS, C_KV_HEADS
    hd = LANES // 2
    assert kv_heads * hd == LANES
    qw = heads * hd
    nblk = seq_rows // w
    ctx_blocks = ctx_rows // w
    kcol = (col0 + qw) // LANES
    vcol = kcol + 1

    def cur(b, i):
        return b * nblk + i

    def prev(b, i):
        return b * nblk + jnp.maximum(i - 1, 0)

    def nxt(b, i):
        return b * nblk + jnp.minimum(i + 1, nblk - 1)

    def kv_spec(fn, colblk):
        return pl.BlockSpec((w, LANES), lambda b, i: (fn(b, i), colblk))

    def tab_spec(fn):
        return pl.BlockSpec((w, LANES), lambda b, i: (fn(0, i), 0))

    kern = functools.partial(_window_kernel, heads=heads, kv_heads=kv_heads, hd=hd, ctx_blocks=ctx_blocks,
                             n_latent=seq_rows - ctx_rows, slabs_per_dot=4)
    ctx_spec_k = pl.BlockSpec((ctx_rows, LANES), lambda b, i: (b * (seq_rows // ctx_rows), kcol))
    ctx_spec_v = pl.BlockSpec((ctx_rows, LANES), lambda b, i: (b * (seq_rows // ctx_rows), vcol))
    return pl.pallas_call(
        kern,
        grid=(batch, nblk),
        in_specs=[
            pl.BlockSpec(memory_space=pltpu.SMEM),
            pl.BlockSpec((w, qw), lambda b, i: (cur(b, i), col0 // qw)),
            kv_spec(prev, kcol), kv_spec(cur, kcol), kv_spec(nxt, kcol),
            kv_spec(prev, vcol), kv_spec(cur, vcol), kv_spec(nxt, vcol),
            ctx_spec_k, ctx_spec_v,
            tab_spec(cur), tab_spec(cur), tab_spec(prev), tab_spec(prev), tab_spec(nxt), tab_spec(nxt),
        ],
        out_specs=pl.BlockSpec((w, qw), lambda b, i: (cur(b, i), 0)),
        out_shape=jax.ShapeDtypeStruct((t, qw), BF16),
        compiler_params=_params(("parallel", "arbitrary")), name="window_attention",
    )(sink, p, p, p, p, p, p, p, p, p, cosf, sinf, cosf, sinf, cosf, sinf)


def _nbr_layout():
    masked = 2 * NA_ROWS - 1
    variants = [(0, [0] * NBR_ROWS), (NA_ROWS // 2, list(range(NBR_ROWS))),
                (NBR_UNION - NBR_ROWS, [NBR_UNION - NA_ROWS] * NBR_ROWS)]
    pairs, index = [], []
    for delta, rel_start in variants:
        per_kr = []
        for kr in range(NBR_UNION):
            codes = [kr - (delta + rr) + NA_ROWS - 1 if rel_start[rr] <= kr < rel_start[rr] + NA_ROWS else masked
                     for rr in range(NBR_ROWS)]
            row = []
            for a in range(NBR_ROWS // 2):
                pair = (codes[2 * a], codes[2 * a + 1])
                if pair not in pairs:
                    pairs.append(pair)
                row.append(pairs.index(pair))
            per_kr.append(row)
        index.append(per_kr)
    return pairs, index


def _nbr_kernel(q_ref, k_ref, v_ref, tab_ref, o_ref, bias_ref, *, hd, ctx_rows, grid_rows, index):
    i = pl.program_id(2)
    scale = hd ** -0.5
    tq = q_ref.shape[0]
    qq = _stack_heads(q_ref[...] * scale, hd)
    kc = k_ref[0:ctx_rows, :]
    vc = v_ref[0:ctx_rows, :]
    s_ctx = _dot_nt(kc, qq)
    m_ctx = jnp.max(s_ctx, axis=0, keepdims=True)

    @pl.when(i == 0)
    def _():
        for variant, per_kr in enumerate(index):
            for kr, row in enumerate(per_kr):
                for sub in range(2):
                    for a, u in enumerate(row):
                        bias_ref[variant, kr * GRID_W:(kr + 1) * GRID_W,
                                 sub * tq + a * LANES:sub * tq + (a + 1) * LANES] = tab_ref[sub, u]
        e = jnp.exp(s_ctx - m_ctx)
        o_t = _dot_tn(vc, e.astype(BF16)) / jnp.sum(e, axis=0, keepdims=True)
        o_ref[...] = _unstack_heads(o_t.T, hd).astype(o_ref.dtype)

    @pl.when(i > 0)
    def _():
        blk = i - 1
        n_blk = grid_rows // NBR_ROWS
        u0 = jnp.clip(blk * NBR_ROWS - NA_ROWS // 2, 0, grid_rows - NBR_UNION)
        variant = jnp.where(blk == 0, 0, jnp.where(blk == n_blk - 1, 2, 1))
        base = pl.multiple_of(ctx_rows + u0 * GRID_W, GRID_W)
        kw = k_ref[pl.ds(base, NBR_UNION * GRID_W), :]
        vw = v_ref[pl.ds(base, NBR_UNION * GRID_W), :]
        s_loc = _dot_nt(kw, qq) + bias_ref[variant]
        m = jnp.maximum(jnp.max(s_loc, axis=0, keepdims=True), m_ctx)
        e_loc = jnp.exp(s_loc - m)
        e_ctx = jnp.exp(s_ctx - m)
        den = jnp.sum(e_loc, axis=0, keepdims=True) + jnp.sum(e_ctx, axis=0, keepdims=True)
        o_t = (_dot_tn(vw, e_loc.astype(BF16)) + _dot_tn(vc, e_ctx.astype(BF16))) / den
        o_ref[...] = _unstack_heads(o_t.T, hd).astype(o_ref.dtype)


def _nbr_table(na_bias_l, pairs):
    col = jnp.arange(GRID_W)
    col_start = jnp.clip(col - NA_COLS // 2, 0, GRID_W - NA_COLS)
    col_valid = (col[None, :] >= col_start[:, None]) & (col[None, :] < col_start[:, None] + NA_COLS)
    dc = jnp.clip(col[None, :] - col[:, None], -(NA_COLS - 1), NA_COLS - 1) + (NA_COLS - 1)
    blocks = jnp.where(col_valid.T[None, None], na_bias_l[:, :, dc.T], NEG)
    blocks = jnp.concatenate([blocks, jnp.full_like(blocks[:, :1], NEG)], axis=1)
    left = blocks[:, np.array([p[0] for p in pairs])]
    right = blocks[:, np.array([p[1] for p in pairs])]
    return jnp.concatenate([left, right], axis=-1).astype(F32)


def _nbr_attention(p, col0, na_bias_l, batch, seq_rows, ctx_rows):
    t = p.shape[0]
    heads = D_HEADS
    hd = LANES // 2
    qw = heads * hd
    slabs = heads // 2
    tq = NBR_ROWS * GRID_W
    n_latent = seq_rows - ctx_rows
    grid_rows = n_latent // GRID_W
    assert tq == ctx_rows and grid_rows % NBR_ROWS == 0 and grid_rows >= NBR_UNION + 1
    steps = seq_rows // tq
    q0 = col0 // LANES
    pairs, index = _nbr_layout()
    table = _nbr_table(na_bias_l, pairs)
    kern = functools.partial(_nbr_kernel, hd=hd, ctx_rows=ctx_rows, grid_rows=grid_rows, index=index)
    return pl.pallas_call(
        kern,
        grid=(batch, slabs, steps),
        in_specs=[
            pl.BlockSpec((tq, LANES), lambda b, s, i: (b * steps + i, q0 + s)),
            pl.BlockSpec((seq_rows, LANES), lambda b, s, i: (b, q0 + slabs + s)),
            pl.BlockSpec((seq_rows, LANES), lambda b, s, i: (b, q0 + 2 * slabs + s)),
            pl.BlockSpec((2,) + table.shape[1:], lambda b, s, i: (s, 0, 0, 0)),
        ],
        out_specs=pl.BlockSpec((tq, LANES), lambda b, s, i: (b * steps + i, s)),
        out_shape=jax.ShapeDtypeStruct((t, qw), BF16),
        scratch_shapes=[pltpu.VMEM((len(index), NBR_UNION * GRID_W, 2 * tq), F32)],
        compiler_params=_params(("parallel", "parallel", "arbitrary")), name="nbr_attention",
    )(p, p, p, table)


def _merge_kernel(*refs):
    ys = refs[0:N_BRANCH]
    gs = refs[N_BRANCH:2 * N_BRANCH]
    ws = refs[2 * N_BRANCH:3 * N_BRANCH]
    o_ref = refs[3 * N_BRANCH]
    acc = None
    for y_ref, g_ref, w_ref in zip(ys, gs, ws):
        term = _sigmoid(g_ref[...].astype(F32)) * _dot(y_ref[...], w_ref[...])
        acc = term if acc is None else acc + term
    o_ref[...] = acc.astype(o_ref.dtype)


def _merge(ys, p, gate_col0, w_branch, row0, tm, tn):
    t, bw = ys[0].shape
    d = w_branch.shape[1]
    nt = d // tn
    y_spec = pl.BlockSpec((tm, bw), lambda i, j: (i, 0))
    g_specs = [pl.BlockSpec((tm, tn), functools.partial(lambda i, j, br: (i, (gate_col0 + br * d) // tn + j), br=br))
               for br in range(N_BRANCH)]
    w_specs = [pl.BlockSpec((bw, tn), functools.partial(lambda i, j, br: (row0 // bw + br, j), br=br))
               for br in range(N_BRANCH)]
    return pl.pallas_call(
        _merge_kernel,
        grid=(t // tm, nt),
        in_specs=[y_spec] * N_BRANCH + g_specs + w_specs,
        out_specs=pl.BlockSpec((tm, tn), lambda i, j: (i, j)),
        out_shape=jax.ShapeDtypeStruct((t, d), BF16),
        compiler_params=_params(("parallel", "arbitrary")), name="merge_branches",
    )(*ys, *([p] * N_BRANCH), *([w_branch] * N_BRANCH))


def _moe_kernel(t_ref, gates_ref, w1_ref, b1_ref, w2_ref, b2_ref, o_ref, acc_ref, *, ff, pairs):
    e = pl.program_id(1)
    gates = gates_ref[...]

    @pl.when(e == 0)
    def _():
        acc_ref[...] = _dot(gates.astype(BF16), b2_ref[...])

    tokens = t_ref[...]
    lane = lax.broadcasted_iota(jnp.int32, gates.shape, 1)
    acts = []
    for pr in range(pairs):
        hid = _dot(tokens, w1_ref[pr]) + b1_ref[pr]
        g_h = jnp.minimum(hid[:, :2 * ff], SWIGLU_LIMIT)
        u_h = jnp.clip(hid[:, 2 * ff:], -SWIGLU_LIMIT, SWIGLU_LIMIT)
        cols = []
        for s in range(2):
            expert = (e * pairs + pr) * 2 + s
            gsel = jnp.sum(jnp.where(lane == expert, gates, 0.0), axis=-1, keepdims=True)
            cols.append(jnp.broadcast_to(gsel, (gates.shape[0], ff)))
        act = g_h * _sigmoid(SWIGLU_ALPHA * g_h) * (u_h + 1.0) * jnp.concatenate(cols, axis=-1)
        acts.append(act.astype(BF16))
    acc_ref[...] += _dot(jnp.concatenate(acts, axis=-1), w2_ref[...])

    @pl.when(e == pl.num_programs(1) - 1)
    def _():
        o_ref[...] = acc_ref[...].astype(o_ref.dtype)


def _moe(h, gates, w1p, b1p, w2, b2p, pair0, tm, pairs):
    t, d = h.shape
    width = w1p.shape[2]
    ff = width // 4
    steps = b1p.shape[0] // pairs
    kern = functools.partial(_moe_kernel, ff=ff, pairs=pairs)
    return pl.pallas_call(
        kern,
        grid=(t // tm, steps),
        in_specs=[
            pl.BlockSpec((tm, d), lambda i, e: (i, 0)),
            pl.BlockSpec((tm, LANES), lambda i, e: (i, 0)),
            pl.BlockSpec((pairs, d, width), lambda i, e: (pair0 // pairs + e, 0, 0)),
            pl.BlockSpec((pairs, 1, width), lambda i, e: (e, 0, 0)),
            pl.BlockSpec((pairs * 2 * ff, d), lambda i, e: (pair0 // pairs + e, 0)),
            pl.BlockSpec(b2p.shape, lambda i, e: (0, 0)),
        ],
        out_specs=pl.BlockSpec((tm, d), lambda i, e: (i, 0)),
        out_shape=jax.ShapeDtypeStruct((t, d), BF16),
        scratch_shapes=[pltpu.VMEM((tm, d), F32)],
        compiler_params=_params(("parallel", "arbitrary")), name="moe_experts",
    )(h, gates, w1p, b1p, w2, b2p)


def _rope_tables(n_latent, ctx_rows, hd):
    pos = np.arange(n_latent)
    row = (pos // GRID_W).astype(np.float32)
    col = (pos % GRID_W).astype(np.float32)
    n_freq = hd // 4
    inv = jnp.asarray(ROPE_BASE, F32) ** (-jnp.arange(n_freq, dtype=F32) / n_freq)
    ang = jnp.concatenate([jnp.asarray(row)[:, None] * inv, jnp.asarray(col)[:, None] * inv], axis=-1)
    ang = jnp.concatenate([jnp.zeros((ctx_rows, hd // 2), F32), ang], axis=0)
    cos, sin = jnp.cos(ang), jnp.sin(ang)
    reps = LANES // hd
    cosf = jnp.tile(jnp.concatenate([cos, cos], axis=-1), (1, reps))
    sinf = jnp.tile(jnp.concatenate([-sin, sin], axis=-1), (1, reps))
    return cosf, sinf


def _pad_lanes(a, value=0.0):
    pad = LANES - a.shape[-1]
    return jnp.pad(a, [(0, 0)] * (a.ndim - 1) + [(0, pad)], constant_values=value)


def kernel(x, c, ctx, c_ctx, ada_down, ada_up, ada_bias, norm_gain, w_in, mlstm_gate_bias, mlstm_norm_gain,
           ret_decay, ret_norm_gain, sink, na_bias, w_branch, w_out, router_w, router_b, moe_w1, moe_b1,
           moe_w2, moe_b2, final_gain):
    batch, n_latent, d = x.shape
    ctx_rows = ctx.shape[1]
    depth = w_in.shape[0]
    seq_rows = ctx_rows + n_latent
    t = batch * seq_rows
    bw = d // N_BRANCH
    n_experts = router_w.shape[2]
    ff = moe_w2.shape[2]

    a_sz = (A_HEADS * 128, A_HEADS * 128, bw, bw)
    n_gate = 4 * A_HEADS
    b_sz = (B_HEADS * 128, B_HEADS * 128, bw, bw)
    c_sz = (bw, C_KV_HEADS * 64, C_KV_HEADS * 64)
    d_sz = (bw, bw, bw)
    sizes = a_sz + (n_gate,) + b_sz + c_sz + d_sz + (N_BRANCH * d,)
    offs = [int(o) for o in np.concatenate([[0], np.cumsum(sizes)])]
    a0, g0, b0, c0, d0, bg0, end = offs[0], offs[4], offs[5], offs[9], offs[12], offs[15], offs[16]
    col_gate = 0
    col_a = N_BRANCH * d
    col_b = col_a + sum(a_sz)
    col_d = col_b + sum(b_sz)
    col_c = col_d + sum(d_sz)

    w_cat_t, w_g_t = _win_prep(jnp.swapaxes(w_in, 1, 2), [(bg0, end), (a0, g0), (b0, c0), (d0, bg0), (c0, d0)],
                               (g0, b0))
    w1p = _moe_w1_prep(moe_w1)
    wb_bf, wo_bf, w2_bf = _cast_bf16(w_branch.reshape(depth * N_BRANCH * bw, d), w_out.reshape(depth * d, d),
                                     moe_w2.reshape(depth * n_experts * ff, d))
    b1p_all = jnp.concatenate([moe_b1[..., 0::2].reshape(depth, n_experts // 2, 1, 2 * ff),
                               moe_b1[..., 1::2].reshape(depth, n_experts // 2, 1, 2 * ff)], axis=-1)
    b2p_all = jnp.pad(moe_b2, ((0, 0), (0, LANES - n_experts), (0, 0))).astype(BF16)

    mod_all = _ada_modulation(jnp.pad(jnp.concatenate([c_ctx[None], c], axis=0), ((0, 8 - 1 - batch), (0, 0))),
                              ada_down, ada_up, ada_bias)
    cos_b, sin_b = _rope_tables(n_latent, ctx_rows, 128)
    cos_c, sin_c = _rope_tables(n_latent, ctx_rows, 64)
    decay_p = jnp.pad(ret_decay, ((0, 0), (0, 8 - ret_decay.shape[1]), (0, LANES - ret_decay.shape[2])))

    tm_proj = _row_tile(t, 768)
    tm_moe = _row_tile(t, 512)
    z = (ctx.reshape(batch * ctx_rows, d), x.reshape(batch * n_latent, d))
    delta = None
    for l in range(depth):
        mod = mod_all[l]
        if l == 0:
            h = _resid_norm(z, norm_gain[l, 0], mod, 0, 1, batch, seq_rows, ctx_rows)[0]
        else:
            z, h = _resid_norm(z, norm_gain[l, 0], mod, 0, 1, batch, seq_rows, ctx_rows, delta=delta,
                               gate_mod=mod_all[l - 1], gate_slot=5)
        p = _matmul(h, w_cat_t, l, BF16, tm=tm_proj, tn=1280, w_transposed=True)
        gates_c, gates_r = _gate_proj(h, w_g_t, l, tm_proj)
        gates_r = gates_r[:n_gate]
        bias_c = _pad_lanes(mlstm_gate_bias[l][None])
        bias_r = mlstm_gate_bias[l][:, None]

        hf = _mlstm(p, col_a, gates_c, gates_r, bias_c, bias_r, batch, seq_rows, False)
        hb = _mlstm(p, col_a, gates_c, gates_r, bias_c, bias_r, batch, seq_rows, True)
        y_a = _headnorm(hf, hb, p, col_a + 2 * A_HEADS * 128 + bw, mlstm_norm_gain[l], A_HEADS, False, False)
        of = _retention(p, col_b, cos_b, sin_b, decay_p[l], batch, seq_rows, False)
        ob = _retention(p, col_b, cos_b, sin_b, decay_p[l], batch, seq_rows, True)
        y_b = _headnorm(of, ob, p, col_b + 2 * B_HEADS * 128 + bw, ret_norm_gain[l], B_HEADS, True, True)
        y_c = _window_attention(p, col_c, cos_c, sin_c, sink[l], batch, seq_rows, ctx_rows)
        y_d = _nbr_attention(p, col_d, na_bias[l], batch, seq_rows, ctx_rows)

        acc = _merge([y_a, y_b, y_c, y_d], p, col_gate, wb_bf, l * N_BRANCH * bw, tm=tm_moe, tn=1024)
        delta1 = _matmul(acc, wo_bf.reshape(depth, d, d), l, BF16, tm=tm_proj, tn=1024)

        rw = _pad_lanes(router_w[l]).astype(BF16)
        rb = _pad_lanes(router_b[l][None], NEG)
        z, h2, gates = _resid_norm(z, norm_gain[l, 1], mod, 3, 4, batch, seq_rows, ctx_rows, delta=delta1,
                                   gate_mod=mod, gate_slot=2, router=(rw, rb))
        delta = _moe(h2, gates, w1p, b1p_all[l], w2_bf, b2p_all[l], l * (n_experts // 2), tm=tm_moe, pairs=2)

    out = _final_norm(z, delta, mod_all[depth - 1], 5, final_gain, batch, seq_rows, ctx_rows)
    return out.reshape(batch, n_latent, d)
```

```python
import functools

import jax
import jax.numpy as jnp
import numpy as np
from jax import lax
from jax.experimental import pallas as pl
from jax.experimental.pallas import tpu as pltpu

F32 = jnp.float32
BF16 = jnp.bfloat16

GRID_W = 64
EPS = 1e-6
NEG = -1e30
ROPE_BASE = 10000.0
N_BRANCH = 4
A_HEADS = 4
B_HEADS = 4
C_HEADS = 16
C_KV_HEADS = 2
D_HEADS = 16
WINDOW = 128
NA_ROWS = 8
NA_COLS = 16
TOP_K = 4
SWIGLU_LIMIT = 7.0
SWIGLU_ALPHA = 1.702

LANES = 128
VMEM_LIMIT = 56 * 1024 * 1024
ROW_TILE = 256
NBR_ROWS = 4
NBR_UNION = NBR_ROWS + NA_ROWS - 1
NBR_SLABS_PER_STEP = 2


def _row_tile(rows, target):
    best = ROW_TILE
    for tile in range(ROW_TILE, target + 1, ROW_TILE):
        if rows % tile == 0:
            best = tile
    assert rows % best == 0
    return best


def _params(sem, vmem=VMEM_LIMIT):
    return pltpu.CompilerParams(dimension_semantics=sem, vmem_limit_bytes=vmem)


def _dot(a, b):
    return jnp.dot(a, b, preferred_element_type=F32)


def _dot_nt(a, b):
    return lax.dot_general(a, b, (((1,), (1,)), ((), ())), preferred_element_type=F32)


def _dot_tn(a, b):
    return lax.dot_general(a, b, (((0,), (0,)), ((), ())), preferred_element_type=F32)


def _sigmoid(x):
    return 1.0 / (1.0 + jnp.exp(-x))


def _softplus(x):
    return jnp.maximum(x, 0.0) + jnp.log1p(jnp.exp(-jnp.abs(x)))


def _log_sigmoid(x):
    return -_softplus(-x)


def _win_prep_kernel(w_ref, g_ref, cat_ref, gt_ref):
    cat_ref[...] = w_ref[0].astype(BF16)

    @pl.when(pl.program_id(1) == 0)
    def _():
        g = g_ref[0].astype(BF16)
        gt_ref[...] = jnp.concatenate([g, jnp.zeros((LANES - g.shape[0], g.shape[1]), BF16)], axis=0)


def _win_prep(w_in_t, segments, gate_seg):
    depth, _, d = w_in_t.shape
    blk = ROW_TILE
    sub = 8
    assert all(a % sub == 0 and (b - a) % blk == 0 for a, b in segments)
    n_blk = sum((b - a) // blk for a, b in segments)
    ga, gb = gate_seg

    def src_row(j):
        s = jnp.int32(0)
        base = 0
        for a, b in segments:
            n = (b - a) // blk
            s = jnp.where((j >= base) & (j < base + n), a // sub + (j - base) * (blk // sub), s)
            base += n
        return s * sub

    return pl.pallas_call(
        _win_prep_kernel, grid=(depth, n_blk),
        in_specs=[pl.BlockSpec((pl.Element(1), pl.Element(blk), pl.Element(d)), lambda l, j: (l, src_row(j), 0)),
                  pl.BlockSpec((pl.Element(1), pl.Element(gb - ga), pl.Element(d)), lambda l, j: (l, ga, 0))],
        out_specs=[pl.BlockSpec((None, blk, d), lambda l, j: (l, j, 0)),
                   pl.BlockSpec((None, LANES, d), lambda l, j: (l, 0, 0))],
        out_shape=[jax.ShapeDtypeStruct((depth, n_blk * blk, d), BF16),
                   jax.ShapeDtypeStruct((depth, LANES, d), BF16)],
        compiler_params=_params(("parallel", "arbitrary")), name="win_prep",
    )(w_in_t, w_in_t)


def _moe_w1_prep_kernel(w_ref, perm_ref, o_ref, *, ff):
    parts = [_dot(w_ref[0, s].astype(BF16), perm_ref[...]) for s in range(2)]
    o_ref[0] = jnp.concatenate([parts[0][:, :ff], parts[1][:, :ff], parts[0][:, ff:], parts[1][:, ff:]],
                               axis=1).astype(BF16)


def _moe_w1_prep(moe_w1):
    depth, n_exp, d, two_ff = moe_w1.shape
    ff = two_ff // 2
    rows = 1024
    src = np.concatenate([np.arange(0, two_ff, 2), np.arange(1, two_ff, 2)])
    perm = jnp.asarray(np.arange(two_ff)[:, None] == src[None, :], BF16)
    kern = functools.partial(_moe_w1_prep_kernel, ff=ff)
    return pl.pallas_call(
        kern, grid=(depth * n_exp // 2, d // rows),
        in_specs=[pl.BlockSpec((1, 2, rows, two_ff), lambda e, i: (e, 0, i, 0)),
                  pl.BlockSpec((two_ff, two_ff), lambda e, i: (0, 0))],
        out_specs=pl.BlockSpec((1, rows, 2 * two_ff), lambda e, i: (e, i, 0)),
        out_shape=jax.ShapeDtypeStruct((depth * n_exp // 2, d, 2 * two_ff), BF16),
        compiler_params=_params(("parallel", "parallel")), name="moe_w1_prep",
    )(moe_w1.reshape(depth * n_exp // 2, 2, d, two_ff), perm)


def _cast_kernel(*refs):
    n = len(refs) // 2
    for src, dst in zip(refs[:n], refs[n:]):
        dst[...] = src[...].astype(dst.dtype)


def _cast_bf16(*arrays):
    rows, cols = arrays[0].shape
    tm = ROW_TILE
    spec = pl.BlockSpec((tm, cols), lambda i: (i, 0))
    return pl.pallas_call(
        _cast_kernel, grid=(rows // tm,),
        in_specs=[spec] * len(arrays), out_specs=[spec] * len(arrays),
        out_shape=[jax.ShapeDtypeStruct((rows, cols), BF16)] * len(arrays),
        compiler_params=_params(("parallel",)), name="cast_bf16",
    )(*arrays)


def _ada_kernel(cv_ref, down_ref, up_ref, bias_ref, out_ref):
    cv = cv_ref[...]
    a = cv * _sigmoid(cv)
    z = _dot(a.astype(BF16), down_ref[0].astype(BF16))
    out_ref[0] = _dot(z.astype(BF16), up_ref[0].astype(BF16)) + bias_ref[0]


def _ada_modulation(cvecs, ada_down, ada_up, ada_bias):
    depth, d, r = ada_down.shape
    rows = cvecs.shape[0]
    return pl.pallas_call(
        _ada_kernel,
        grid=(depth, 6),
        in_specs=[
            pl.BlockSpec((rows, d), lambda l, j: (0, 0)),
            pl.BlockSpec((1, d, r), lambda l, j: (l, 0, 0)),
            pl.BlockSpec((1, r, d), lambda l, j: (l, 0, j)),
            pl.BlockSpec((1, 1, d), lambda l, j: (l, 0, j)),
        ],
        out_specs=pl.BlockSpec((1, rows, d), lambda l, j: (l, 0, j)),
        out_shape=jax.ShapeDtypeStruct((depth, rows, 6 * d), F32),
        compiler_params=_params(("parallel", "arbitrary")),
        name="ada_modulation",
    )(cvecs, ada_down, ada_up, ada_bias.reshape(depth, 1, 6 * d))


def _rmsnorm(x, gain):
    return x * lax.rsqrt(jnp.mean(x * x, axis=-1, keepdims=True) + EPS) * gain


def _resid_norm_kernel(*refs, split_src, has_delta, has_router, tiles_per_seq, ctx_tiles):
    refs = list(refs)
    tile = pl.program_id(0)
    b = tile // tiles_per_seq
    j = tile - b * tiles_per_seq
    is_ctx = j < ctx_tiles
    row = jnp.where(is_ctx, 0, 1 + b)
    if split_src:
        c_ref = refs.pop(0)
        x_ref = refs.pop(0)
        z = jnp.where(is_ctx, c_ref[...], x_ref[...])
    else:
        z = refs.pop(0)[...]
    if has_delta:
        delta_ref = refs.pop(0)
        gate_ref = refs.pop(0)
    gain_ref = refs.pop(0)
    shift_ref = refs.pop(0)
    scale_ref = refs.pop(0)
    if has_router:
        rw_ref = refs.pop(0)
        rb_ref = refs.pop(0)
    if has_delta:
        znew_ref = refs.pop(0)
    h_ref = refs.pop(0)
    if has_router:
        gates_ref = refs.pop(0)

    if has_delta:
        z = z + gate_ref[pl.ds(row, 1), :] * delta_ref[...].astype(F32)
        znew_ref[...] = z
    y = _rmsnorm(z, gain_ref[...])
    h = y * (1.0 + scale_ref[pl.ds(row, 1), :]) + shift_ref[pl.ds(row, 1), :]
    hb = h.astype(BF16)
    h_ref[...] = hb
    if has_router:
        logits = _dot(hb, rw_ref[...]) + rb_ref[...]
        lane = lax.broadcasted_iota(jnp.int32, logits.shape, 1).astype(F32)
        gates = jnp.zeros_like(logits)
        den = jnp.zeros((logits.shape[0], 1), F32)
        top = None
        for _ in range(TOP_K):
            m = jnp.max(logits, axis=-1, keepdims=True)
            idx = jnp.min(jnp.where(logits == m, lane, float(LANES)), axis=-1, keepdims=True)
            sel = lane == idx
            if top is None:
                top = m
            e = jnp.exp(m - top)
            den = den + e
            gates = jnp.where(sel, e, gates)
            logits = jnp.where(sel, NEG * 2.0, logits)
        gates_ref[...] = gates / den


def _resid_norm(src, gain, mod, shift_slot, scale_slot, batch, seq_rows, ctx_rows, delta=None, gate_mod=None,
                gate_slot=None, router=None):
    split_src = isinstance(src, tuple)
    d = src[0].shape[1] if split_src else src.shape[1]
    t = batch * seq_rows
    tm = ROW_TILE
    tiles_per_seq = seq_rows // tm
    ctx_tiles = ctx_rows // tm
    lat_tiles = tiles_per_seq - ctx_tiles
    has_delta = delta is not None
    has_router = router is not None
    row_spec = pl.BlockSpec((tm, d), lambda i: (i, 0))
    vec_spec = pl.BlockSpec((1, d), lambda i: (0, 0))

    def mod_spec(slot):
        return pl.BlockSpec((mod.shape[0], d), lambda i: (0, slot))

    def ctx_map(i):
        b = i // tiles_per_seq
        return b * ctx_tiles + jnp.minimum(i - b * tiles_per_seq, ctx_tiles - 1), 0

    def lat_map(i):
        b = i // tiles_per_seq
        return b * lat_tiles + jnp.maximum(i - b * tiles_per_seq - ctx_tiles, 0), 0

    if split_src:
        operands, in_specs = list(src), [pl.BlockSpec((tm, d), ctx_map), pl.BlockSpec((tm, d), lat_map)]
    else:
        operands, in_specs = [src], [row_spec]
    if has_delta:
        operands += [delta, gate_mod]
        in_specs += [row_spec, mod_spec(gate_slot)]
    operands += [gain.reshape(1, d), mod, mod]
    in_specs += [vec_spec, mod_spec(shift_slot), mod_spec(scale_slot)]
    if has_router:
        rw, rb = router
        operands += [rw, rb]
        in_specs += [pl.BlockSpec(rw.shape, lambda i: (0, 0)), pl.BlockSpec(rb.shape, lambda i: (0, 0))]
    out_shape, out_specs = [], []
    if has_delta:
        out_shape.append(jax.ShapeDtypeStruct((t, d), F32))
        out_specs.append(row_spec)
    out_shape.append(jax.ShapeDtypeStruct((t, d), BF16))
    out_specs.append(row_spec)
    if has_router:
        out_shape.append(jax.ShapeDtypeStruct((t, LANES), F32))
        out_specs.append(pl.BlockSpec((tm, LANES), lambda i: (i, 0)))
    kern = functools.partial(_resid_norm_kernel, split_src=split_src, has_delta=has_delta, has_router=has_router,
                             tiles_per_seq=tiles_per_seq, ctx_tiles=ctx_tiles)
    return pl.pallas_call(
        kern, grid=(t // tm,), in_specs=in_specs, out_specs=out_specs, out_shape=out_shape,
        compiler_params=_params(("parallel",)), name="resid_norm",
    )(*operands)


def _final_norm_kernel(z_ref, delta_ref, gate_ref, gain_ref, out_ref):
    row = 1 + pl.program_id(0)
    z = z_ref[...] + gate_ref[pl.ds(row, 1), :] * delta_ref[...].astype(F32)
    out_ref[...] = _rmsnorm(z, gain_ref[...])


def _final_norm(z, delta, gate_mod, gate_slot, gain, batch, seq_rows, ctx_rows):
    t, d = z.shape
    tm = ROW_TILE
    lat_tiles = (seq_rows - ctx_rows) // tm
    tiles_per_seq = seq_rows // tm
    ctx_tiles = ctx_rows // tm
    row_spec = pl.BlockSpec((tm, d), lambda b, j: (b * tiles_per_seq + ctx_tiles + j, 0))
    return pl.pallas_call(
        _final_norm_kernel,
        grid=(batch, lat_tiles),
        in_specs=[row_spec, row_spec,
                  pl.BlockSpec((gate_mod.shape[0], d), lambda b, j: (0, gate_slot)),
                  pl.BlockSpec((1, d), lambda b, j: (0, 0))],
        out_specs=pl.BlockSpec((tm, d), lambda b, j: (b * lat_tiles + j, 0)),
        out_shape=jax.ShapeDtypeStruct((batch * lat_tiles * tm, d), F32),
        compiler_params=_params(("parallel", "parallel")), name="final_norm",
    )(z, delta, gate_mod, gain.reshape(1, d))


def _matmul_kernel(a_ref, w_ref, o_ref, *, w_transposed):
    dot = _dot_nt if w_transposed else _dot
    o_ref[...] = dot(a_ref[...], w_ref[...]).astype(o_ref.dtype)


def _matmul(a, w, layer, out_dtype, tm, tn, w_transposed=False):
    m, k = a.shape
    if w_transposed:
        n = w.shape[1]
        w_spec = pl.BlockSpec((None, tn, k), lambda i, j: (layer, j, 0))
    else:
        n = w.shape[2]
        w_spec = pl.BlockSpec((None, k, tn), lambda i, j: (layer, 0, j))
    return pl.pallas_call(
        functools.partial(_matmul_kernel, w_transposed=w_transposed),
        grid=(m // tm, n // tn),
        in_specs=[pl.BlockSpec((tm, k), lambda i, j: (i, 0)), w_spec],
        out_specs=pl.BlockSpec((tm, tn), lambda i, j: (i, j)),
        out_shape=jax.ShapeDtypeStruct((m, n), out_dtype),
        compiler_params=_params(("parallel", "arbitrary")), name="matmul",
    )(a, w)


def _gate_proj_kernel(a_ref, wt_ref, oc_ref, or_ref):
    a = a_ref[...]
    wt = wt_ref[...]
    oc_ref[...] = _dot_nt(a, wt)
    or_ref[...] = _dot_nt(wt, a)


def _gate_proj(a, wt, layer, tm):
    m, k = a.shape
    return pl.pallas_call(
        _gate_proj_kernel,
        grid=(m // tm,),
        in_specs=[pl.BlockSpec((tm, k), lambda i: (i, 0)),
                  pl.BlockSpec((None, LANES, k), lambda i: (layer, 0, 0))],
        out_specs=[pl.BlockSpec((tm, LANES), lambda i: (i, 0)), pl.BlockSpec((LANES, tm), lambda i: (0, i))],
        out_shape=[jax.ShapeDtypeStruct((m, LANES), F32), jax.ShapeDtypeStruct((LANES, m), F32)],
        compiler_params=_params(("parallel",)), name="gate_proj",
    )(a, wt)


def _tri_masks(length, reverse):
    row = lax.broadcasted_iota(jnp.int32, (length, length), 0)
    col = lax.broadcasted_iota(jnp.int32, (length, length), 1)
    if reverse:
        return col >= row, row >= col
    return col <= row, row <= col


def _mlstm_kernel(q_ref, k_ref, v_ref, gc_ref, gr_ref, bc_ref, br_ref, h_ref, ct_ref, n_ref, m_ref, *,
                  reverse, heads, dqk, dv):
    @pl.when(pl.program_id(1) == 0)
    def _():
        ct_ref[...] = jnp.zeros_like(ct_ref)
        n_ref[...] = jnp.zeros_like(n_ref)
        m_ref[...] = jnp.zeros_like(m_ref)

    length = q_ref.shape[0]
    valid, valid_t = _tri_masks(length, reverse)
    gc = gc_ref[...] + bc_ref[...]
    gr = gr_ref[...] + br_ref[...]
    d0 = 2 * heads if reverse else 0
    scale = dqk ** -0.5
    for h in range(heads):
        i_col = gc[:, d0 + h:d0 + h + 1]
        f_col = _log_sigmoid(gc[:, d0 + heads + h:d0 + heads + h + 1])
        i_row = gr[d0 + h:d0 + h + 1, :]
        f_row = _log_sigmoid(gr[d0 + heads + h:d0 + heads + h + 1, :])
        b_col = jnp.sum(jnp.where(valid, f_row, 0.0), axis=1, keepdims=True)
        b_row = jnp.sum(jnp.where(valid_t, f_col, 0.0), axis=0, keepdims=True)
        b_end = jnp.sum(f_row, axis=1, keepdims=True)
        m_prev = m_ref[h][:, :1]
        log_d = jnp.where(valid, b_col - b_row + i_row, NEG)
        m_inter = b_col + m_prev
        m_t = jnp.maximum(m_inter, jnp.max(log_d, axis=1, keepdims=True))
        w_intra = jnp.exp(log_d - m_t) * scale
        w_inter = jnp.exp(m_inter - m_t) * scale
        q = q_ref[:, h * dqk:(h + 1) * dqk]
        k = k_ref[:, h * dqk:(h + 1) * dqk]
        v = v_ref[:, h * dv:(h + 1) * dv]
        s = _dot_nt(q, k) * w_intra
        num = _dot(s.astype(BF16), v) + w_inter * _dot(q, ct_ref[h].astype(BF16))
        qn = jnp.sum(q.astype(F32) * n_ref[h], axis=1, keepdims=True)
        den = jnp.sum(s, axis=1, keepdims=True) + w_inter * qn
        h_ref[:, h * dv:(h + 1) * dv] = (num / jnp.maximum(jnp.abs(den), jnp.exp(-m_t))).astype(h_ref.dtype)
        log_w = b_end - b_col + i_col
        m_new = jnp.maximum(b_end + m_prev, jnp.max(log_w, axis=0, keepdims=True))
        w_end = jnp.exp(log_w - m_new)
        decay = jnp.exp(b_end + m_prev - m_new)
        wv = (w_end * v.astype(F32)).astype(BF16)
        ct_ref[h] = decay * ct_ref[h] + _dot_tn(k, wv)
        n_ref[h] = decay * n_ref[h] + jnp.sum(w_end * k.astype(F32), axis=0, keepdims=True)
        m_ref[h] = jnp.broadcast_to(m_new, (1, LANES))


def _chunk_index(i, n_chunks, reverse):
    if not reverse:
        return i
    return jnp.where(i == 0, 0, n_chunks - i)


def _mlstm(p, col0, dv, gates_c, gates_r, bias_c, bias_r, batch, seq_rows, reverse):
    t = p.shape[0]
    length = ROW_TILE
    n_chunks = seq_rows // length
    heads = A_HEADS
    dqk = dv // 2
    qw, vw = heads * dqk, heads * dv

    def rows(b, i):
        return b * n_chunks + _chunk_index(i, n_chunks, reverse)

    kern = functools.partial(_mlstm_kernel, reverse=reverse, heads=heads, dqk=dqk, dv=dv)
    return pl.pallas_call(
        kern,
        grid=(batch, n_chunks),
        in_specs=[
            pl.BlockSpec((length, qw), lambda b, i: (rows(b, i), col0 // qw)),
            pl.BlockSpec((length, qw), lambda b, i: (rows(b, i), col0 // qw + 1)),
            pl.BlockSpec((length, vw), lambda b, i: (rows(b, i), (col0 + 2 * qw) // vw)),
            pl.BlockSpec((length, LANES), lambda b, i: (rows(b, i), 0)),
            pl.BlockSpec((4 * heads, length), lambda b, i: (0, rows(b, i))),
            pl.BlockSpec((1, LANES), lambda b, i: (0, 0)),
            pl.BlockSpec((4 * heads, 1), lambda b, i: (0, 0)),
        ],
        out_specs=pl.BlockSpec((length, vw), lambda b, i: (rows(b, i), 0)),
        out_shape=jax.ShapeDtypeStruct((t, vw), BF16),
        scratch_shapes=[pltpu.VMEM((heads, dqk, dv), F32), pltpu.VMEM((heads, 1, dqk), F32),
                        pltpu.VMEM((heads, 1, LANES), F32)],
        compiler_params=_params(("parallel", "arbitrary")), name="mlstm_scan",
    )(p, p, p, gates_c, gates_r, bias_c, bias_r)


def _rope(t, cosf, sinf, half):
    if 2 * half == LANES:
        partner = pltpu.roll(t, half, axis=1)
    else:
        lane = lax.broadcasted_iota(jnp.int32, t.shape, 1)
        partner = jnp.where((lane & (2 * half - 1)) < half, pltpu.roll(t, LANES - half, axis=1),
                            pltpu.roll(t, half, axis=1))
    return t * cosf + partner * sinf


def _retention_kernel(q_ref, k_ref, v_ref, cos_ref, sin_ref, dp_ref, o_ref, s_ref, *, reverse, heads, dk, dv):
    @pl.when(pl.program_id(1) == 0)
    def _():
        s_ref[...] = jnp.zeros_like(s_ref)

    length = q_ref.shape[0]
    row = lax.broadcasted_iota(jnp.int32, (length, length), 0)
    col = lax.broadcasted_iota(jnp.int32, (length, length), 1)
    diff = (col - row if reverse else row - col).astype(F32)
    tpos = lax.broadcasted_iota(jnp.int32, (length, 1), 0).astype(F32)
    lg_all = -_softplus(dp_ref[...])
    cosf = cos_ref[...]
    sinf = sin_ref[...]
    scale = dk ** -0.5
    d = 1 if reverse else 0
    for h in range(heads):
        lg = lg_all[d:d + 1, h:h + 1]
        decay_mat = jnp.where(diff >= 0.0, jnp.exp(lg * jnp.maximum(diff, 0.0)), 0.0) * scale
        if reverse:
            read_w = jnp.exp(lg * (length - tpos))
            write_w = jnp.exp(lg * tpos)
        else:
            read_w = jnp.exp(lg * (tpos + 1.0))
            write_w = jnp.exp(lg * (length - 1.0 - tpos))
        chunk_decay = jnp.exp(lg * float(length))
        q = _rope(q_ref[:, h * dk:(h + 1) * dk].astype(F32), cosf, sinf, dk // 2).astype(BF16)
        k = _rope(k_ref[:, h * dk:(h + 1) * dk].astype(F32), cosf, sinf, dk // 2).astype(BF16)
        v = v_ref[:, h * dv:(h + 1) * dv]
        s = _dot_nt(q, k) * decay_mat
        o_ref[:, h * dv:(h + 1) * dv] = (_dot(s.astype(BF16), v)
                                         + read_w * _dot(q, s_ref[h].astype(BF16))).astype(o_ref.dtype)
        wv = (write_w * v.astype(F32)).astype(BF16)
        s_ref[h] = chunk_decay * s_ref[h] + scale * _dot_tn(k, wv)


def _retention(p, col0, dv, cosf, sinf, decay_p, batch, seq_rows, reverse):
    t = p.shape[0]
    length = ROW_TILE
    n_chunks = seq_rows // length
    heads = B_HEADS
    dk = dv // 2
    assert dk == LANES
    qw, vw = heads * dk, heads * dv

    def rows(b, i):
        return b * n_chunks + _chunk_index(i, n_chunks, reverse)

    kern = functools.partial(_retention_kernel, reverse=reverse, heads=heads, dk=dk, dv=dv)
    return pl.pallas_call(
        kern,
        grid=(batch, n_chunks),
        in_specs=[
            pl.BlockSpec((length, qw), lambda b, i: (rows(b, i), col0 // qw)),
            pl.BlockSpec((length, qw), lambda b, i: (rows(b, i), col0 // qw + 1)),
            pl.BlockSpec((length, vw), lambda b, i: (rows(b, i), (col0 + 2 * qw) // vw)),
            pl.BlockSpec((length, LANES), lambda b, i: (_chunk_index(i, n_chunks, reverse), 0)),
            pl.BlockSpec((length, LANES), lambda b, i: (_chunk_index(i, n_chunks, reverse), 0)),
            pl.BlockSpec(decay_p.shape, lambda b, i: (0, 0)),
        ],
        out_specs=pl.BlockSpec((length, vw), lambda b, i: (rows(b, i), 0)),
        out_shape=jax.ShapeDtypeStruct((t, vw), BF16),
        scratch_shapes=[pltpu.VMEM((heads, dk, dv), F32)],
        compiler_params=_params(("parallel", "arbitrary")), name="retention_scan",
    )(p, p, p, cosf, sinf, decay_p)


def _headnorm_kernel(hf_ref, hb_ref, gate_ref, gain_ref, y_ref, *, heads, dv, center, silu_gate):
    g = gate_ref[...].astype(F32)
    sg = _sigmoid(g)
    gate = g * sg if silu_gate else sg
    gain = gain_ref[...]
    for h in range(heads):
        sl = slice(h * dv, (h + 1) * dv)
        x = hf_ref[:, sl].astype(F32) + hb_ref[:, sl].astype(F32)
        if center:
            x = x - jnp.mean(x, axis=-1, keepdims=True)
        y = x * lax.rsqrt(jnp.mean(x * x, axis=-1, keepdims=True) + EPS)
        y_ref[:, sl] = (gate[:, sl] * (y * gain[:, sl])).astype(y_ref.dtype)


def _headnorm(hf, hb, p, gate_col, gain, heads, center, silu_gate):
    t, w = hf.shape
    tm = _row_tile(t, 2 * ROW_TILE)
    kern = functools.partial(_headnorm_kernel, heads=heads, dv=w // heads, center=center, silu_gate=silu_gate)
    row_spec = pl.BlockSpec((tm, w), lambda i: (i, 0))
    return pl.pallas_call(
        kern, grid=(t // tm,),
        in_specs=[row_spec, row_spec, pl.BlockSpec((tm, w), lambda i: (i, gate_col // w)),
                  pl.BlockSpec((1, w), lambda i: (0, 0))],
        out_specs=row_spec,
        out_shape=jax.ShapeDtypeStruct((t, w), BF16),
        compiler_params=_params(("parallel",)), name="headnorm_gate",
    )(hf, hb, p, gain.reshape(1, w))


def _lane_low(shape, hd):
    return lax.broadcasted_iota(jnp.int32, shape, 1) < hd


def _stack_heads(q, hd):
    low = _lane_low(q.shape, hd)
    zero = jnp.zeros_like(q)
    return jnp.concatenate([jnp.where(low, q, zero), jnp.where(low, zero, q)], axis=0)


def _unstack_heads(o, hd):
    rows = o.shape[0] // 2
    return jnp.where(_lane_low((rows, LANES), hd), o[:rows], o[rows:])


def _window_kernel(sink_ref, q_ref, kp_ref, k0_ref, kn_ref, vp_ref, v0_ref, vn_ref, kc_ref, vc_ref,
                   cq_ref, sq_ref, cp_ref, sp_ref, cn_ref, sn_ref, o_ref, *, heads, kv_heads, hd, ctx_blocks,
                   n_latent, slabs_per_dot):
    i = pl.program_id(1)
    j = i - ctx_blocks
    w = q_ref.shape[0]
    half = hd // 2
    slabs_per_group = heads // kv_heads // 2

    def roped(x_ref, c_ref, s_ref):
        return _rope(x_ref[...].astype(F32), c_ref[...], s_ref[...], half)

    kb = jnp.concatenate([roped(kp_ref, cp_ref, sp_ref), roped(k0_ref, cq_ref, sq_ref),
                          roped(kn_ref, cn_ref, sn_ref)], axis=0)
    vb = jnp.concatenate([vp_ref[...], v0_ref[...], vn_ref[...]], axis=0).astype(F32)
    kc = kc_ref[...].astype(F32)
    vc = vc_ref[...].astype(F32)

    def group_copy(x, g):
        keep = _lane_low(x.shape, hd) if g == 0 else jnp.logical_not(_lane_low(x.shape, hd))
        return jnp.where(keep, x, pltpu.roll(x, hd, axis=1)).astype(BF16)

    c = lax.broadcasted_iota(jnp.int32, (3 * w, w), 0)
    t = lax.broadcasted_iota(jnp.int32, (3 * w, w), 1)
    lower = jnp.maximum(t + (w - WINDOW), (1 - j) * w)
    upper = jnp.minimum(t + (w + WINDOW), n_latent - 1 - (j - 1) * w)
    upper = jnp.where(j < 0, -1, upper)
    mask_bias = jnp.where((c >= lower) & (c <= upper), 0.0, NEG)
    mask_bias = jnp.concatenate([mask_bias] * (2 * slabs_per_dot), axis=1)
    scale = hd ** -0.5
    cq = cq_ref[...]
    sq = sq_ref[...]
    for g in range(kv_heads):
        kk, vv, kkc, vvc = group_copy(kb, g), group_copy(vb, g), group_copy(kc, g), group_copy(vc, g)
        for s0 in range(0, slabs_per_group, slabs_per_dot):
            slabs = [g * slabs_per_group + s0 + s for s in range(slabs_per_dot)]
            qs = []
            sinks = []
            for slab in slabs:
                qr = _rope(q_ref[:, slab * LANES:(slab + 1) * LANES].astype(F32), cq, sq, half)
                qs.append(_stack_heads((qr * scale).astype(BF16), hd))
                sinks += [jnp.full((1, w), sink_ref[2 * slab], F32), jnp.full((1, w), sink_ref[2 * slab + 1], F32)]
            qq = jnp.concatenate(qs, axis=0)
            sink = jnp.concatenate(sinks, axis=1)
            s_loc = _dot_nt(kk, qq) + mask_bias
            s_ctx = _dot_nt(kkc, qq)
            m = jnp.maximum(jnp.maximum(jnp.max(s_loc, axis=0, keepdims=True),
                                        jnp.max(s_ctx, axis=0, keepdims=True)), sink)
            e_loc = jnp.exp(s_loc - m)
            e_ctx = jnp.exp(s_ctx - m)
            den = (jnp.exp(sink - m) + jnp.sum(e_loc, axis=0, keepdims=True)
                   + jnp.sum(e_ctx, axis=0, keepdims=True))
            o_t = (_dot_tn(vv, e_loc.astype(BF16)) + _dot_tn(vvc, e_ctx.astype(BF16))) / den
            o = o_t.T
            for n, slab in enumerate(slabs):
                o_ref[:, slab * LANES:(slab + 1) * LANES] = _unstack_heads(
                    o[2 * n * w:2 * (n + 1) * w], hd).astype(o_ref.dtype)


def _window_attention(p, col0, cosf, sinf, sink, batch, seq_rows, ctx_rows):
    t = p.shape[0]
    w = WINDOW
    heads, kv_heads = C_HEADS, C_KV_HEADS
    hd = LANES // 2
    assert kv_heads * hd == LANES
    qw = heads * hd
    nblk = seq_rows // w
    ctx_blocks = ctx_rows // w
    kcol = (col0 + qw) // LANES
    vcol = kcol + 1

    def cur(b, i):
        return b * nblk + i

    def prev(b, i):
        return b * nblk + jnp.maximum(i - 1, 0)

    def nxt(b, i):
        return b * nblk + jnp.minimum(i + 1, nblk - 1)

    def kv_spec(fn, colblk):
        return pl.BlockSpec((w, LANES), lambda b, i: (fn(b, i), colblk))

    def tab_spec(fn):
        return pl.BlockSpec((w, LANES), lambda b, i: (fn(0, i), 0))

    kern = functools.partial(_window_kernel, heads=heads, kv_heads=kv_heads, hd=hd, ctx_blocks=ctx_blocks,
                             n_latent=seq_rows - ctx_rows, slabs_per_dot=4)
    ctx_spec_k = pl.BlockSpec((ctx_rows, LANES), lambda b, i: (b * (seq_rows // ctx_rows), kcol))
    ctx_spec_v = pl.BlockSpec((ctx_rows, LANES), lambda b, i: (b * (seq_rows // ctx_rows), vcol))
    return pl.pallas_call(
        kern,
        grid=(batch, nblk),
        in_specs=[
            pl.BlockSpec(memory_space=pltpu.SMEM),
            pl.BlockSpec((w, qw), lambda b, i: (cur(b, i), col0 // qw)),
            kv_spec(prev, kcol), kv_spec(cur, kcol), kv_spec(nxt, kcol),
            kv_spec(prev, vcol), kv_spec(cur, vcol), kv_spec(nxt, vcol),
            ctx_spec_k, ctx_spec_v,
            tab_spec(cur), tab_spec(cur), tab_spec(prev), tab_spec(prev), tab_spec(nxt), tab_spec(nxt),
        ],
        out_specs=pl.BlockSpec((w, qw), lambda b, i: (cur(b, i), 0)),
        out_shape=jax.ShapeDtypeStruct((t, qw), BF16),
        compiler_params=_params(("parallel", "arbitrary")), name="window_attention",
    )(sink, p, p, p, p, p, p, p, p, p, cosf, sinf, cosf, sinf, cosf, sinf)


def _nbr_layout():
    masked = 2 * NA_ROWS - 1
    variants = [(0, [0] * NBR_ROWS), (NA_ROWS // 2, list(range(NBR_ROWS))),
                (NBR_UNION - NBR_ROWS, [NBR_UNION - NA_ROWS] * NBR_ROWS)]
    pairs, index = [], []
    for delta, rel_start in variants:
        per_kr = []
        for kr in range(NBR_UNION):
            codes = [kr - (delta + rr) + NA_ROWS - 1 if rel_start[rr] <= kr < rel_start[rr] + NA_ROWS else masked
                     for rr in range(NBR_ROWS)]
            row = []
            for a in range(NBR_ROWS // 2):
                pair = (codes[2 * a], codes[2 * a + 1])
                if pair not in pairs:
                    pairs.append(pair)
                row.append(pairs.index(pair))
            per_kr.append(row)
        index.append(per_kr)
    return pairs, index


def _nbr_kernel(q_ref, k_ref, v_ref, tab_ref, o_ref, bias_ref, *, hd, ctx_rows, grid_rows, index, slabs_per_step):
    i = pl.program_id(2)
    scale = hd ** -0.5
    tq = q_ref.shape[0]
    lanes = [slice(sl * LANES, (sl + 1) * LANES) for sl in range(slabs_per_step)]
    qqs = [_stack_heads(q_ref[:, ln] * scale, hd) for ln in lanes]
    kcs = [k_ref[0:ctx_rows, ln] for ln in lanes]
    vcs = [v_ref[0:ctx_rows, ln] for ln in lanes]
    s_ctxs = [_dot_nt(kc, qq) for kc, qq in zip(kcs, qqs)]
    m_ctxs = [jnp.max(s, axis=0, keepdims=True) for s in s_ctxs]

    @pl.when(i == 0)
    def _():
        for sl, ln in enumerate(lanes):
            for variant, per_kr in enumerate(index):
                for kr, row in enumerate(per_kr):
                    for sub in range(2):
                        for a, u in enumerate(row):
                            bias_ref[sl, variant, kr * GRID_W:(kr + 1) * GRID_W,
                                     sub * tq + a * LANES:sub * tq + (a + 1) * LANES] = tab_ref[2 * sl + sub, u]
            e = jnp.exp(s_ctxs[sl] - m_ctxs[sl])
            o_t = _dot_tn(vcs[sl], e.astype(BF16)) / jnp.sum(e, axis=0, keepdims=True)
            o_ref[:, ln] = _unstack_heads(o_t.T, hd).astype(o_ref.dtype)

    @pl.when(i > 0)
    def _():
        blk = i - 1
        n_blk = grid_rows // NBR_ROWS
        u0 = jnp.clip(blk * NBR_ROWS - NA_ROWS // 2, 0, grid_rows - NBR_UNION)
        variant = jnp.where(blk == 0, 0, jnp.where(blk == n_blk - 1, 2, 1))
        base = pl.multiple_of(ctx_rows + u0 * GRID_W, GRID_W)
        for sl, ln in enumerate(lanes):
            kw = k_ref[pl.ds(base, NBR_UNION * GRID_W), ln]
            vw = v_ref[pl.ds(base, NBR_UNION * GRID_W), ln]
            s_loc = _dot_nt(kw, qqs[sl]) + bias_ref[sl, variant]
            m = jnp.maximum(jnp.max(s_loc, axis=0, keepdims=True), m_ctxs[sl])
            e_loc = jnp.exp(s_loc - m)
            e_ctx = jnp.exp(s_ctxs[sl] - m)
            den = jnp.sum(e_loc, axis=0, keepdims=True) + jnp.sum(e_ctx, axis=0, keepdims=True)
            o_t = (_dot_tn(vw, e_loc.astype(BF16)) + _dot_tn(vcs[sl], e_ctx.astype(BF16))) / den
            o_ref[:, ln] = _unstack_heads(o_t.T, hd).astype(o_ref.dtype)


def _nbr_table(na_bias_l, pairs):
    col = jnp.arange(GRID_W)
    col_start = jnp.clip(col - NA_COLS // 2, 0, GRID_W - NA_COLS)
    col_valid = (col[None, :] >= col_start[:, None]) & (col[None, :] < col_start[:, None] + NA_COLS)
    dc = jnp.clip(col[None, :] - col[:, None], -(NA_COLS - 1), NA_COLS - 1) + (NA_COLS - 1)
    blocks = jnp.where(col_valid.T[None, None], na_bias_l[:, :, dc.T], NEG)
    blocks = jnp.concatenate([blocks, jnp.full_like(blocks[:, :1], NEG)], axis=1)
    left = blocks[:, np.array([p[0] for p in pairs])]
    right = blocks[:, np.array([p[1] for p in pairs])]
    return jnp.concatenate([left, right], axis=-1).astype(F32)


def _nbr_attention(p, col0, na_bias_l, batch, seq_rows, ctx_rows):
    t = p.shape[0]
    heads = D_HEADS
    hd = LANES // 2
    qw = heads * hd
    slabs = heads // 2
    tq = NBR_ROWS * GRID_W
    n_latent = seq_rows - ctx_rows
    grid_rows = n_latent // GRID_W
    assert tq == ctx_rows and grid_rows % NBR_ROWS == 0 and grid_rows >= NBR_UNION + 1
    assert slabs % NBR_SLABS_PER_STEP == 0 and (col0 // LANES) % NBR_SLABS_PER_STEP == 0
    steps = seq_rows // tq
    q0 = col0 // LANES
    pairs, index = _nbr_layout()
    table = _nbr_table(na_bias_l, pairs)
    sps = NBR_SLABS_PER_STEP
    width = sps * LANES
    kern = functools.partial(_nbr_kernel, hd=hd, ctx_rows=ctx_rows, grid_rows=grid_rows, index=index,
                             slabs_per_step=sps)
    return pl.pallas_call(
        kern,
        grid=(batch, slabs // sps, steps),
        in_specs=[
            pl.BlockSpec((tq, width), lambda b, s, i: (b * steps + i, q0 // sps + s)),
            pl.BlockSpec((seq_rows, width), lambda b, s, i: (b, (q0 + slabs) // sps + s)),
            pl.BlockSpec((seq_rows, width), lambda b, s, i: (b, (q0 + 2 * slabs) // sps + s)),
            pl.BlockSpec((2 * sps,) + table.shape[1:], lambda b, s, i: (s, 0, 0, 0)),
        ],
        out_specs=pl.BlockSpec((tq, width), lambda b, s, i: (b * steps + i, s)),
        out_shape=jax.ShapeDtypeStruct((t, qw), BF16),
        scratch_shapes=[pltpu.VMEM((sps, len(index), NBR_UNION * GRID_W, 2 * tq), F32)],
        compiler_params=_params(("parallel", "parallel", "arbitrary")), name="nbr_attention",
    )(p, p, p, table)


def _merge_kernel(*refs):
    ys = refs[0:N_BRANCH]
    gs = refs[N_BRANCH:2 * N_BRANCH]
    ws = refs[2 * N_BRANCH:3 * N_BRANCH]
    o_ref = refs[3 * N_BRANCH]
    acc = None
    for y_ref, g_ref, w_ref in zip(ys, gs, ws):
        term = _sigmoid(g_ref[...].astype(F32)) * _dot(y_ref[...], w_ref[...])
        acc = term if acc is None else acc + term
    o_ref[...] = acc.astype(o_ref.dtype)


def _merge(ys, p, gate_col0, w_branch, row0, tm, tn):
    t, bw = ys[0].shape
    d = w_branch.shape[1]
    nt = d // tn
    y_spec = pl.BlockSpec((tm, bw), lambda i, j: (i, 0))
    g_specs = [pl.BlockSpec((tm, tn), functools.partial(lambda i, j, br: (i, (gate_col0 + br * d) // tn + j), br=br))
               for br in range(N_BRANCH)]
    w_specs = [pl.BlockSpec((bw, tn), functools.partial(lambda i, j, br: (row0 // bw + br, j), br=br))
               for br in range(N_BRANCH)]
    return pl.pallas_call(
        _merge_kernel,
        grid=(t // tm, nt),
        in_specs=[y_spec] * N_BRANCH + g_specs + w_specs,
        out_specs=pl.BlockSpec((tm, tn), lambda i, j: (i, j)),
        out_shape=jax.ShapeDtypeStruct((t, d), BF16),
        compiler_params=_params(("parallel", "arbitrary")), name="merge_branches",
    )(*ys, *([p] * N_BRANCH), *([w_branch] * N_BRANCH))


def _moe_kernel(t_ref, gates_ref, w1_ref, b1_ref, w2_ref, b2_ref, o_ref, acc_ref, *, ff, pairs):
    e = pl.program_id(1)
    gates = gates_ref[...]

    @pl.when(e == 0)
    def _():
        acc_ref[...] = _dot(gates.astype(BF16), b2_ref[...])

    tokens = t_ref[...]
    lane = lax.broadcasted_iota(jnp.int32, gates.shape, 1)
    acts = []
    for pr in range(pairs):
        hid = _dot(tokens, w1_ref[pr]) + b1_ref[pr]
        g_h = jnp.minimum(hid[:, :2 * ff], SWIGLU_LIMIT)
        u_h = jnp.clip(hid[:, 2 * ff:], -SWIGLU_LIMIT, SWIGLU_LIMIT)
        cols = []
        for s in range(2):
            expert = (e * pairs + pr) * 2 + s
            gsel = jnp.sum(jnp.where(lane == expert, gates, 0.0), axis=-1, keepdims=True)
            cols.append(jnp.broadcast_to(gsel, (gates.shape[0], ff)))
        act = g_h * _sigmoid(SWIGLU_ALPHA * g_h) * (u_h + 1.0) * jnp.concatenate(cols, axis=-1)
        acts.append(act.astype(BF16))
    acc_ref[...] += _dot(jnp.concatenate(acts, axis=-1), w2_ref[...])

    @pl.when(e == pl.num_programs(1) - 1)
    def _():
        o_ref[...] = acc_ref[...].astype(o_ref.dtype)


def _moe(h, gates, w1p, b1p, w2, b2p, pair0, tm, pairs):
    t, d = h.shape
    width = w1p.shape[2]
    ff = width // 4
    steps = b1p.shape[0] // pairs
    kern = functools.partial(_moe_kernel, ff=ff, pairs=pairs)
    return pl.pallas_call(
        kern,
        grid=(t // tm, steps),
        in_specs=[
            pl.BlockSpec((tm, d), lambda i, e: (i, 0)),
            pl.BlockSpec((tm, LANES), lambda i, e: (i, 0)),
            pl.BlockSpec((pairs, d, width), lambda i, e: (pair0 // pairs + e, 0, 0)),
            pl.BlockSpec((pairs, 1, width), lambda i, e: (e, 0, 0)),
            pl.BlockSpec((pairs * 2 * ff, d), lambda i, e: (pair0 // pairs + e, 0)),
            pl.BlockSpec(b2p.shape, lambda i, e: (0, 0)),
        ],
        out_specs=pl.BlockSpec((tm, d), lambda i, e: (i, 0)),
        out_shape=jax.ShapeDtypeStruct((t, d), BF16),
        scratch_shapes=[pltpu.VMEM((tm, d), F32)],
        compiler_params=_params(("parallel", "arbitrary")), name="moe_experts",
    )(h, gates, w1p, b1p, w2, b2p)


def _rope_tables(n_latent, ctx_rows, hd):
    pos = np.arange(n_latent)
    row = (pos // GRID_W).astype(np.float32)
    col = (pos % GRID_W).astype(np.float32)
    n_freq = hd // 4
    inv = jnp.asarray(ROPE_BASE, F32) ** (-jnp.arange(n_freq, dtype=F32) / n_freq)
    ang = jnp.concatenate([jnp.asarray(row)[:, None] * inv, jnp.asarray(col)[:, None] * inv], axis=-1)
    ang = jnp.concatenate([jnp.zeros((ctx_rows, hd // 2), F32), ang], axis=0)
    cos, sin = jnp.cos(ang), jnp.sin(ang)
    reps = LANES // hd
    cosf = jnp.tile(jnp.concatenate([cos, cos], axis=-1), (1, reps))
    sinf = jnp.tile(jnp.concatenate([-sin, sin], axis=-1), (1, reps))
    return cosf, sinf


def _pad_lanes(a, value=0.0):
    pad = LANES - a.shape[-1]
    return jnp.pad(a, [(0, 0)] * (a.ndim - 1) + [(0, pad)], constant_values=value)


def kernel(x, c, ctx, c_ctx, ada_down, ada_up, ada_bias, norm_gain, w_in, mlstm_gate_bias, mlstm_norm_gain,
           ret_decay, ret_norm_gain, sink, na_bias, w_branch, w_out, router_w, router_b, moe_w1, moe_b1,
           moe_w2, moe_b2, final_gain):
    batch, n_latent, d = x.shape
    ctx_rows = ctx.shape[1]
    depth = w_in.shape[0]
    seq_rows = ctx_rows + n_latent
    t = batch * seq_rows
    bw = d // N_BRANCH
    n_experts = router_w.shape[2]
    ff = moe_w2.shape[2]

    a_dv, b_dv = bw // A_HEADS, bw // B_HEADS
    c_hd, d_hd = bw // C_HEADS, bw // D_HEADS
    assert c_hd == LANES // 2 and d_hd == LANES // 2
    a_sz = (bw // 2, bw // 2, bw, bw)
    n_gate = 4 * A_HEADS
    b_sz = (bw // 2, bw // 2, bw, bw)
    c_sz = (bw, C_KV_HEADS * c_hd, C_KV_HEADS * c_hd)
    d_sz = (bw, bw, bw)
    sizes = a_sz + (n_gate,) + b_sz + c_sz + d_sz + (N_BRANCH * d,)
    offs = [int(o) for o in np.concatenate([[0], np.cumsum(sizes)])]
    a0, g0, b0, c0, d0, bg0, end = offs[0], offs[4], offs[5], offs[9], offs[12], offs[15], offs[16]
    col_gate = 0
    col_a = N_BRANCH * d
    col_b = col_a + sum(a_sz)
    col_d = col_b + sum(b_sz)
    col_c = col_d + sum(d_sz)

    w_cat_t, w_g_t = _win_prep(jnp.swapaxes(w_in, 1, 2), [(bg0, end), (a0, g0), (b0, c0), (d0, bg0), (c0, d0)],
                               (g0, b0))
    w1p = _moe_w1_prep(moe_w1)
    wb_bf, wo_bf, w2_bf = _cast_bf16(w_branch.reshape(depth * N_BRANCH * bw, d), w_out.reshape(depth * d, d),
                                     moe_w2.reshape(depth * n_experts * ff, d))
    b1p_all = jnp.concatenate([moe_b1[..., 0::2].reshape(depth, n_experts // 2, 1, 2 * ff),
                               moe_b1[..., 1::2].reshape(depth, n_experts // 2, 1, 2 * ff)], axis=-1)
    b2p_all = jnp.pad(moe_b2, ((0, 0), (0, LANES - n_experts), (0, 0))).astype(BF16)

    mod_all = _ada_modulation(jnp.pad(jnp.concatenate([c_ctx[None], c], axis=0), ((0, 8 - 1 - batch), (0, 0))),
                              ada_down, ada_up, ada_bias)
    cos_b, sin_b = _rope_tables(n_latent, ctx_rows, b_dv // 2)
    cos_c, sin_c = _rope_tables(n_latent, ctx_rows, c_hd)
    decay_p = jnp.pad(ret_decay, ((0, 0), (0, 8 - ret_decay.shape[1]), (0, LANES - ret_decay.shape[2])))

    tm_proj = _row_tile(t, 768)
    tm_moe = _row_tile(t, 512)
    z = (ctx.reshape(batch * ctx_rows, d), x.reshape(batch * n_latent, d))
    delta = None
    for l in range(depth):
        mod = mod_all[l]
        if l == 0:
            h = _resid_norm(z, norm_gain[l, 0], mod, 0, 1, batch, seq_rows, ctx_rows)[0]
        else:
            z, h = _resid_norm(z, norm_gain[l, 0], mod, 0, 1, batch, seq_rows, ctx_rows, delta=delta,
                               gate_mod=mod_all[l - 1], gate_slot=5)
        p = _matmul(h, w_cat_t, l, BF16, tm=tm_proj, tn=1280, w_transposed=True)
        gates_c, gates_r = _gate_proj(h, w_g_t, l, tm_proj)
        gates_r = gates_r[:n_gate]
        bias_c = _pad_lanes(mlstm_gate_bias[l][None])
        bias_r = mlstm_gate_bias[l][:, None]

        hf = _mlstm(p, col_a, a_dv, gates_c, gates_r, bias_c, bias_r, batch, seq_rows, False)
        hb = _mlstm(p, col_a, a_dv, gates_c, gates_r, bias_c, bias_r, batch, seq_rows, True)
        y_a = _headnorm(hf, hb, p, col_a + 2 * bw, mlstm_norm_gain[l], A_HEADS, False, False)
        of = _retention(p, col_b, b_dv, cos_b, sin_b, decay_p[l], batch, seq_rows, False)
        ob = _retention(p, col_b, b_dv, cos_b, sin_b, decay_p[l], batch, seq_rows, True)
        y_b = _headnorm(of, ob, p, col_b + 2 * bw, ret_norm_gain[l], B_HEADS, True, True)
        y_c = _window_attention(p, col_c, cos_c, sin_c, sink[l], batch, seq_rows, ctx_rows)
        y_d = _nbr_attention(p, col_d, na_bias[l], batch, seq_rows, ctx_rows)

        acc = _merge([y_a, y_b, y_c, y_d], p, col_gate, wb_bf, l * N_BRANCH * bw, tm=tm_moe, tn=1024)
        delta1 = _matmul(acc, wo_bf.reshape(depth, d, d), l, BF16, tm=tm_proj, tn=1024)

        rw = _pad_lanes(router_w[l]).astype(BF16)
        rb = _pad_lanes(router_b[l][None], NEG)
        z, h2, gates = _resid_norm(z, norm_gain[l, 1], mod, 3, 4, batch, seq_rows, ctx_rows, delta=delta1,
                                   gate_mod=mod, gate_slot=2, router=(rw, rb))
        delta = _moe(h2, gates, w1p, b1p_all[l], w2_bf, b2p_all[l], l * (n_experts // 2), tm=tm_moe, pairs=2)

    out = _final_norm(z, delta, mod_all[depth - 1], 5, final_gain, batch, seq_rows, ctx_rows)
    return out.reshape(batch, n_latent, d)
```

```python
import functools

import jax
import jax.numpy as jnp
import numpy as np
from jax import lax
from jax.experimental import pallas as pl
from jax.experimental.pallas import tpu as pltpu

F32 = jnp.float32
BF16 = jnp.bfloat16

GRID_W = 64
EPS = 1e-6
NEG = -1e30
ROPE_BASE = 10000.0
N_BRANCH = 4
A_HEADS = 4
B_HEADS = 4
C_HEADS = 16
C_KV_HEADS = 2
D_HEADS = 16
WINDOW = 128
NA_ROWS = 8
NA_COLS = 16
TOP_K = 4
SWIGLU_LIMIT = 7.0
SWIGLU_ALPHA = 1.702

LANES = 128
VMEM_LIMIT = 56 * 1024 * 1024
ROW_TILE = 256
NBR_ROWS = 4
NBR_UNION = NBR_ROWS + NA_ROWS - 1
NBR_SLABS_PER_STEP = 2


def _row_tile(rows, target):
    best = ROW_TILE
    for tile in range(ROW_TILE, target + 1, ROW_TILE):
        if rows % tile == 0:
            best = tile
    assert rows % best == 0
    return best


def _params(sem, vmem=VMEM_LIMIT):
    return pltpu.CompilerParams(dimension_semantics=sem, vmem_limit_bytes=vmem)


def _dot(a, b):
    return jnp.dot(a, b, preferred_element_type=F32)


def _dot_nt(a, b):
    return lax.dot_general(a, b, (((1,), (1,)), ((), ())), preferred_element_type=F32)


def _dot_tn(a, b):
    return lax.dot_general(a, b, (((0,), (0,)), ((), ())), preferred_element_type=F32)


def _sigmoid(x):
    return 1.0 / (1.0 + jnp.exp(-x))


def _softplus(x):
    return jnp.maximum(x, 0.0) + jnp.log1p(jnp.exp(-jnp.abs(x)))


def _log_sigmoid(x):
    return -_softplus(-x)


def _win_prep_kernel(w_ref, g_ref, cat_ref, gt_ref):
    cat_ref[...] = w_ref[0].astype(BF16)

    @pl.when(pl.program_id(1) == 0)
    def _():
        g = g_ref[0].astype(BF16)
        gt_ref[...] = jnp.concatenate([g, jnp.zeros((LANES - g.shape[0], g.shape[1]), BF16)], axis=0)


def _win_prep(w_in_t, segments, gate_seg):
    depth, _, d = w_in_t.shape
    blk = ROW_TILE
    sub = 8
    assert all(a % sub == 0 and (b - a) % blk == 0 for a, b in segments)
    n_blk = sum((b - a) // blk for a, b in segments)
    ga, gb = gate_seg

    def src_row(j):
        s = jnp.int32(0)
        base = 0
        for a, b in segments:
            n = (b - a) // blk
            s = jnp.where((j >= base) & (j < base + n), a // sub + (j - base) * (blk // sub), s)
            base += n
        return s * sub

    return pl.pallas_call(
        _win_prep_kernel, grid=(depth, n_blk),
        in_specs=[pl.BlockSpec((pl.Element(1), pl.Element(blk), pl.Element(d)), lambda l, j: (l, src_row(j), 0)),
                  pl.BlockSpec((pl.Element(1), pl.Element(gb - ga), pl.Element(d)), lambda l, j: (l, ga, 0))],
        out_specs=[pl.BlockSpec((None, blk, d), lambda l, j: (l, j, 0)),
                   pl.BlockSpec((None, LANES, d), lambda l, j: (l, 0, 0))],
        out_shape=[jax.ShapeDtypeStruct((depth, n_blk * blk, d), BF16),
                   jax.ShapeDtypeStruct((depth, LANES, d), BF16)],
        compiler_params=_params(("parallel", "arbitrary")), name="win_prep",
    )(w_in_t, w_in_t)


def _moe_w1_prep_kernel(w_ref, perm_ref, o_ref, *, ff):
    parts = [_dot(w_ref[0, s].astype(BF16), perm_ref[...]) for s in range(2)]
    o_ref[0] = jnp.concatenate([parts[0][:, :ff], parts[1][:, :ff], parts[0][:, ff:], parts[1][:, ff:]],
                               axis=1).astype(BF16)


def _moe_w1_prep(moe_w1):
    depth, n_exp, d, two_ff = moe_w1.shape
    ff = two_ff // 2
    rows = 1024
    src = np.concatenate([np.arange(0, two_ff, 2), np.arange(1, two_ff, 2)])
    perm = jnp.asarray(np.arange(two_ff)[:, None] == src[None, :], BF16)
    kern = functools.partial(_moe_w1_prep_kernel, ff=ff)
    return pl.pallas_call(
        kern, grid=(depth * n_exp // 2, d // rows),
        in_specs=[pl.BlockSpec((1, 2, rows, two_ff), lambda e, i: (e, 0, i, 0)),
                  pl.BlockSpec((two_ff, two_ff), lambda e, i: (0, 0))],
        out_specs=pl.BlockSpec((1, rows, 2 * two_ff), lambda e, i: (e, i, 0)),
        out_shape=jax.ShapeDtypeStruct((depth * n_exp // 2, d, 2 * two_ff), BF16),
        compiler_params=_params(("parallel", "parallel")), name="moe_w1_prep",
    )(moe_w1.reshape(depth * n_exp // 2, 2, d, two_ff), perm)


def _cast_kernel(*refs):
    n = len(refs) // 2
    for src, dst in zip(refs[:n], refs[n:]):
        dst[...] = src[...].astype(dst.dtype)


def _cast_bf16(*arrays):
    rows, cols = arrays[0].shape
    tm = ROW_TILE
    spec = pl.BlockSpec((tm, cols), lambda i: (i, 0))
    return pl.pallas_call(
        _cast_kernel, grid=(rows // tm,),
        in_specs=[spec] * len(arrays), out_specs=[spec] * len(arrays),
        out_shape=[jax.ShapeDtypeStruct((rows, cols), BF16)] * len(arrays),
        compiler_params=_params(("parallel",)), name="cast_bf16",
    )(*arrays)


def _ada_kernel(cv_ref, down_ref, up_ref, bias_ref, out_ref):
    cv = cv_ref[...]
    a = cv * _sigmoid(cv)
    z = _dot(a.astype(BF16), down_ref[0].astype(BF16))
    out_ref[0] = _dot(z.astype(BF16), up_ref[0].astype(BF16)) + bias_ref[0]


def _ada_modulation(cvecs, ada_down, ada_up, ada_bias):
    depth, d, r = ada_down.shape
    rows = cvecs.shape[0]
    return pl.pallas_call(
        _ada_kernel,
        grid=(depth, 6),
        in_specs=[
            pl.BlockSpec((rows, d), lambda l, j: (0, 0)),
            pl.BlockSpec((1, d, r), lambda l, j: (l, 0, 0)),
            pl.BlockSpec((1, r, d), lambda l, j: (l, 0, j)),
            pl.BlockSpec((1, 1, d), lambda l, j: (l, 0, j)),
        ],
        out_specs=pl.BlockSpec((1, rows, d), lambda l, j: (l, 0, j)),
        out_shape=jax.ShapeDtypeStruct((depth, rows, 6 * d), F32),
        compiler_params=_params(("parallel", "arbitrary")),
        name="ada_modulation",
    )(cvecs, ada_down, ada_up, ada_bias.reshape(depth, 1, 6 * d))


def _rmsnorm(x, gain):
    return x * lax.rsqrt(jnp.mean(x * x, axis=-1, keepdims=True) + EPS) * gain


def _resid_norm_kernel(*refs, split_src, has_delta, has_router, tiles_per_seq, ctx_tiles):
    refs = list(refs)
    tile = pl.program_id(0)
    b = tile // tiles_per_seq
    j = tile - b * tiles_per_seq
    is_ctx = j < ctx_tiles
    row = jnp.where(is_ctx, 0, 1 + b)
    if split_src:
        c_ref = refs.pop(0)
        x_ref = refs.pop(0)
        z = jnp.where(is_ctx, c_ref[...], x_ref[...])
    else:
        z = refs.pop(0)[...]
    if has_delta:
        delta_ref = refs.pop(0)
        gate_ref = refs.pop(0)
    gain_ref = refs.pop(0)
    shift_ref = refs.pop(0)
    scale_ref = refs.pop(0)
    if has_router:
        rw_ref = refs.pop(0)
        rb_ref = refs.pop(0)
    if has_delta:
        znew_ref = refs.pop(0)
    h_ref = refs.pop(0)
    if has_router:
        gates_ref = refs.pop(0)

    if has_delta:
        z = z + gate_ref[pl.ds(row, 1), :] * delta_ref[...].astype(F32)
        znew_ref[...] = z
    y = _rmsnorm(z, gain_ref[...])
    h = y * (1.0 + scale_ref[pl.ds(row, 1), :]) + shift_ref[pl.ds(row, 1), :]
    hb = h.astype(BF16)
    h_ref[...] = hb
    if has_router:
        logits = _dot(hb, rw_ref[...]) + rb_ref[...]
        lane = lax.broadcasted_iota(jnp.int32, logits.shape, 1).astype(F32)
        gates = jnp.zeros_like(logits)
        den = jnp.zeros((logits.shape[0], 1), F32)
        top = None
        for _ in range(TOP_K):
            m = jnp.max(logits, axis=-1, keepdims=True)
            idx = jnp.min(jnp.where(logits == m, lane, float(LANES)), axis=-1, keepdims=True)
            sel = lane == idx
            if top is None:
                top = m
            e = jnp.exp(m - top)
            den = den + e
            gates = jnp.where(sel, e, gates)
            logits = jnp.where(sel, NEG * 2.0, logits)
        gates_ref[...] = gates / den


def _resid_norm(src, gain, mod, shift_slot, scale_slot, batch, seq_rows, ctx_rows, delta=None, gate_mod=None,
                gate_slot=None, router=None):
    split_src = isinstance(src, tuple)
    d = src[0].shape[1] if split_src else src.shape[1]
    t = batch * seq_rows
    tm = ROW_TILE
    tiles_per_seq = seq_rows // tm
    ctx_tiles = ctx_rows // tm
    lat_tiles = tiles_per_seq - ctx_tiles
    has_delta = delta is not None
    has_router = router is not None
    row_spec = pl.BlockSpec((tm, d), lambda i: (i, 0))
    vec_spec = pl.BlockSpec((1, d), lambda i: (0, 0))

    def mod_spec(slot):
        return pl.BlockSpec((mod.shape[0], d), lambda i: (0, slot))

    def ctx_map(i):
        b = i // tiles_per_seq
        return b * ctx_tiles + jnp.minimum(i - b * tiles_per_seq, ctx_tiles - 1), 0

    def lat_map(i):
        b = i // tiles_per_seq
        return b * lat_tiles + jnp.maximum(i - b * tiles_per_seq - ctx_tiles, 0), 0

    if split_src:
        operands, in_specs = list(src), [pl.BlockSpec((tm, d), ctx_map), pl.BlockSpec((tm, d), lat_map)]
    else:
        operands, in_specs = [src], [row_spec]
    if has_delta:
        operands += [delta, gate_mod]
        in_specs += [row_spec, mod_spec(gate_slot)]
    operands += [gain.reshape(1, d), mod, mod]
    in_specs += [vec_spec, mod_spec(shift_slot), mod_spec(scale_slot)]
    if has_router:
        rw, rb = router
        operands += [rw, rb]
        in_specs += [pl.BlockSpec(rw.shape, lambda i: (0, 0)), pl.BlockSpec(rb.shape, lambda i: (0, 0))]
    out_shape, out_specs = [], []
    if has_delta:
        out_shape.append(jax.ShapeDtypeStruct((t, d), F32))
        out_specs.append(row_spec)
    out_shape.append(jax.ShapeDtypeStruct((t, d), BF16))
    out_specs.append(row_spec)
    if has_router:
        out_shape.append(jax.ShapeDtypeStruct((t, LANES), F32))
        out_specs.append(pl.BlockSpec((tm, LANES), lambda i: (i, 0)))
    kern = functools.partial(_resid_norm_kernel, split_src=split_src, has_delta=has_delta, has_router=has_router,
                             tiles_per_seq=tiles_per_seq, ctx_tiles=ctx_tiles)
    return pl.pallas_call(
        kern, grid=(t // tm,), in_specs=in_specs, out_specs=out_specs, out_shape=out_shape,
        compiler_params=_params(("parallel",)), name="resid_norm",
    )(*operands)


def _final_norm_kernel(z_ref, delta_ref, gate_ref, gain_ref, out_ref):
    row = 1 + pl.program_id(0)
    z = z_ref[...] + gate_ref[pl.ds(row, 1), :] * delta_ref[...].astype(F32)
    out_ref[...] = _rmsnorm(z, gain_ref[...])


def _final_norm(z, delta, gate_mod, gate_slot, gain, batch, seq_rows, ctx_rows):
    t, d = z.shape
    tm = ROW_TILE
    lat_tiles = (seq_rows - ctx_rows) // tm
    tiles_per_seq = seq_rows // tm
    ctx_tiles = ctx_rows // tm
    row_spec = pl.BlockSpec((tm, d), lambda b, j: (b * tiles_per_seq + ctx_tiles + j, 0))
    return pl.pallas_call(
        _final_norm_kernel,
        grid=(batch, lat_tiles),
        in_specs=[row_spec, row_spec,
                  pl.BlockSpec((gate_mod.shape[0], d), lambda b, j: (0, gate_slot)),
                  pl.BlockSpec((1, d), lambda b, j: (0, 0))],
        out_specs=pl.BlockSpec((tm, d), lambda b, j: (b * lat_tiles + j, 0)),
        out_shape=jax.ShapeDtypeStruct((batch * lat_tiles * tm, d), F32),
        compiler_params=_params(("parallel", "parallel")), name="final_norm",
    )(z, delta, gate_mod, gain.reshape(1, d))


def _matmul_kernel(a_ref, w_ref, o_ref):
    o_ref[...] = _dot(a_ref[...], w_ref[...]).astype(o_ref.dtype)


def _matmul(a, w, layer, out_dtype, tm, tn):
    m, k = a.shape
    n = w.shape[2]
    return pl.pallas_call(
        _matmul_kernel,
        grid=(m // tm, n // tn),
        in_specs=[pl.BlockSpec((tm, k), lambda i, j: (i, 0)),
                  pl.BlockSpec((None, k, tn), lambda i, j: (layer, 0, j))],
        out_specs=pl.BlockSpec((tm, tn), lambda i, j: (i, j)),
        out_shape=jax.ShapeDtypeStruct((m, n), out_dtype),
        compiler_params=_params(("parallel", "arbitrary")), name="matmul",
    )(a, w)


def _in_proj_kernel(a_ref, w_ref, wg_ref, o_ref, gc_ref, gr_ref):
    a = a_ref[...]
    o_ref[...] = _dot_nt(a, w_ref[...]).astype(o_ref.dtype)

    @pl.when(pl.program_id(1) == 0)
    def _():
        wg = wg_ref[...]
        gc_ref[...] = _dot_nt(a, wg)
        gr_ref[...] = _dot_nt(wg, a)


def _in_proj(a, w_t, wg_t, layer, tm, tn):
    m, k = a.shape
    n = w_t.shape[1]
    return pl.pallas_call(
        _in_proj_kernel,
        grid=(m // tm, n // tn),
        in_specs=[pl.BlockSpec((tm, k), lambda i, j: (i, 0)),
                  pl.BlockSpec((None, tn, k), lambda i, j: (layer, j, 0)),
                  pl.BlockSpec((None, LANES, k), lambda i, j: (layer, 0, 0))],
        out_specs=[pl.BlockSpec((tm, tn), lambda i, j: (i, j)),
                   pl.BlockSpec((tm, LANES), lambda i, j: (i, 0)),
                   pl.BlockSpec((LANES, tm), lambda i, j: (0, i))],
        out_shape=[jax.ShapeDtypeStruct((m, n), BF16), jax.ShapeDtypeStruct((m, LANES), F32),
                   jax.ShapeDtypeStruct((LANES, m), F32)],
        compiler_params=_params(("parallel", "arbitrary")), name="in_proj",
    )(a, w_t, wg_t)


def _tri_masks(length, reverse):
    row = lax.broadcasted_iota(jnp.int32, (length, length), 0)
    col = lax.broadcasted_iota(jnp.int32, (length, length), 1)
    if reverse:
        return col >= row, row >= col
    return col <= row, row <= col


def _mlstm_kernel(q_ref, k_ref, v_ref, gc_ref, gr_ref, bc_ref, br_ref, h_ref, ct_ref, n_ref, m_ref, *,
                  reverse, heads, dqk, dv):
    @pl.when(pl.program_id(1) == 0)
    def _():
        ct_ref[...] = jnp.zeros_like(ct_ref)
        n_ref[...] = jnp.zeros_like(n_ref)
        m_ref[...] = jnp.zeros_like(m_ref)

    length = q_ref.shape[0]
    valid, valid_t = _tri_masks(length, reverse)
    gc = gc_ref[...] + bc_ref[...]
    gr = gr_ref[...] + br_ref[...]
    log_f_c = _log_sigmoid(gc)
    log_f_r = _log_sigmoid(gr)
    d0 = 2 * heads if reverse else 0
    scale = dqk ** -0.5
    for h in range(heads):
        i_col = gc[:, d0 + h:d0 + h + 1]
        f_col = log_f_c[:, d0 + heads + h:d0 + heads + h + 1]
        i_row = gr[d0 + h:d0 + h + 1, :]
        f_row = log_f_r[d0 + heads + h:d0 + heads + h + 1, :]
        b_col = jnp.sum(jnp.where(valid, f_row, 0.0), axis=1, keepdims=True)
        b_row = jnp.sum(jnp.where(valid_t, f_col, 0.0), axis=0, keepdims=True)
        b_end = jnp.sum(f_row, axis=1, keepdims=True)
        m_prev = m_ref[h][:, :1]
        log_d = jnp.where(valid, b_col - b_row + i_row, NEG)
        m_inter = b_col + m_prev
        m_t = jnp.maximum(m_inter, jnp.max(log_d, axis=1, keepdims=True))
        w_intra = jnp.exp(log_d - m_t) * scale
        w_inter = jnp.exp(m_inter - m_t) * scale
        q = q_ref[:, h * dqk:(h + 1) * dqk]
        k = k_ref[:, h * dqk:(h + 1) * dqk]
        v = v_ref[:, h * dv:(h + 1) * dv]
        s = _dot_nt(q, k) * w_intra
        num = _dot(s.astype(BF16), v) + w_inter * _dot(q, ct_ref[h].astype(BF16))
        qn = jnp.sum(q.astype(F32) * n_ref[h], axis=1, keepdims=True)
        den = jnp.sum(s, axis=1, keepdims=True) + w_inter * qn
        h_ref[:, h * dv:(h + 1) * dv] = (num / jnp.maximum(jnp.abs(den), jnp.exp(-m_t))).astype(h_ref.dtype)
        log_w = b_end - b_col + i_col
        m_new = jnp.maximum(b_end + m_prev, jnp.max(log_w, axis=0, keepdims=True))
        w_end = jnp.exp(log_w - m_new)
        decay = jnp.exp(b_end + m_prev - m_new)
        wv = (w_end * v.astype(F32)).astype(BF16)
        ct_ref[h] = decay * ct_ref[h] + _dot_tn(k, wv)
        n_ref[h] = decay * n_ref[h] + jnp.sum(w_end * k.astype(F32), axis=0, keepdims=True)
        m_ref[h] = jnp.broadcast_to(m_new, (1, LANES))


def _chunk_index(i, n_chunks, reverse):
    if not reverse:
        return i
    return jnp.where(i == 0, 0, n_chunks - i)


def _mlstm(p, col0, dv, gates_c, gates_r, bias_c, bias_r, batch, seq_rows, reverse):
    t = p.shape[0]
    length = ROW_TILE
    n_chunks = seq_rows // length
    heads = A_HEADS
    dqk = dv // 2
    qw, vw = heads * dqk, heads * dv

    def rows(b, i):
        return b * n_chunks + _chunk_index(i, n_chunks, reverse)

    kern = functools.partial(_mlstm_kernel, reverse=reverse, heads=heads, dqk=dqk, dv=dv)
    return pl.pallas_call(
        kern,
        grid=(batch, n_chunks),
        in_specs=[
            pl.BlockSpec((length, qw), lambda b, i: (rows(b, i), col0 // qw)),
            pl.BlockSpec((length, qw), lambda b, i: (rows(b, i), col0 // qw + 1)),
            pl.BlockSpec((length, vw), lambda b, i: (rows(b, i), (col0 + 2 * qw) // vw)),
            pl.BlockSpec((length, LANES), lambda b, i: (rows(b, i), 0)),
            pl.BlockSpec((4 * heads, length), lambda b, i: (0, rows(b, i))),
            pl.BlockSpec((1, LANES), lambda b, i: (0, 0)),
            pl.BlockSpec((4 * heads, 1), lambda b, i: (0, 0)),
        ],
        out_specs=pl.BlockSpec((length, vw), lambda b, i: (rows(b, i), 0)),
        out_shape=jax.ShapeDtypeStruct((t, vw), BF16),
        scratch_shapes=[pltpu.VMEM((heads, dqk, dv), F32), pltpu.VMEM((heads, 1, dqk), F32),
                        pltpu.VMEM((heads, 1, LANES), F32)],
        compiler_params=_params(("parallel", "arbitrary")), name="mlstm_scan",
    )(p, p, p, gates_c, gates_r, bias_c, bias_r)


def _rope(t, cosf, sinf, half):
    if 2 * half == LANES:
        partner = pltpu.roll(t, half, axis=1)
    else:
        lane = lax.broadcasted_iota(jnp.int32, t.shape, 1)
        partner = jnp.where((lane & (2 * half - 1)) < half, pltpu.roll(t, LANES - half, axis=1),
                            pltpu.roll(t, half, axis=1))
    return t * cosf + partner * sinf


def _retention_kernel(q_ref, k_ref, v_ref, cos_ref, sin_ref, dp_ref, o_ref, s_ref, *, reverse, heads, dk, dv):
    @pl.when(pl.program_id(1) == 0)
    def _():
        s_ref[...] = jnp.zeros_like(s_ref)

    length = q_ref.shape[0]
    row = lax.broadcasted_iota(jnp.int32, (length, length), 0)
    col = lax.broadcasted_iota(jnp.int32, (length, length), 1)
    diff = (col - row if reverse else row - col).astype(F32)
    tpos = lax.broadcasted_iota(jnp.int32, (length, 1), 0).astype(F32)
    lg_all = -_softplus(dp_ref[...])
    cosf = cos_ref[...]
    sinf = sin_ref[...]
    scale = dk ** -0.5
    d = 1 if reverse else 0
    for h in range(heads):
        lg = lg_all[d:d + 1, h:h + 1]
        decay_mat = jnp.where(diff >= 0.0, jnp.exp(lg * jnp.maximum(diff, 0.0)), 0.0) * scale
        if reverse:
            read_w = jnp.exp(lg * (length - tpos))
            write_w = jnp.exp(lg * tpos)
        else:
            read_w = jnp.exp(lg * (tpos + 1.0))
            write_w = jnp.exp(lg * (length - 1.0 - tpos))
        chunk_decay = jnp.exp(lg * float(length))
        q = _rope(q_ref[:, h * dk:(h + 1) * dk].astype(F32), cosf, sinf, dk // 2).astype(BF16)
        k = _rope(k_ref[:, h * dk:(h + 1) * dk].astype(F32), cosf, sinf, dk // 2).astype(BF16)
        v = v_ref[:, h * dv:(h + 1) * dv]
        s = _dot_nt(q, k) * decay_mat
        o_ref[:, h * dv:(h + 1) * dv] = (_dot(s.astype(BF16), v)
                                         + read_w * _dot(q, s_ref[h].astype(BF16))).astype(o_ref.dtype)
        wv = (write_w * v.astype(F32)).astype(BF16)
        s_ref[h] = chunk_decay * s_ref[h] + scale * _dot_tn(k, wv)


def _retention(p, col0, dv, cosf, sinf, decay_p, batch, seq_rows, reverse):
    t = p.shape[0]
    length = ROW_TILE
    n_chunks = seq_rows // length
    heads = B_HEADS
    dk = dv // 2
    assert dk == LANES
    qw, vw = heads * dk, heads * dv

    def rows(b, i):
        return b * n_chunks + _chunk_index(i, n_chunks, reverse)

    kern = functools.partial(_retention_kernel, reverse=reverse, heads=heads, dk=dk, dv=dv)
    return pl.pallas_call(
        kern,
        grid=(batch, n_chunks),
        in_specs=[
            pl.BlockSpec((length, qw), lambda b, i: (rows(b, i), col0 // qw)),
            pl.BlockSpec((length, qw), lambda b, i: (rows(b, i), col0 // qw + 1)),
            pl.BlockSpec((length, vw), lambda b, i: (rows(b, i), (col0 + 2 * qw) // vw)),
            pl.BlockSpec((length, LANES), lambda b, i: (_chunk_index(i, n_chunks, reverse), 0)),
            pl.BlockSpec((length, LANES), lambda b, i: (_chunk_index(i, n_chunks, reverse), 0)),
            pl.BlockSpec(decay_p.shape, lambda b, i: (0, 0)),
        ],
        out_specs=pl.BlockSpec((length, vw), lambda b, i: (rows(b, i), 0)),
        out_shape=jax.ShapeDtypeStruct((t, vw), BF16),
        scratch_shapes=[pltpu.VMEM((heads, dk, dv), F32)],
        compiler_params=_params(("parallel", "arbitrary")), name="retention_scan",
    )(p, p, p, cosf, sinf, decay_p)


def _headnorm_kernel(hf_ref, hb_ref, gate_ref, gain_ref, y_ref, *, heads, dv, center, silu_gate):
    g = gate_ref[...].astype(F32)
    sg = _sigmoid(g)
    gate = g * sg if silu_gate else sg
    gain = gain_ref[...]
    for h in range(heads):
        sl = slice(h * dv, (h + 1) * dv)
        x = hf_ref[:, sl].astype(F32) + hb_ref[:, sl].astype(F32)
        if center:
            x = x - jnp.mean(x, axis=-1, keepdims=True)
        y = x * lax.rsqrt(jnp.mean(x * x, axis=-1, keepdims=True) + EPS)
        y_ref[:, sl] = (gate[:, sl] * (y * gain[:, sl])).astype(y_ref.dtype)


def _headnorm(hf, hb, p, gate_col, gain, heads, center, silu_gate):
    t, w = hf.shape
    tm = _row_tile(t, 2 * ROW_TILE)
    kern = functools.partial(_headnorm_kernel, heads=heads, dv=w // heads, center=center, silu_gate=silu_gate)
    row_spec = pl.BlockSpec((tm, w), lambda i: (i, 0))
    return pl.pallas_call(
        kern, grid=(t // tm,),
        in_specs=[row_spec, row_spec, pl.BlockSpec((tm, w), lambda i: (i, gate_col // w)),
                  pl.BlockSpec((1, w), lambda i: (0, 0))],
        out_specs=row_spec,
        out_shape=jax.ShapeDtypeStruct((t, w), BF16),
        compiler_params=_params(("parallel",)), name="headnorm_gate",
    )(hf, hb, p, gain.reshape(1, w))


def _lane_low(shape, hd):
    return lax.broadcasted_iota(jnp.int32, shape, 1) < hd


def _stack_heads(q, hd):
    low = _lane_low(q.shape, hd)
    zero = jnp.zeros_like(q)
    return jnp.concatenate([jnp.where(low, q, zero), jnp.where(low, zero, q)], axis=0)


def _unstack_heads(o, hd):
    rows = o.shape[0] // 2
    return jnp.where(_lane_low((rows, LANES), hd), o[:rows], o[rows:])


def _window_kernel(sink_ref, q_ref, kp_ref, k0_ref, kn_ref, vp_ref, v0_ref, vn_ref, kc_ref, vc_ref,
                   cq_ref, sq_ref, cp_ref, sp_ref, cn_ref, sn_ref, o_ref, *, heads, kv_heads, hd, ctx_blocks,
                   n_latent, slabs_per_dot):
    i = pl.program_id(1)
    j = i - ctx_blocks
    w = q_ref.shape[0]
    half = hd // 2
    slabs_per_group = heads // kv_heads // 2

    def roped(x_ref, c_ref, s_ref):
        return _rope(x_ref[...].astype(F32), c_ref[...], s_ref[...], half)

    kb = jnp.concatenate([roped(kp_ref, cp_ref, sp_ref), roped(k0_ref, cq_ref, sq_ref),
                          roped(kn_ref, cn_ref, sn_ref)], axis=0)
    vb = jnp.concatenate([vp_ref[...], v0_ref[...], vn_ref[...]], axis=0).astype(F32)
    kc = kc_ref[...].astype(F32)
    vc = vc_ref[...].astype(F32)

    def group_copy(x, g):
        keep = _lane_low(x.shape, hd) if g == 0 else jnp.logical_not(_lane_low(x.shape, hd))
        return jnp.where(keep, x, pltpu.roll(x, hd, axis=1)).astype(BF16)

    c = lax.broadcasted_iota(jnp.int32, (3 * w, w), 0)
    t = lax.broadcasted_iota(jnp.int32, (3 * w, w), 1)
    lower = jnp.maximum(t + (w - WINDOW), (1 - j) * w)
    upper = jnp.minimum(t + (w + WINDOW), n_latent - 1 - (j - 1) * w)
    upper = jnp.where(j < 0, -1, upper)
    mask_bias = jnp.where((c >= lower) & (c <= upper), 0.0, NEG)
    mask_bias = jnp.concatenate([mask_bias] * (2 * slabs_per_dot), axis=1)
    scale = hd ** -0.5
    cq = cq_ref[...]
    sq = sq_ref[...]
    for g in range(kv_heads):
        kk, vv, kkc, vvc = group_copy(kb, g), group_copy(vb, g), group_copy(kc, g), group_copy(vc, g)
        for s0 in range(0, slabs_per_group, slabs_per_dot):
            slabs = [g * slabs_per_group + s0 + s for s in range(slabs_per_dot)]
            qs = []
            sinks = []
            for slab in slabs:
                qr = _rope(q_ref[:, slab * LANES:(slab + 1) * LANES].astype(F32), cq, sq, half)
                qs.append(_stack_heads((qr * scale).astype(BF16), hd))
                sinks += [jnp.full((1, w), sink_ref[2 * slab], F32), jnp.full((1, w), sink_ref[2 * slab + 1], F32)]
            qq = jnp.concatenate(qs, axis=0)
            sink = jnp.concatenate(sinks, axis=1)
            s_loc = _dot_nt(kk, qq) + mask_bias
            s_ctx = _dot_nt(kkc, qq)
            m = jnp.maximum(jnp.maximum(jnp.max(s_loc, axis=0, keepdims=True),
                                        jnp.max(s_ctx, axis=0, keepdims=True)), sink)
            e_loc = jnp.exp(s_loc - m)
            e_ctx = jnp.exp(s_ctx - m)
            den = (jnp.exp(sink - m) + jnp.sum(e_loc, axis=0, keepdims=True)
                   + jnp.sum(e_ctx, axis=0, keepdims=True))
            o_t = (_dot_tn(vv, e_loc.astype(BF16)) + _dot_tn(vvc, e_ctx.astype(BF16))) / den
            o = o_t.T
            for n, slab in enumerate(slabs):
                o_ref[:, slab * LANES:(slab + 1) * LANES] = _unstack_heads(
                    o[2 * n * w:2 * (n + 1) * w], hd).astype(o_ref.dtype)


def _window_attention(p, col0, cosf, sinf, sink, batch, seq_rows, ctx_rows):
    t = p.shape[0]
    w = WINDOW
    heads, kv_heads = C_HEADS, C_KV_HEADS
    hd = LANES // 2
    assert kv_heads * hd == LANES
    qw = heads * hd
    nblk = seq_rows // w
    ctx_blocks = ctx_rows // w
    kcol = (col0 + qw) // LANES
    vcol = kcol + 1

    def cur(b, i):
        return b * nblk + i

    def prev(b, i):
        return b * nblk + jnp.maximum(i - 1, 0)

    def nxt(b, i):
        return b * nblk + jnp.minimum(i + 1, nblk - 1)

    def kv_spec(fn, colblk):
        return pl.BlockSpec((w, LANES), lambda b, i: (fn(b, i), colblk))

    def tab_spec(fn):
        return pl.BlockSpec((w, LANES), lambda b, i: (fn(0, i), 0))

    kern = functools.partial(_window_kernel, heads=heads, kv_heads=kv_heads, hd=hd, ctx_blocks=ctx_blocks,
                             n_latent=seq_rows - ctx_rows, slabs_per_dot=4)
    ctx_spec_k = pl.BlockSpec((ctx_rows, LANES), lambda b, i: (b * (seq_rows // ctx_rows), kcol))
    ctx_spec_v = pl.BlockSpec((ctx_rows, LANES), lambda b, i: (b * (seq_rows // ctx_rows), vcol))
    return pl.pallas_call(
        kern,
        grid=(batch, nblk),
        in_specs=[
            pl.BlockSpec(memory_space=pltpu.SMEM),
            pl.BlockSpec((w, qw), lambda b, i: (cur(b, i), col0 // qw)),
            kv_spec(prev, kcol), kv_spec(cur, kcol), kv_spec(nxt, kcol),
            kv_spec(prev, vcol), kv_spec(cur, vcol), kv_spec(nxt, vcol),
            ctx_spec_k, ctx_spec_v,
            tab_spec(cur), tab_spec(cur), tab_spec(prev), tab_spec(prev), tab_spec(nxt), tab_spec(nxt),
        ],
        out_specs=pl.BlockSpec((w, qw), lambda b, i: (cur(b, i), 0)),
        out_shape=jax.ShapeDtypeStruct((t, qw), BF16),
        compiler_params=_params(("parallel", "arbitrary")), name="window_attention",
    )(sink, p, p, p, p, p, p, p, p, p, cosf, sinf, cosf, sinf, cosf, sinf)


def _nbr_layout():
    masked = 2 * NA_ROWS - 1
    variants = [(0, [0] * NBR_ROWS), (NA_ROWS // 2, list(range(NBR_ROWS))),
                (NBR_UNION - NBR_ROWS, [NBR_UNION - NA_ROWS] * NBR_ROWS)]
    pairs, index = [], []
    for delta, rel_start in variants:
        per_kr = []
        for kr in range(NBR_UNION):
            codes = [kr - (delta + rr) + NA_ROWS - 1 if rel_start[rr] <= kr < rel_start[rr] + NA_ROWS else masked
                     for rr in range(NBR_ROWS)]
            row = []
            for a in range(NBR_ROWS // 2):
                pair = (codes[2 * a], codes[2 * a + 1])
                if pair not in pairs:
                    pairs.append(pair)
                row.append(pairs.index(pair))
            per_kr.append(row)
        index.append(per_kr)
    return pairs, index


def _nbr_kernel(q_ref, k_ref, v_ref, tab_ref, o_ref, bias_ref, *, hd, ctx_rows, grid_rows, index, slabs_per_step):
    i = pl.program_id(2)
    scale = hd ** -0.5
    tq = q_ref.shape[0]
    lanes = [slice(sl * LANES, (sl + 1) * LANES) for sl in range(slabs_per_step)]
    qqs = [_stack_heads(q_ref[:, ln] * scale, hd) for ln in lanes]
    kcs = [k_ref[0:ctx_rows, ln] for ln in lanes]
    vcs = [v_ref[0:ctx_rows, ln] for ln in lanes]
    s_ctxs = [_dot_nt(kc, qq) for kc, qq in zip(kcs, qqs)]
    m_ctxs = [jnp.max(s, axis=0, keepdims=True) for s in s_ctxs]

    @pl.when(i == 0)
    def _():
        for sl, ln in enumerate(lanes):
            for variant, per_kr in enumerate(index):
                for kr, row in enumerate(per_kr):
                    for sub in range(2):
                        for a, u in enumerate(row):
                            bias_ref[sl, variant, kr * GRID_W:(kr + 1) * GRID_W,
                                     sub * tq + a * LANES:sub * tq + (a + 1) * LANES] = tab_ref[2 * sl + sub, u]
            e = jnp.exp(s_ctxs[sl] - m_ctxs[sl])
            o_t = _dot_tn(vcs[sl], e.astype(BF16)) / jnp.sum(e, axis=0, keepdims=True)
            o_ref[:, ln] = _unstack_heads(o_t.T, hd).astype(o_ref.dtype)

    @pl.when(i > 0)
    def _():
        blk = i - 1
        n_blk = grid_rows // NBR_ROWS
        u0 = jnp.clip(blk * NBR_ROWS - NA_ROWS // 2, 0, grid_rows - NBR_UNION)
        variant = jnp.where(blk == 0, 0, jnp.where(blk == n_blk - 1, 2, 1))
        base = pl.multiple_of(ctx_rows + u0 * GRID_W, GRID_W)
        for sl, ln in enumerate(lanes):
            kw = k_ref[pl.ds(base, NBR_UNION * GRID_W), ln]
            vw = v_ref[pl.ds(base, NBR_UNION * GRID_W), ln]
            s_loc = _dot_nt(kw, qqs[sl]) + bias_ref[sl, variant]
            m = jnp.maximum(jnp.max(s_loc, axis=0, keepdims=True), m_ctxs[sl])
            e_loc = jnp.exp(s_loc - m)
            e_ctx = jnp.exp(s_ctxs[sl] - m)
            den = jnp.sum(e_loc, axis=0, keepdims=True) + jnp.sum(e_ctx, axis=0, keepdims=True)
            o_t = (_dot_tn(vw, e_loc.astype(BF16)) + _dot_tn(vcs[sl], e_ctx.astype(BF16))) / den
            o_ref[:, ln] = _unstack_heads(o_t.T, hd).astype(o_ref.dtype)


def _nbr_table(na_bias_l, pairs):
    col = jnp.arange(GRID_W)
    col_start = jnp.clip(col - NA_COLS // 2, 0, GRID_W - NA_COLS)
    col_valid = (col[None, :] >= col_start[:, None]) & (col[None, :] < col_start[:, None] + NA_COLS)
    dc = jnp.clip(col[None, :] - col[:, None], -(NA_COLS - 1), NA_COLS - 1) + (NA_COLS - 1)
    blocks = jnp.where(col_valid.T[None, None], na_bias_l[:, :, dc.T], NEG)
    blocks = jnp.concatenate([blocks, jnp.full_like(blocks[:, :1], NEG)], axis=1)
    left = blocks[:, np.array([p[0] for p in pairs])]
    right = blocks[:, np.array([p[1] for p in pairs])]
    return jnp.concatenate([left, right], axis=-1).astype(F32)


def _nbr_attention(p, col0, na_bias_l, batch, seq_rows, ctx_rows):
    t = p.shape[0]
    heads = D_HEADS
    hd = LANES // 2
    qw = heads * hd
    slabs = heads // 2
    tq = NBR_ROWS * GRID_W
    n_latent = seq_rows - ctx_rows
    grid_rows = n_latent // GRID_W
    assert tq == ctx_rows and grid_rows % NBR_ROWS == 0 and grid_rows >= NBR_UNION + 1
    assert slabs % NBR_SLABS_PER_STEP == 0 and (col0 // LANES) % NBR_SLABS_PER_STEP == 0
    steps = seq_rows // tq
    q0 = col0 // LANES
    pairs, index = _nbr_layout()
    table = _nbr_table(na_bias_l, pairs)
    sps = NBR_SLABS_PER_STEP
    width = sps * LANES
    kern = functools.partial(_nbr_kernel, hd=hd, ctx_rows=ctx_rows, grid_rows=grid_rows, index=index,
                             slabs_per_step=sps)
    return pl.pallas_call(
        kern,
        grid=(batch, slabs // sps, steps),
        in_specs=[
            pl.BlockSpec((tq, width), lambda b, s, i: (b * steps + i, q0 // sps + s)),
            pl.BlockSpec((seq_rows, width), lambda b, s, i: (b, (q0 + slabs) // sps + s)),
            pl.BlockSpec((seq_rows, width), lambda b, s, i: (b, (q0 + 2 * slabs) // sps + s)),
            pl.BlockSpec((2 * sps,) + table.shape[1:], lambda b, s, i: (s, 0, 0, 0)),
        ],
        out_specs=pl.BlockSpec((tq, width), lambda b, s, i: (b * steps + i, s)),
        out_shape=jax.ShapeDtypeStruct((t, qw), BF16),
        scratch_shapes=[pltpu.VMEM((sps, len(index), NBR_UNION * GRID_W, 2 * tq), F32)],
        compiler_params=_params(("parallel", "parallel", "arbitrary")), name="nbr_attention",
    )(p, p, p, table)


def _merge_kernel(*refs):
    ys = refs[0:N_BRANCH]
    gs = refs[N_BRANCH:2 * N_BRANCH]
    ws = refs[2 * N_BRANCH:3 * N_BRANCH]
    o_ref = refs[3 * N_BRANCH]
    acc = None
    for y_ref, g_ref, w_ref in zip(ys, gs, ws):
        term = _sigmoid(g_ref[...].astype(F32)) * _dot(y_ref[...], w_ref[...])
        acc = term if acc is None else acc + term
    o_ref[...] = acc.astype(o_ref.dtype)


def _merge(ys, p, gate_col0, w_branch, row0, tm, tn):
    t, bw = ys[0].shape
    d = w_branch.shape[1]
    nt = d // tn
    y_spec = pl.BlockSpec((tm, bw), lambda i, j: (i, 0))
    g_specs = [pl.BlockSpec((tm, tn), functools.partial(lambda i, j, br: (i, (gate_col0 + br * d) // tn + j), br=br))
               for br in range(N_BRANCH)]
    w_specs = [pl.BlockSpec((bw, tn), functools.partial(lambda i, j, br: (row0 // bw + br, j), br=br))
               for br in range(N_BRANCH)]
    return pl.pallas_call(
        _merge_kernel,
        grid=(t // tm, nt),
        in_specs=[y_spec] * N_BRANCH + g_specs + w_specs,
        out_specs=pl.BlockSpec((tm, tn), lambda i, j: (i, j)),
        out_shape=jax.ShapeDtypeStruct((t, d), BF16),
        compiler_params=_params(("parallel", "arbitrary")), name="merge_branches",
    )(*ys, *([p] * N_BRANCH), *([w_branch] * N_BRANCH))


def _moe_kernel(t_ref, gates_ref, w1_ref, b1_ref, w2_ref, b2_ref, o_ref, acc_ref, *, ff, pairs):
    e = pl.program_id(1)
    gates = gates_ref[...]

    @pl.when(e == 0)
    def _():
        acc_ref[...] = _dot(gates.astype(BF16), b2_ref[...])

    tokens = t_ref[...]
    lane = lax.broadcasted_iota(jnp.int32, gates.shape, 1)
    acts = []
    for pr in range(pairs):
        hid = _dot(tokens, w1_ref[pr]) + b1_ref[pr]
        g_h = jnp.minimum(hid[:, :2 * ff], SWIGLU_LIMIT)
        u_h = jnp.clip(hid[:, 2 * ff:], -SWIGLU_LIMIT, SWIGLU_LIMIT)
        cols = []
        for s in range(2):
            expert = (e * pairs + pr) * 2 + s
            gsel = jnp.sum(jnp.where(lane == expert, gates, 0.0), axis=-1, keepdims=True)
            cols.append(jnp.broadcast_to(gsel, (gates.shape[0], ff)))
        act = g_h * _sigmoid(SWIGLU_ALPHA * g_h) * (u_h + 1.0) * jnp.concatenate(cols, axis=-1)
        acts.append(act.astype(BF16))
    acc_ref[...] += _dot(jnp.concatenate(acts, axis=-1), w2_ref[...])

    @pl.when(e == pl.num_programs(1) - 1)
    def _():
        o_ref[...] = acc_ref[...].astype(o_ref.dtype)


def _moe(h, gates, w1p, b1p, w2, b2p, pair0, tm, pairs):
    t, d = h.shape
    width = w1p.shape[2]
    ff = width // 4
    steps = b1p.shape[0] // pairs
    kern = functools.partial(_moe_kernel, ff=ff, pairs=pairs)
    return pl.pallas_call(
        kern,
        grid=(t // tm, steps),
        in_specs=[
            pl.BlockSpec((tm, d), lambda i, e: (i, 0)),
            pl.BlockSpec((tm, LANES), lambda i, e: (i, 0)),
            pl.BlockSpec((pairs, d, width), lambda i, e: (pair0 // pairs + e, 0, 0)),
            pl.BlockSpec((pairs, 1, width), lambda i, e: (e, 0, 0)),
            pl.BlockSpec((pairs * 2 * ff, d), lambda i, e: (pair0 // pairs + e, 0)),
            pl.BlockSpec(b2p.shape, lambda i, e: (0, 0)),
        ],
        out_specs=pl.BlockSpec((tm, d), lambda i, e: (i, 0)),
        out_shape=jax.ShapeDtypeStruct((t, d), BF16),
        scratch_shapes=[pltpu.VMEM((tm, d), F32)],
        compiler_params=_params(("parallel", "arbitrary")), name="moe_experts",
    )(h, gates, w1p, b1p, w2, b2p)


def _rope_tables(n_latent, ctx_rows, hd):
    pos = np.arange(n_latent)
    row = (pos // GRID_W).astype(np.float32)
    col = (pos % GRID_W).astype(np.float32)
    n_freq = hd // 4
    inv = jnp.asarray(ROPE_BASE, F32) ** (-jnp.arange(n_freq, dtype=F32) / n_freq)
    ang = jnp.concatenate([jnp.asarray(row)[:, None] * inv, jnp.asarray(col)[:, None] * inv], axis=-1)
    ang = jnp.concatenate([jnp.zeros((ctx_rows, hd // 2), F32), ang], axis=0)
    cos, sin = jnp.cos(ang), jnp.sin(ang)
    reps = LANES // hd
    cosf = jnp.tile(jnp.concatenate([cos, cos], axis=-1), (1, reps))
    sinf = jnp.tile(jnp.concatenate([-sin, sin], axis=-1), (1, reps))
    return cosf, sinf


def _pad_lanes(a, value=0.0):
    pad = LANES - a.shape[-1]
    return jnp.pad(a, [(0, 0)] * (a.ndim - 1) + [(0, pad)], constant_values=value)


def kernel(x, c, ctx, c_ctx, ada_down, ada_up, ada_bias, norm_gain, w_in, mlstm_gate_bias, mlstm_norm_gain,
           ret_decay, ret_norm_gain, sink, na_bias, w_branch, w_out, router_w, router_b, moe_w1, moe_b1,
           moe_w2, moe_b2, final_gain):
    batch, n_latent, d = x.shape
    ctx_rows = ctx.shape[1]
    depth = w_in.shape[0]
    seq_rows = ctx_rows + n_latent
    t = batch * seq_rows
    bw = d // N_BRANCH
    n_experts = router_w.shape[2]
    ff = moe_w2.shape[2]

    a_dv, b_dv = bw // A_HEADS, bw // B_HEADS
    c_hd, d_hd = bw // C_HEADS, bw // D_HEADS
    assert c_hd == LANES // 2 and d_hd == LANES // 2
    a_sz = (bw // 2, bw // 2, bw, bw)
    n_gate = 4 * A_HEADS
    b_sz = (bw // 2, bw // 2, bw, bw)
    c_sz = (bw, C_KV_HEADS * c_hd, C_KV_HEADS * c_hd)
    d_sz = (bw, bw, bw)
    sizes = a_sz + (n_gate,) + b_sz + c_sz + d_sz + (N_BRANCH * d,)
    offs = [int(o) for o in np.concatenate([[0], np.cumsum(sizes)])]
    a0, g0, b0, c0, d0, bg0, end = offs[0], offs[4], offs[5], offs[9], offs[12], offs[15], offs[16]
    col_gate = 0
    col_a = N_BRANCH * d
    col_b = col_a + sum(a_sz)
    col_d = col_b + sum(b_sz)
    col_c = col_d + sum(d_sz)

    w_cat_t, w_g_t = _win_prep(jnp.swapaxes(w_in, 1, 2), [(bg0, end), (a0, g0), (b0, c0), (d0, bg0), (c0, d0)],
                               (g0, b0))
    w1p = _moe_w1_prep(moe_w1)
    wb_bf, wo_bf, w2_bf = _cast_bf16(w_branch.reshape(depth * N_BRANCH * bw, d), w_out.reshape(depth * d, d),
                                     moe_w2.reshape(depth * n_experts * ff, d))
    b1p_all = jnp.concatenate([moe_b1[..., 0::2].reshape(depth, n_experts // 2, 1, 2 * ff),
                               moe_b1[..., 1::2].reshape(depth, n_experts // 2, 1, 2 * ff)], axis=-1)
    b2p_all = jnp.pad(moe_b2, ((0, 0), (0, LANES - n_experts), (0, 0))).astype(BF16)

    mod_all = _ada_modulation(jnp.pad(jnp.concatenate([c_ctx[None], c], axis=0), ((0, 8 - 1 - batch), (0, 0))),
                              ada_down, ada_up, ada_bias)
    cos_b, sin_b = _rope_tables(n_latent, ctx_rows, b_dv // 2)
    cos_c, sin_c = _rope_tables(n_latent, ctx_rows, c_hd)
    decay_p = jnp.pad(ret_decay, ((0, 0), (0, 8 - ret_decay.shape[1]), (0, LANES - ret_decay.shape[2])))

    tm_proj = _row_tile(t, 768)
    tm_moe = _row_tile(t, 512)
    z = (ctx.reshape(batch * ctx_rows, d), x.reshape(batch * n_latent, d))
    delta = None
    for l in range(depth):
        mod = mod_all[l]
        if l == 0:
            h = _resid_norm(z, norm_gain[l, 0], mod, 0, 1, batch, seq_rows, ctx_rows)[0]
        else:
            z, h = _resid_norm(z, norm_gain[l, 0], mod, 0, 1, batch, seq_rows, ctx_rows, delta=delta,
                               gate_mod=mod_all[l - 1], gate_slot=5)
        p, gates_c, gates_r = _in_proj(h, w_cat_t, w_g_t, l, tm=tm_proj, tn=1280)
        gates_r = gates_r[:n_gate]
        bias_c = _pad_lanes(mlstm_gate_bias[l][None])
        bias_r = mlstm_gate_bias[l][:, None]

        hf = _mlstm(p, col_a, a_dv, gates_c, gates_r, bias_c, bias_r, batch, seq_rows, False)
        hb = _mlstm(p, col_a, a_dv, gates_c, gates_r, bias_c, bias_r, batch, seq_rows, True)
        y_a = _headnorm(hf, hb, p, col_a + 2 * bw, mlstm_norm_gain[l], A_HEADS, False, False)
        of = _retention(p, col_b, b_dv, cos_b, sin_b, decay_p[l], batch, seq_rows, False)
        ob = _retention(p, col_b, b_dv, cos_b, sin_b, decay_p[l], batch, seq_rows, True)
        y_b = _headnorm(of, ob, p, col_b + 2 * bw, ret_norm_gain[l], B_HEADS, True, True)
        y_c = _window_attention(p, col_c, cos_c, sin_c, sink[l], batch, seq_rows, ctx_rows)
        y_d = _nbr_attention(p, col_d, na_bias[l], batch, seq_rows, ctx_rows)

        acc = _merge([y_a, y_b, y_c, y_d], p, col_gate, wb_bf, l * N_BRANCH * bw, tm=tm_moe, tn=1024)
        delta1 = _matmul(acc, wo_bf.reshape(depth, d, d), l, BF16, tm=tm_proj, tn=1024)

        rw = _pad_lanes(router_w[l]).astype(BF16)
        rb = _pad_lanes(router_b[l][None], NEG)
        z, h2, gates = _resid_norm(z, norm_gain[l, 1], mod, 3, 4, batch, seq_rows, ctx_rows, delta=delta1,
                                   gate_mod=mod, gate_slot=2, router=(rw, rb))
        delta = _moe(h2, gates, w1p, b1p_all[l], w2_bf, b2p_all[l], l * (n_experts // 2), tm=tm_moe, pairs=2)

    out = _final_norm(z, delta, mod_all[depth - 1], 5, final_gain, batch, seq_rows, ctx_rows)
    return out.reshape(batch, n_latent, d)
```

```python
import functools

import jax
import jax.numpy as jnp
import numpy as np
from jax import lax
from jax.experimental import pallas as pl
from jax.experimental.pallas import tpu as pltpu

F32 = jnp.float32
BF16 = jnp.bfloat16

GRID_W = 64
EPS = 1e-6
NEG = -1e30
ROPE_BASE = 10000.0
N_BRANCH = 4
A_HEADS = 4
B_HEADS = 4
C_HEADS = 16
C_KV_HEADS = 2
D_HEADS = 16
WINDOW = 128
NA_ROWS = 8
NA_COLS = 16
TOP_K = 4
SWIGLU_LIMIT = 7.0
SWIGLU_ALPHA = 1.702

LANES = 128
VMEM_LIMIT = 56 * 1024 * 1024
ROW_TILE = 256
NBR_ROWS = 4
NBR_UNION = NBR_ROWS + NA_ROWS - 1
NBR_SLABS_PER_STEP = 2


def _row_tile(rows, target):
    best = ROW_TILE
    for tile in range(ROW_TILE, target + 1, ROW_TILE):
        if rows % tile == 0:
            best = tile
    assert rows % best == 0
    return best


def _params(sem, vmem=VMEM_LIMIT):
    return pltpu.CompilerParams(dimension_semantics=sem, vmem_limit_bytes=vmem)


def _dot(a, b):
    return jnp.dot(a, b, preferred_element_type=F32)


def _dot_nt(a, b):
    return lax.dot_general(a, b, (((1,), (1,)), ((), ())), preferred_element_type=F32)


def _dot_tn(a, b):
    return lax.dot_general(a, b, (((0,), (0,)), ((), ())), preferred_element_type=F32)


def _sigmoid(x):
    return 1.0 / (1.0 + jnp.exp(-x))


def _softplus(x):
    return jnp.maximum(x, 0.0) + jnp.log1p(jnp.exp(-jnp.abs(x)))


def _log_sigmoid(x):
    return -_softplus(-x)


def _win_prep_kernel(w_ref, g_ref, cat_ref, gt_ref):
    cat_ref[...] = w_ref[0].astype(BF16)

    @pl.when(pl.program_id(1) == 0)
    def _():
        g = g_ref[0].astype(BF16)
        gt_ref[...] = jnp.concatenate([g, jnp.zeros((LANES - g.shape[0], g.shape[1]), BF16)], axis=0)


def _win_prep(w_in_t, segments, gate_seg):
    depth, _, d = w_in_t.shape
    blk = ROW_TILE
    sub = 8
    assert all(a % sub == 0 and (b - a) % blk == 0 for a, b in segments)
    n_blk = sum((b - a) // blk for a, b in segments)
    ga, gb = gate_seg

    def src_row(j):
        s = jnp.int32(0)
        base = 0
        for a, b in segments:
            n = (b - a) // blk
            s = jnp.where((j >= base) & (j < base + n), a // sub + (j - base) * (blk // sub), s)
            base += n
        return s * sub

    return pl.pallas_call(
        _win_prep_kernel, grid=(depth, n_blk),
        in_specs=[pl.BlockSpec((pl.Element(1), pl.Element(blk), pl.Element(d)), lambda l, j: (l, src_row(j), 0)),
                  pl.BlockSpec((pl.Element(1), pl.Element(gb - ga), pl.Element(d)), lambda l, j: (l, ga, 0))],
        out_specs=[pl.BlockSpec((None, blk, d), lambda l, j: (l, j, 0)),
                   pl.BlockSpec((None, LANES, d), lambda l, j: (l, 0, 0))],
        out_shape=[jax.ShapeDtypeStruct((depth, n_blk * blk, d), BF16),
                   jax.ShapeDtypeStruct((depth, LANES, d), BF16)],
        compiler_params=_params(("parallel", "arbitrary")), name="win_prep",
    )(w_in_t, w_in_t)


def _moe_w1_prep_kernel(w_ref, perm_ref, o_ref, *, ff):
    parts = [_dot(w_ref[0, s].astype(BF16), perm_ref[...]) for s in range(2)]
    o_ref[0] = jnp.concatenate([parts[0][:, :ff], parts[1][:, :ff], parts[0][:, ff:], parts[1][:, ff:]],
                               axis=1).astype(BF16)


def _moe_w1_prep(moe_w1):
    depth, n_exp, d, two_ff = moe_w1.shape
    ff = two_ff // 2
    rows = 1024
    src = np.concatenate([np.arange(0, two_ff, 2), np.arange(1, two_ff, 2)])
    perm = jnp.asarray(np.arange(two_ff)[:, None] == src[None, :], BF16)
    kern = functools.partial(_moe_w1_prep_kernel, ff=ff)
    return pl.pallas_call(
        kern, grid=(depth * n_exp // 2, d // rows),
        in_specs=[pl.BlockSpec((1, 2, rows, two_ff), lambda e, i: (e, 0, i, 0)),
                  pl.BlockSpec((two_ff, two_ff), lambda e, i: (0, 0))],
        out_specs=pl.BlockSpec((1, rows, 2 * two_ff), lambda e, i: (e, i, 0)),
        out_shape=jax.ShapeDtypeStruct((depth * n_exp // 2, d, 2 * two_ff), BF16),
        compiler_params=_params(("parallel", "parallel")), name="moe_w1_prep",
    )(moe_w1.reshape(depth * n_exp // 2, 2, d, two_ff), perm)


def _cast_kernel(*refs):
    n = len(refs) // 2
    for src, dst in zip(refs[:n], refs[n:]):
        dst[...] = src[...].astype(dst.dtype)


def _cast_bf16(*arrays):
    rows, cols = arrays[0].shape
    tm = ROW_TILE
    spec = pl.BlockSpec((tm, cols), lambda i: (i, 0))
    return pl.pallas_call(
        _cast_kernel, grid=(rows // tm,),
        in_specs=[spec] * len(arrays), out_specs=[spec] * len(arrays),
        out_shape=[jax.ShapeDtypeStruct((rows, cols), BF16)] * len(arrays),
        compiler_params=_params(("parallel",)), name="cast_bf16",
    )(*arrays)


def _ada_kernel(cv_ref, down_ref, up_ref, bias_ref, out_ref):
    cv = cv_ref[...]
    a = cv * _sigmoid(cv)
    z = _dot(a.astype(BF16), down_ref[0].astype(BF16))
    out_ref[0] = _dot(z.astype(BF16), up_ref[0].astype(BF16)) + bias_ref[0]


def _ada_modulation(cvecs, ada_down, ada_up, ada_bias):
    depth, d, r = ada_down.shape
    rows = cvecs.shape[0]
    return pl.pallas_call(
        _ada_kernel,
        grid=(depth, 6),
        in_specs=[
            pl.BlockSpec((rows, d), lambda l, j: (0, 0)),
            pl.BlockSpec((1, d, r), lambda l, j: (l, 0, 0)),
            pl.BlockSpec((1, r, d), lambda l, j: (l, 0, j)),
            pl.BlockSpec((1, 1, d), lambda l, j: (l, 0, j)),
        ],
        out_specs=pl.BlockSpec((1, rows, d), lambda l, j: (l, 0, j)),
        out_shape=jax.ShapeDtypeStruct((depth, rows, 6 * d), F32),
        compiler_params=_params(("parallel", "arbitrary")),
        name="ada_modulation",
    )(cvecs, ada_down, ada_up, ada_bias.reshape(depth, 1, 6 * d))


def _rmsnorm(x, gain):
    return x * lax.rsqrt(jnp.mean(x * x, axis=-1, keepdims=True) + EPS) * gain


def _resid_norm_kernel(*refs, split_src, has_delta, has_router, tiles_per_seq, ctx_tiles):
    refs = list(refs)
    tile = pl.program_id(0)
    b = tile // tiles_per_seq
    j = tile - b * tiles_per_seq
    is_ctx = j < ctx_tiles
    row = jnp.where(is_ctx, 0, 1 + b)
    if split_src:
        c_ref = refs.pop(0)
        x_ref = refs.pop(0)
        z = jnp.where(is_ctx, c_ref[...], x_ref[...])
    else:
        z = refs.pop(0)[...]
    if has_delta:
        delta_ref = refs.pop(0)
        gate_ref = refs.pop(0)
    gain_ref = refs.pop(0)
    shift_ref = refs.pop(0)
    scale_ref = refs.pop(0)
    if has_router:
        rw_ref = refs.pop(0)
        rb_ref = refs.pop(0)
    if has_delta:
        znew_ref = refs.pop(0)
    h_ref = refs.pop(0)
    if has_router:
        gates_ref = refs.pop(0)

    if has_delta:
        z = z + gate_ref[pl.ds(row, 1), :] * delta_ref[...].astype(F32)
        znew_ref[...] = z
    y = _rmsnorm(z, gain_ref[...])
    h = y * (1.0 + scale_ref[pl.ds(row, 1), :]) + shift_ref[pl.ds(row, 1), :]
    hb = h.astype(BF16)
    h_ref[...] = hb
    if has_router:
        logits = _dot(hb, rw_ref[...]) + rb_ref[...]
        lane = lax.broadcasted_iota(jnp.int32, logits.shape, 1).astype(F32)
        gates = jnp.zeros_like(logits)
        den = jnp.zeros((logits.shape[0], 1), F32)
        top = None
        for _ in range(TOP_K):
            m = jnp.max(logits, axis=-1, keepdims=True)
            idx = jnp.min(jnp.where(logits == m, lane, float(LANES)), axis=-1, keepdims=True)
            sel = lane == idx
            if top is None:
                top = m
            e = jnp.exp(m - top)
            den = den + e
            gates = jnp.where(sel, e, gates)
            logits = jnp.where(sel, NEG * 2.0, logits)
        gates_ref[...] = gates / den


def _resid_norm(src, gain, mod, shift_slot, scale_slot, batch, seq_rows, ctx_rows, delta=None, gate_mod=None,
                gate_slot=None, router=None):
    split_src = isinstance(src, tuple)
    d = src[0].shape[1] if split_src else src.shape[1]
    t = batch * seq_rows
    tm = ROW_TILE
    tiles_per_seq = seq_rows // tm
    ctx_tiles = ctx_rows // tm
    lat_tiles = tiles_per_seq - ctx_tiles
    has_delta = delta is not None
    has_router = router is not None
    row_spec = pl.BlockSpec((tm, d), lambda i: (i, 0))
    vec_spec = pl.BlockSpec((1, d), lambda i: (0, 0))

    def mod_spec(slot):
        return pl.BlockSpec((mod.shape[0], d), lambda i: (0, slot))

    def ctx_map(i):
        b = i // tiles_per_seq
        return b * ctx_tiles + jnp.minimum(i - b * tiles_per_seq, ctx_tiles - 1), 0

    def lat_map(i):
        b = i // tiles_per_seq
        return b * lat_tiles + jnp.maximum(i - b * tiles_per_seq - ctx_tiles, 0), 0

    if split_src:
        operands, in_specs = list(src), [pl.BlockSpec((tm, d), ctx_map), pl.BlockSpec((tm, d), lat_map)]
    else:
        operands, in_specs = [src], [row_spec]
    if has_delta:
        operands += [delta, gate_mod]
        in_specs += [row_spec, mod_spec(gate_slot)]
    operands += [gain.reshape(1, d), mod, mod]
    in_specs += [vec_spec, mod_spec(shift_slot), mod_spec(scale_slot)]
    if has_router:
        rw, rb = router
        operands += [rw, rb]
        in_specs += [pl.BlockSpec(rw.shape, lambda i: (0, 0)), pl.BlockSpec(rb.shape, lambda i: (0, 0))]
    out_shape, out_specs = [], []
    if has_delta:
        out_shape.append(jax.ShapeDtypeStruct((t, d), F32))
        out_specs.append(row_spec)
    out_shape.append(jax.ShapeDtypeStruct((t, d), BF16))
    out_specs.append(row_spec)
    if has_router:
        out_shape.append(jax.ShapeDtypeStruct((t, LANES), F32))
        out_specs.append(pl.BlockSpec((tm, LANES), lambda i: (i, 0)))
    kern = functools.partial(_resid_norm_kernel, split_src=split_src, has_delta=has_delta, has_router=has_router,
                             tiles_per_seq=tiles_per_seq, ctx_tiles=ctx_tiles)
    return pl.pallas_call(
        kern, grid=(t // tm,), in_specs=in_specs, out_specs=out_specs, out_shape=out_shape,
        compiler_params=_params(("parallel",)), name="resid_norm",
    )(*operands)


def _final_norm_kernel(z_ref, delta_ref, gate_ref, gain_ref, out_ref):
    row = 1 + pl.program_id(0)
    z = z_ref[...] + gate_ref[pl.ds(row, 1), :] * delta_ref[...].astype(F32)
    out_ref[...] = _rmsnorm(z, gain_ref[...])


def _final_norm(z, delta, gate_mod, gate_slot, gain, batch, seq_rows, ctx_rows):
    t, d = z.shape
    tm = ROW_TILE
    lat_tiles = (seq_rows - ctx_rows) // tm
    tiles_per_seq = seq_rows // tm
    ctx_tiles = ctx_rows // tm
    row_spec = pl.BlockSpec((tm, d), lambda b, j: (b * tiles_per_seq + ctx_tiles + j, 0))
    return pl.pallas_call(
        _final_norm_kernel,
        grid=(batch, lat_tiles),
        in_specs=[row_spec, row_spec,
                  pl.BlockSpec((gate_mod.shape[0], d), lambda b, j: (0, gate_slot)),
                  pl.BlockSpec((1, d), lambda b, j: (0, 0))],
        out_specs=pl.BlockSpec((tm, d), lambda b, j: (b * lat_tiles + j, 0)),
        out_shape=jax.ShapeDtypeStruct((batch * lat_tiles * tm, d), F32),
        compiler_params=_params(("parallel", "parallel")), name="final_norm",
    )(z, delta, gate_mod, gain.reshape(1, d))


def _matmul_kernel(a_ref, w_ref, o_ref):
    o_ref[...] = _dot(a_ref[...], w_ref[...]).astype(o_ref.dtype)


def _matmul(a, w, layer, out_dtype, tm, tn):
    m, k = a.shape
    n = w.shape[2]
    return pl.pallas_call(
        _matmul_kernel,
        grid=(m // tm, n // tn),
        in_specs=[pl.BlockSpec((tm, k), lambda i, j: (i, 0)),
                  pl.BlockSpec((None, k, tn), lambda i, j: (layer, 0, j))],
        out_specs=pl.BlockSpec((tm, tn), lambda i, j: (i, j)),
        out_shape=jax.ShapeDtypeStruct((m, n), out_dtype),
        compiler_params=_params(("parallel", "arbitrary")), name="matmul",
    )(a, w)


def _in_proj_kernel(a_ref, w_ref, wg_ref, o_ref, gc_ref, gr_ref):
    a = a_ref[...]
    o_ref[...] = _dot_nt(a, w_ref[...]).astype(o_ref.dtype)

    @pl.when(pl.program_id(1) == 0)
    def _():
        wg = wg_ref[...]
        gc_ref[...] = _dot_nt(a, wg)
        gr_ref[...] = _dot_nt(wg, a)


def _in_proj(a, w_t, wg_t, layer, tm, tn):
    m, k = a.shape
    n = w_t.shape[1]
    return pl.pallas_call(
        _in_proj_kernel,
        grid=(m // tm, n // tn),
        in_specs=[pl.BlockSpec((tm, k), lambda i, j: (i, 0)),
                  pl.BlockSpec((None, tn, k), lambda i, j: (layer, j, 0)),
                  pl.BlockSpec((None, LANES, k), lambda i, j: (layer, 0, 0))],
        out_specs=[pl.BlockSpec((tm, tn), lambda i, j: (i, j)),
                   pl.BlockSpec((tm, LANES), lambda i, j: (i, 0)),
                   pl.BlockSpec((LANES, tm), lambda i, j: (0, i))],
        out_shape=[jax.ShapeDtypeStruct((m, n), BF16), jax.ShapeDtypeStruct((m, LANES), F32),
                   jax.ShapeDtypeStruct((LANES, m), F32)],
        compiler_params=_params(("parallel", "arbitrary")), name="in_proj",
    )(a, w_t, wg_t)


def _tri_masks(length, reverse):
    row = lax.broadcasted_iota(jnp.int32, (length, length), 0)
    col = lax.broadcasted_iota(jnp.int32, (length, length), 1)
    if reverse:
        return col >= row, row >= col
    return col <= row, row <= col


def _mlstm_kernel(q_ref, k_ref, v_ref, gc_ref, gr_ref, bc_ref, br_ref, h_ref, ct_ref, n_ref, m_ref, *,
                  reverse, heads, dqk, dv):
    @pl.when(pl.program_id(1) == 0)
    def _():
        ct_ref[...] = jnp.zeros_like(ct_ref)
        n_ref[...] = jnp.zeros_like(n_ref)
        m_ref[...] = jnp.zeros_like(m_ref)

    length = q_ref.shape[0]
    valid, valid_t = _tri_masks(length, reverse)
    gc = gc_ref[...] + bc_ref[...]
    gr = gr_ref[...] + br_ref[...]
    log_f_c = _log_sigmoid(gc)
    log_f_r = _log_sigmoid(gr)
    d0 = 2 * heads if reverse else 0
    scale = dqk ** -0.5
    for h in range(heads):
        i_col = gc[:, d0 + h:d0 + h + 1]
        f_col = log_f_c[:, d0 + heads + h:d0 + heads + h + 1]
        i_row = gr[d0 + h:d0 + h + 1, :]
        f_row = log_f_r[d0 + heads + h:d0 + heads + h + 1, :]
        b_col = jnp.sum(jnp.where(valid, f_row, 0.0), axis=1, keepdims=True)
        b_row = jnp.sum(jnp.where(valid_t, f_col, 0.0), axis=0, keepdims=True)
        b_end = jnp.sum(f_row, axis=1, keepdims=True)
        m_prev = m_ref[h][:, :1]
        log_d = jnp.where(valid, b_col - b_row + i_row, NEG)
        m_inter = b_col + m_prev
        m_t = jnp.maximum(m_inter, jnp.max(log_d, axis=1, keepdims=True))
        w_intra = jnp.exp(log_d - m_t) * scale
        w_inter = jnp.exp(m_inter - m_t) * scale
        q = q_ref[:, h * dqk:(h + 1) * dqk]
        k = k_ref[:, h * dqk:(h + 1) * dqk]
        v = v_ref[:, h * dv:(h + 1) * dv]
        s = _dot_nt(q, k) * w_intra
        num = _dot(s.astype(BF16), v) + w_inter * _dot(q, ct_ref[h].astype(BF16))
        qn = jnp.sum(q.astype(F32) * n_ref[h], axis=1, keepdims=True)
        den = jnp.sum(s, axis=1, keepdims=True) + w_inter * qn
        h_ref[:, h * dv:(h + 1) * dv] = (num / jnp.maximum(jnp.abs(den), jnp.exp(-m_t))).astype(h_ref.dtype)
        log_w = b_end - b_col + i_col
        m_new = jnp.maximum(b_end + m_prev, jnp.max(log_w, axis=0, keepdims=True))
        w_end = jnp.exp(log_w - m_new)
        decay = jnp.exp(b_end + m_prev - m_new)
        wv = (w_end * v.astype(F32)).astype(BF16)
        ct_ref[h] = decay * ct_ref[h] + _dot_tn(k, wv)
        n_ref[h] = decay * n_ref[h] + jnp.sum(w_end * k.astype(F32), axis=0, keepdims=True)
        m_ref[h] = jnp.broadcast_to(m_new, (1, LANES))


def _chunk_index(i, n_chunks, reverse):
    if not reverse:
        return i
    return jnp.where(i == 0, 0, n_chunks - i)


def _mlstm(p, col0, dv, gates_c, gates_r, bias_c, bias_r, batch, seq_rows, reverse):
    t = p.shape[0]
    length = ROW_TILE
    n_chunks = seq_rows // length
    heads = A_HEADS
    dqk = dv // 2
    qw, vw = heads * dqk, heads * dv

    def rows(b, i):
        return b * n_chunks + _chunk_index(i, n_chunks, reverse)

    kern = functools.partial(_mlstm_kernel, reverse=reverse, heads=heads, dqk=dqk, dv=dv)
    return pl.pallas_call(
        kern,
        grid=(batch, n_chunks),
        in_specs=[
            pl.BlockSpec((length, qw), lambda b, i: (rows(b, i), col0 // qw)),
            pl.BlockSpec((length, qw), lambda b, i: (rows(b, i), col0 // qw + 1)),
            pl.BlockSpec((length, vw), lambda b, i: (rows(b, i), (col0 + 2 * qw) // vw)),
            pl.BlockSpec((length, LANES), lambda b, i: (rows(b, i), 0)),
            pl.BlockSpec((4 * heads, length), lambda b, i: (0, rows(b, i))),
            pl.BlockSpec((1, LANES), lambda b, i: (0, 0)),
            pl.BlockSpec((4 * heads, 1), lambda b, i: (0, 0)),
        ],
        out_specs=pl.BlockSpec((length, vw), lambda b, i: (rows(b, i), 0)),
        out_shape=jax.ShapeDtypeStruct((t, vw), BF16),
        scratch_shapes=[pltpu.VMEM((heads, dqk, dv), F32), pltpu.VMEM((heads, 1, dqk), F32),
                        pltpu.VMEM((heads, 1, LANES), F32)],
        compiler_params=_params(("parallel", "arbitrary")), name="mlstm_scan",
    )(p, p, p, gates_c, gates_r, bias_c, bias_r)


def _rope(t, cosf, sinf, half):
    if 2 * half == LANES:
        partner = pltpu.roll(t, half, axis=1)
    else:
        lane = lax.broadcasted_iota(jnp.int32, t.shape, 1)
        partner = jnp.where((lane & (2 * half - 1)) < half, pltpu.roll(t, LANES - half, axis=1),
                            pltpu.roll(t, half, axis=1))
    return t * cosf + partner * sinf


def _retention_kernel(q_ref, k_ref, v_ref, cos_ref, sin_ref, dp_ref, o_ref, s_ref, *, reverse, heads, dk, dv):
    @pl.when(pl.program_id(1) == 0)
    def _():
        s_ref[...] = jnp.zeros_like(s_ref)

    length = q_ref.shape[0]
    row = lax.broadcasted_iota(jnp.int32, (length, length), 0)
    col = lax.broadcasted_iota(jnp.int32, (length, length), 1)
    diff = (col - row if reverse else row - col).astype(F32)
    tpos = lax.broadcasted_iota(jnp.int32, (length, 1), 0).astype(F32)
    lg_all = -_softplus(dp_ref[...])
    cosf = cos_ref[...]
    sinf = sin_ref[...]
    scale = dk ** -0.5
    d = 1 if reverse else 0
    for h in range(heads):
        lg = lg_all[d:d + 1, h:h + 1]
        decay_mat = jnp.where(diff >= 0.0, jnp.exp(lg * jnp.maximum(diff, 0.0)), 0.0) * scale
        if reverse:
            read_w = jnp.exp(lg * (length - tpos))
            write_w = jnp.exp(lg * tpos)
        else:
            read_w = jnp.exp(lg * (tpos + 1.0))
            write_w = jnp.exp(lg * (length - 1.0 - tpos))
        chunk_decay = jnp.exp(lg * float(length))
        q = _rope(q_ref[:, h * dk:(h + 1) * dk].astype(F32), cosf, sinf, dk // 2).astype(BF16)
        k = _rope(k_ref[:, h * dk:(h + 1) * dk].astype(F32), cosf, sinf, dk // 2).astype(BF16)
        v = v_ref[:, h * dv:(h + 1) * dv]
        s = _dot_nt(q, k) * decay_mat
        o_ref[:, h * dv:(h + 1) * dv] = (_dot(s.astype(BF16), v)
                                         + read_w * _dot(q, s_ref[h].astype(BF16))).astype(o_ref.dtype)
        wv = (write_w * v.astype(F32)).astype(BF16)
        s_ref[h] = chunk_decay * s_ref[h] + scale * _dot_tn(k, wv)


def _retention(p, col0, dv, cosf, sinf, decay_p, batch, seq_rows, reverse):
    t = p.shape[0]
    length = ROW_TILE
    n_chunks = seq_rows // length
    heads = B_HEADS
    dk = dv // 2
    assert dk == LANES
    qw, vw = heads * dk, heads * dv

    def rows(b, i):
        return b * n_chunks + _chunk_index(i, n_chunks, reverse)

    kern = functools.partial(_retention_kernel, reverse=reverse, heads=heads, dk=dk, dv=dv)
    return pl.pallas_call(
        kern,
        grid=(batch, n_chunks),
        in_specs=[
            pl.BlockSpec((length, qw), lambda b, i: (rows(b, i), col0 // qw)),
            pl.BlockSpec((length, qw), lambda b, i: (rows(b, i), col0 // qw + 1)),
            pl.BlockSpec((length, vw), lambda b, i: (rows(b, i), (col0 + 2 * qw) // vw)),
            pl.BlockSpec((length, LANES), lambda b, i: (_chunk_index(i, n_chunks, reverse), 0)),
            pl.BlockSpec((length, LANES), lambda b, i: (_chunk_index(i, n_chunks, reverse), 0)),
            pl.BlockSpec(decay_p.shape, lambda b, i: (0, 0)),
        ],
        out_specs=pl.BlockSpec((length, vw), lambda b, i: (rows(b, i), 0)),
        out_shape=jax.ShapeDtypeStruct((t, vw), BF16),
        scratch_shapes=[pltpu.VMEM((heads, dk, dv), F32)],
        compiler_params=_params(("parallel", "arbitrary")), name="retention_scan",
    )(p, p, p, cosf, sinf, decay_p)


def _headnorm_kernel(hf_ref, hb_ref, gate_ref, gain_ref, y_ref, *, heads, dv, center, silu_gate):
    g = gate_ref[...].astype(F32)
    sg = _sigmoid(g)
    gate = g * sg if silu_gate else sg
    gain = gain_ref[...]
    for h in range(heads):
        sl = slice(h * dv, (h + 1) * dv)
        x = hf_ref[:, sl].astype(F32) + hb_ref[:, sl].astype(F32)
        if center:
            x = x - jnp.mean(x, axis=-1, keepdims=True)
        y = x * lax.rsqrt(jnp.mean(x * x, axis=-1, keepdims=True) + EPS)
        y_ref[:, sl] = (gate[:, sl] * (y * gain[:, sl])).astype(y_ref.dtype)


def _headnorm(hf, hb, p, gate_col, gain, heads, center, silu_gate):
    t, w = hf.shape
    tm = _row_tile(t, 2 * ROW_TILE)
    kern = functools.partial(_headnorm_kernel, heads=heads, dv=w // heads, center=center, silu_gate=silu_gate)
    row_spec = pl.BlockSpec((tm, w), lambda i: (i, 0))
    return pl.pallas_call(
        kern, grid=(t // tm,),
        in_specs=[row_spec, row_spec, pl.BlockSpec((tm, w), lambda i: (i, gate_col // w)),
                  pl.BlockSpec((1, w), lambda i: (0, 0))],
        out_specs=row_spec,
        out_shape=jax.ShapeDtypeStruct((t, w), BF16),
        compiler_params=_params(("parallel",)), name="headnorm_gate",
    )(hf, hb, p, gain.reshape(1, w))


def _lane_low(shape, hd):
    return lax.broadcasted_iota(jnp.int32, shape, 1) < hd


def _stack_heads(q, hd):
    low = _lane_low(q.shape, hd)
    zero = jnp.zeros_like(q)
    return jnp.concatenate([jnp.where(low, q, zero), jnp.where(low, zero, q)], axis=0)


def _unstack_heads(o, hd):
    rows = o.shape[0] // 2
    return jnp.where(_lane_low((rows, LANES), hd), o[:rows], o[rows:])


def _window_kernel(sink_ref, q_ref, kp_ref, k0_ref, kn_ref, vp_ref, v0_ref, vn_ref, kc_ref, vc_ref,
                   cq_ref, sq_ref, cp_ref, sp_ref, cn_ref, sn_ref, o_ref, *, heads, kv_heads, hd, ctx_blocks,
                   n_latent, slabs_per_dot):
    i = pl.program_id(1)
    j = i - ctx_blocks
    w = q_ref.shape[0]
    half = hd // 2
    slabs_per_group = heads // kv_heads // 2

    def roped(x_ref, c_ref, s_ref):
        return _rope(x_ref[...].astype(F32), c_ref[...], s_ref[...], half)

    kb = jnp.concatenate([roped(kp_ref, cp_ref, sp_ref), roped(k0_ref, cq_ref, sq_ref),
                          roped(kn_ref, cn_ref, sn_ref)], axis=0)
    vb = jnp.concatenate([vp_ref[...], v0_ref[...], vn_ref[...]], axis=0).astype(F32)
    kc = kc_ref[...].astype(F32)
    vc = vc_ref[...].astype(F32)

    def group_copy(x, g):
        keep = _lane_low(x.shape, hd) if g == 0 else jnp.logical_not(_lane_low(x.shape, hd))
        return jnp.where(keep, x, pltpu.roll(x, hd, axis=1)).astype(BF16)

    c = lax.broadcasted_iota(jnp.int32, (3 * w, w), 0)
    t = lax.broadcasted_iota(jnp.int32, (3 * w, w), 1)
    lower = jnp.maximum(t + (w - WINDOW), (1 - j) * w)
    upper = jnp.minimum(t + (w + WINDOW), n_latent - 1 - (j - 1) * w)
    upper = jnp.where(j < 0, -1, upper)
    mask_bias = jnp.where((c >= lower) & (c <= upper), 0.0, NEG)
    mask_bias = jnp.concatenate([mask_bias] * (2 * slabs_per_dot), axis=1)
    scale = hd ** -0.5
    cq = cq_ref[...]
    sq = sq_ref[...]
    for g in range(kv_heads):
        kk, vv, kkc, vvc = group_copy(kb, g), group_copy(vb, g), group_copy(kc, g), group_copy(vc, g)
        for s0 in range(0, slabs_per_group, slabs_per_dot):
            slabs = [g * slabs_per_group + s0 + s for s in range(slabs_per_dot)]
            qs = []
            sinks = []
            for slab in slabs:
                qr = _rope(q_ref[:, slab * LANES:(slab + 1) * LANES].astype(F32), cq, sq, half)
                qs.append(_stack_heads((qr * scale).astype(BF16), hd))
                sinks += [jnp.full((1, w), sink_ref[2 * slab], F32), jnp.full((1, w), sink_ref[2 * slab + 1], F32)]
            qq = jnp.concatenate(qs, axis=0)
            sink = jnp.concatenate(sinks, axis=1)
            s_loc = _dot_nt(kk, qq) + mask_bias
            s_ctx = _dot_nt(kkc, qq)
            m = jnp.maximum(jnp.maximum(jnp.max(s_loc, axis=0, keepdims=True),
                                        jnp.max(s_ctx, axis=0, keepdims=True)), sink)
            e_loc = jnp.exp(s_loc - m)
            e_ctx = jnp.exp(s_ctx - m)
            den = (jnp.exp(sink - m) + jnp.sum(e_loc, axis=0, keepdims=True)
                   + jnp.sum(e_ctx, axis=0, keepdims=True))
            o_t = (_dot_tn(vv, e_loc.astype(BF16)) + _dot_tn(vvc, e_ctx.astype(BF16))) / den
            o = o_t.T
            for n, slab in enumerate(slabs):
                o_ref[:, slab * LANES:(slab + 1) * LANES] = _unstack_heads(
                    o[2 * n * w:2 * (n + 1) * w], hd).astype(o_ref.dtype)


def _window_attention(p, col0, cosf, sinf, sink, batch, seq_rows, ctx_rows):
    t = p.shape[0]
    w = WINDOW
    heads, kv_heads = C_HEADS, C_KV_HEADS
    hd = LANES // 2
    assert kv_heads * hd == LANES
    qw = heads * hd
    nblk = seq_rows // w
    ctx_blocks = ctx_rows // w
    kcol = (col0 + qw) // LANES
    vcol = kcol + 1

    def cur(b, i):
        return b * nblk + i

    def prev(b, i):
        return b * nblk + jnp.maximum(i - 1, 0)

    def nxt(b, i):
        return b * nblk + jnp.minimum(i + 1, nblk - 1)

    def kv_spec(fn, colblk):
        return pl.BlockSpec((w, LANES), lambda b, i: (fn(b, i), colblk))

    def tab_spec(fn):
        return pl.BlockSpec((w, LANES), lambda b, i: (fn(0, i), 0))

    kern = functools.partial(_window_kernel, heads=heads, kv_heads=kv_heads, hd=hd, ctx_blocks=ctx_blocks,
                             n_latent=seq_rows - ctx_rows, slabs_per_dot=4)
    ctx_spec_k = pl.BlockSpec((ctx_rows, LANES), lambda b, i: (b * (seq_rows // ctx_rows), kcol))
    ctx_spec_v = pl.BlockSpec((ctx_rows, LANES), lambda b, i: (b * (seq_rows // ctx_rows), vcol))
    return pl.pallas_call(
        kern,
        grid=(batch, nblk),
        in_specs=[
            pl.BlockSpec(memory_space=pltpu.SMEM),
            pl.BlockSpec((w, qw), lambda b, i: (cur(b, i), col0 // qw)),
            kv_spec(prev, kcol), kv_spec(cur, kcol), kv_spec(nxt, kcol),
            kv_spec(prev, vcol), kv_spec(cur, vcol), kv_spec(nxt, vcol),
            ctx_spec_k, ctx_spec_v,
            tab_spec(cur), tab_spec(cur), tab_spec(prev), tab_spec(prev), tab_spec(nxt), tab_spec(nxt),
        ],
        out_specs=pl.BlockSpec((w, qw), lambda b, i: (cur(b, i), 0)),
        out_shape=jax.ShapeDtypeStruct((t, qw), BF16),
        compiler_params=_params(("parallel", "arbitrary")), name="window_attention",
    )(sink, p, p, p, p, p, p, p, p, p, cosf, sinf, cosf, sinf, cosf, sinf)


def _nbr_layout():
    masked = 2 * NA_ROWS - 1
    variants = [(0, [0] * NBR_ROWS), (NA_ROWS // 2, list(range(NBR_ROWS))),
                (NBR_UNION - NBR_ROWS, [NBR_UNION - NA_ROWS] * NBR_ROWS)]
    pairs, index = [], []
    for delta, rel_start in variants:
        per_kr = []
        for kr in range(NBR_UNION):
            codes = [kr - (delta + rr) + NA_ROWS - 1 if rel_start[rr] <= kr < rel_start[rr] + NA_ROWS else masked
                     for rr in range(NBR_ROWS)]
            row = []
            for a in range(NBR_ROWS // 2):
                pair = (codes[2 * a], codes[2 * a + 1])
                if pair not in pairs:
                    pairs.append(pair)
                row.append(pairs.index(pair))
            per_kr.append(row)
        index.append(per_kr)
    return pairs, index


def _nbr_kernel(q_ref, k_ref, v_ref, tab_ref, o_ref, bias_ref, *, hd, ctx_rows, grid_rows, index, slabs_per_step):
    i = pl.program_id(2)
    scale = hd ** -0.5
    tq = q_ref.shape[0]
    lanes = [slice(sl * LANES, (sl + 1) * LANES) for sl in range(slabs_per_step)]
    qqs = [_stack_heads(q_ref[:, ln] * scale, hd) for ln in lanes]
    kcs = [k_ref[0:ctx_rows, ln] for ln in lanes]
    vcs = [v_ref[0:ctx_rows, ln] for ln in lanes]
    s_ctxs = [_dot_nt(kc, qq) for kc, qq in zip(kcs, qqs)]
    m_ctxs = [jnp.max(s, axis=0, keepdims=True) for s in s_ctxs]

    @pl.when(i == 0)
    def _():
        for sl, ln in enumerate(lanes):
            for variant, per_kr in enumerate(index):
                for kr, row in enumerate(per_kr):
                    for sub in range(2):
                        for a, u in enumerate(row):
                            bias_ref[sl, variant, kr * GRID_W:(kr + 1) * GRID_W,
                                     sub * tq + a * LANES:sub * tq + (a + 1) * LANES] = tab_ref[2 * sl + sub, u]
            e = jnp.exp(s_ctxs[sl] - m_ctxs[sl])
            o_t = _dot_tn(vcs[sl], e.astype(BF16)) / jnp.sum(e, axis=0, keepdims=True)
            o_ref[:, ln] = _unstack_heads(o_t.T, hd).astype(o_ref.dtype)

    @pl.when(i > 0)
    def _():
        blk = i - 1
        n_blk = grid_rows // NBR_ROWS
        u0 = jnp.clip(blk * NBR_ROWS - NA_ROWS // 2, 0, grid_rows - NBR_UNION)
        variant = jnp.where(blk == 0, 0, jnp.where(blk == n_blk - 1, 2, 1))
        base = pl.multiple_of(ctx_rows + u0 * GRID_W, GRID_W)
        for sl, ln in enumerate(lanes):
            kw = k_ref[pl.ds(base, NBR_UNION * GRID_W), ln]
            vw = v_ref[pl.ds(base, NBR_UNION * GRID_W), ln]
            s_loc = _dot_nt(kw, qqs[sl]) + bias_ref[sl, variant]
            m = jnp.maximum(jnp.max(s_loc, axis=0, keepdims=True), m_ctxs[sl])
            e_loc = jnp.exp(s_loc - m)
            e_ctx = jnp.exp(s_ctxs[sl] - m)
            den = jnp.sum(e_loc, axis=0, keepdims=True) + jnp.sum(e_ctx, axis=0, keepdims=True)
            o_t = (_dot_tn(vw, e_loc.astype(BF16)) + _dot_tn(vcs[sl], e_ctx.astype(BF16))) / den
            o_ref[:, ln] = _unstack_heads(o_t.T, hd).astype(o_ref.dtype)


def _nbr_table(na_bias_l, pairs):
    col = jnp.arange(GRID_W)
    col_start = jnp.clip(col - NA_COLS // 2, 0, GRID_W - NA_COLS)
    col_valid = (col[None, :] >= col_start[:, None]) & (col[None, :] < col_start[:, None] + NA_COLS)
    dc = jnp.clip(col[None, :] - col[:, None], -(NA_COLS - 1), NA_COLS - 1) + (NA_COLS - 1)
    blocks = jnp.where(col_valid.T[None, None], na_bias_l[:, :, dc.T], NEG)
    blocks = jnp.concatenate([blocks, jnp.full_like(blocks[:, :1], NEG)], axis=1)
    left = blocks[:, np.array([p[0] for p in pairs])]
    right = blocks[:, np.array([p[1] for p in pairs])]
    return jnp.concatenate([left, right], axis=-1).astype(F32)


def _nbr_attention(p, col0, na_bias_l, batch, seq_rows, ctx_rows):
    t = p.shape[0]
    heads = D_HEADS
    hd = LANES // 2
    qw = heads * hd
    slabs = heads // 2
    tq = NBR_ROWS * GRID_W
    n_latent = seq_rows - ctx_rows
    grid_rows = n_latent // GRID_W
    assert tq == ctx_rows and grid_rows % NBR_ROWS == 0 and grid_rows >= NBR_UNION + 1
    assert slabs % NBR_SLABS_PER_STEP == 0 and (col0 // LANES) % NBR_SLABS_PER_STEP == 0
    steps = seq_rows // tq
    q0 = col0 // LANES
    pairs, index = _nbr_layout()
    table = _nbr_table(na_bias_l, pairs)
    sps = NBR_SLABS_PER_STEP
    width = sps * LANES
    kern = functools.partial(_nbr_kernel, hd=hd, ctx_rows=ctx_rows, grid_rows=grid_rows, index=index,
                             slabs_per_step=sps)
    return pl.pallas_call(
        kern,
        grid=(batch, slabs // sps, steps),
        in_specs=[
            pl.BlockSpec((tq, width), lambda b, s, i: (b * steps + i, q0 // sps + s)),
            pl.BlockSpec((seq_rows, width), lambda b, s, i: (b, (q0 + slabs) // sps + s)),
            pl.BlockSpec((seq_rows, width), lambda b, s, i: (b, (q0 + 2 * slabs) // sps + s)),
            pl.BlockSpec((2 * sps,) + table.shape[1:], lambda b, s, i: (s, 0, 0, 0)),
        ],
        out_specs=pl.BlockSpec((tq, width), lambda b, s, i: (b * steps + i, s)),
        out_shape=jax.ShapeDtypeStruct((t, qw), BF16),
        scratch_shapes=[pltpu.VMEM((sps, len(index), NBR_UNION * GRID_W, 2 * tq), F32)],
        compiler_params=_params(("parallel", "parallel", "arbitrary")), name="nbr_attention",
    )(p, p, p, table)


def _merge_kernel(*refs):
    ys = refs[0:N_BRANCH]
    gs = refs[N_BRANCH:2 * N_BRANCH]
    ws = refs[2 * N_BRANCH:3 * N_BRANCH]
    o_ref = refs[3 * N_BRANCH]
    acc = None
    for y_ref, g_ref, w_ref in zip(ys, gs, ws):
        term = _sigmoid(g_ref[...].astype(F32)) * _dot(y_ref[...], w_ref[...])
        acc = term if acc is None else acc + term
    o_ref[...] = acc.astype(o_ref.dtype)


def _merge(ys, p, gate_col0, w_branch, row0, tm, tn):
    t, bw = ys[0].shape
    d = w_branch.shape[1]
    nt = d // tn
    y_spec = pl.BlockSpec((tm, bw), lambda i, j: (i, 0))
    g_specs = [pl.BlockSpec((tm, tn), functools.partial(lambda i, j, br: (i, (gate_col0 + br * d) // tn + j), br=br))
               for br in range(N_BRANCH)]
    w_specs = [pl.BlockSpec((bw, tn), functools.partial(lambda i, j, br: (row0 // bw + br, j), br=br))
               for br in range(N_BRANCH)]
    return pl.pallas_call(
        _merge_kernel,
        grid=(t // tm, nt),
        in_specs=[y_spec] * N_BRANCH + g_specs + w_specs,
        out_specs=pl.BlockSpec((tm, tn), lambda i, j: (i, j)),
        out_shape=jax.ShapeDtypeStruct((t, d), BF16),
        compiler_params=_params(("parallel", "arbitrary")), name="merge_branches",
    )(*ys, *([p] * N_BRANCH), *([w_branch] * N_BRANCH))


def _moe_kernel(t_ref, gates_ref, w1_ref, b1_ref, w2_ref, b2_ref, o_ref, acc_ref, act_ref, *, ff, pairs):
    e = pl.program_id(1)
    last = pl.num_programs(1) - 1
    gates = gates_ref[...]

    def activations():
        tokens = t_ref[...]
        lane = lax.broadcasted_iota(jnp.int32, gates.shape, 1)
        acts = []
        for pr in range(pairs):
            hid = _dot(tokens, w1_ref[pr]) + b1_ref[pr]
            g_h = jnp.minimum(hid[:, :2 * ff], SWIGLU_LIMIT)
            u_h = jnp.clip(hid[:, 2 * ff:], -SWIGLU_LIMIT, SWIGLU_LIMIT)
            cols = []
            for s in range(2):
                expert = (e * pairs + pr) * 2 + s
                gsel = jnp.sum(jnp.where(lane == expert, gates, 0.0), axis=-1, keepdims=True)
                cols.append(jnp.broadcast_to(gsel, (gates.shape[0], ff)))
            act = g_h * _sigmoid(SWIGLU_ALPHA * g_h) * (u_h + 1.0) * jnp.concatenate(cols, axis=-1)
            acts.append(act.astype(BF16))
        return jnp.concatenate(acts, axis=-1)

    @pl.when(e == 0)
    def _():
        acc_ref[...] = _dot(gates.astype(BF16), b2_ref[...])
        act_ref[...] = activations()

    @pl.when((e > 0) & (e < last))
    def _():
        down = _dot(act_ref[...], w2_ref[...])
        act_new = activations()
        acc_ref[...] += down
        act_ref[...] = act_new

    @pl.when(e == last)
    def _():
        o_ref[...] = (acc_ref[...] + _dot(act_ref[...], w2_ref[...])).astype(o_ref.dtype)


def _moe(h, gates, w1p, b1p, w2, b2p, pair0, tm, pairs):
    t, d = h.shape
    width = w1p.shape[2]
    ff = width // 4
    groups = b1p.shape[0] // pairs
    g0 = pair0 // pairs
    kern = functools.partial(_moe_kernel, ff=ff, pairs=pairs)
    return pl.pallas_call(
        kern,
        grid=(t // tm, groups + 1),
        in_specs=[
            pl.BlockSpec((tm, d), lambda i, e: (i, 0)),
            pl.BlockSpec((tm, LANES), lambda i, e: (i, 0)),
            pl.BlockSpec((pairs, d, width), lambda i, e: (g0 + jnp.minimum(e, groups - 1), 0, 0)),
            pl.BlockSpec((pairs, 1, width), lambda i, e: (jnp.minimum(e, groups - 1), 0, 0)),
            pl.BlockSpec((pairs * 2 * ff, d), lambda i, e: (g0 + jnp.maximum(e - 1, 0), 0)),
            pl.BlockSpec(b2p.shape, lambda i, e: (0, 0)),
        ],
        out_specs=pl.BlockSpec((tm, d), lambda i, e: (i, 0)),
        out_shape=jax.ShapeDtypeStruct((t, d), BF16),
        scratch_shapes=[pltpu.VMEM((tm, d), F32), pltpu.VMEM((tm, pairs * 2 * ff), BF16)],
        compiler_params=_params(("parallel", "arbitrary")), name="moe_experts",
    )(h, gates, w1p, b1p, w2, b2p)


def _rope_tables(n_latent, ctx_rows, hd):
    pos = np.arange(n_latent)
    row = (pos // GRID_W).astype(np.float32)
    col = (pos % GRID_W).astype(np.float32)
    n_freq = hd // 4
    inv = jnp.asarray(ROPE_BASE, F32) ** (-jnp.arange(n_freq, dtype=F32) / n_freq)
    ang = jnp.concatenate([jnp.asarray(row)[:, None] * inv, jnp.asarray(col)[:, None] * inv], axis=-1)
    ang = jnp.concatenate([jnp.zeros((ctx_rows, hd // 2), F32), ang], axis=0)
    cos, sin = jnp.cos(ang), jnp.sin(ang)
    reps = LANES // hd
    cosf = jnp.tile(jnp.concatenate([cos, cos], axis=-1), (1, reps))
    sinf = jnp.tile(jnp.concatenate([-sin, sin], axis=-1), (1, reps))
    return cosf, sinf


def _pad_lanes(a, value=0.0):
    pad = LANES - a.shape[-1]
    return jnp.pad(a, [(0, 0)] * (a.ndim - 1) + [(0, pad)], constant_values=value)


def kernel(x, c, ctx, c_ctx, ada_down, ada_up, ada_bias, norm_gain, w_in, mlstm_gate_bias, mlstm_norm_gain,
           ret_decay, ret_norm_gain, sink, na_bias, w_branch, w_out, router_w, router_b, moe_w1, moe_b1,
           moe_w2, moe_b2, final_gain):
    batch, n_latent, d = x.shape
    ctx_rows = ctx.shape[1]
    depth = w_in.shape[0]
    seq_rows = ctx_rows + n_latent
    t = batch * seq_rows
    bw = d // N_BRANCH
    n_experts = router_w.shape[2]
    ff = moe_w2.shape[2]

    a_dv, b_dv = bw // A_HEADS, bw // B_HEADS
    c_hd, d_hd = bw // C_HEADS, bw // D_HEADS
    assert c_hd == LANES // 2 and d_hd == LANES // 2
    a_sz = (bw // 2, bw // 2, bw, bw)
    n_gate = 4 * A_HEADS
    b_sz = (bw // 2, bw // 2, bw, bw)
    c_sz = (bw, C_KV_HEADS * c_hd, C_KV_HEADS * c_hd)
    d_sz = (bw, bw, bw)
    sizes = a_sz + (n_gate,) + b_sz + c_sz + d_sz + (N_BRANCH * d,)
    offs = [int(o) for o in np.concatenate([[0], np.cumsum(sizes)])]
    a0, g0, b0, c0, d0, bg0, end = offs[0], offs[4], offs[5], offs[9], offs[12], offs[15], offs[16]
    col_gate = 0
    col_a = N_BRANCH * d
    col_b = col_a + sum(a_sz)
    col_d = col_b + sum(b_sz)
    col_c = col_d + sum(d_sz)

    w_cat_t, w_g_t = _win_prep(jnp.swapaxes(w_in, 1, 2), [(bg0, end), (a0, g0), (b0, c0), (d0, bg0), (c0, d0)],
                               (g0, b0))
    w1p = _moe_w1_prep(moe_w1)
    wb_bf, wo_bf, w2_bf = _cast_bf16(w_branch.reshape(depth * N_BRANCH * bw, d), w_out.reshape(depth * d, d),
                                     moe_w2.reshape(depth * n_experts * ff, d))
    b1p_all = jnp.concatenate([moe_b1[..., 0::2].reshape(depth, n_experts // 2, 1, 2 * ff),
                               moe_b1[..., 1::2].reshape(depth, n_experts // 2, 1, 2 * ff)], axis=-1)
    b2p_all = jnp.pad(moe_b2, ((0, 0), (0, LANES - n_experts), (0, 0))).astype(BF16)

    mod_all = _ada_modulation(jnp.pad(jnp.concatenate([c_ctx[None], c], axis=0), ((0, 8 - 1 - batch), (0, 0))),
                              ada_down, ada_up, ada_bias)
    cos_b, sin_b = _rope_tables(n_latent, ctx_rows, b_dv // 2)
    cos_c, sin_c = _rope_tables(n_latent, ctx_rows, c_hd)
    decay_p = jnp.pad(ret_decay, ((0, 0), (0, 8 - ret_decay.shape[1]), (0, LANES - ret_decay.shape[2])))

    tm_proj = _row_tile(t, 768)
    tm_moe = _row_tile(t, 512)
    z = (ctx.reshape(batch * ctx_rows, d), x.reshape(batch * n_latent, d))
    delta = None
    for l in range(depth):
        mod = mod_all[l]
        if l == 0:
            h = _resid_norm(z, norm_gain[l, 0], mod, 0, 1, batch, seq_rows, ctx_rows)[0]
        else:
            z, h = _resid_norm(z, norm_gain[l, 0], mod, 0, 1, batch, seq_rows, ctx_rows, delta=delta,
                               gate_mod=mod_all[l - 1], gate_slot=5)
        p, gates_c, gates_r = _in_proj(h, w_cat_t, w_g_t, l, tm=tm_proj, tn=1280)
        gates_r = gates_r[:n_gate]
        bias_c = _pad_lanes(mlstm_gate_bias[l][None])
        bias_r = mlstm_gate_bias[l][:, None]

        hf = _mlstm(p, col_a, a_dv, gates_c, gates_r, bias_c, bias_r, batch, seq_rows, False)
        hb = _mlstm(p, col_a, a_dv, gates_c, gates_r, bias_c, bias_r, batch, seq_rows, True)
        y_a = _headnorm(hf, hb, p, col_a + 2 * bw, mlstm_norm_gain[l], A_HEADS, False, False)
        of = _retention(p, col_b, b_dv, cos_b, sin_b, decay_p[l], batch, seq_rows, False)
        ob = _retention(p, col_b, b_dv, cos_b, sin_b, decay_p[l], batch, seq_rows, True)
        y_b = _headnorm(of, ob, p, col_b + 2 * bw, ret_norm_gain[l], B_HEADS, True, True)
        y_c = _window_attention(p, col_c, cos_c, sin_c, sink[l], batch, seq_rows, ctx_rows)
        y_d = _nbr_attention(p, col_d, na_bias[l], batch, seq_rows, ctx_rows)

        acc = _merge([y_a, y_b, y_c, y_d], p, col_gate, wb_bf, l * N_BRANCH * bw, tm=tm_moe, tn=1024)
        delta1 = _matmul(acc, wo_bf.reshape(depth, d, d), l, BF16, tm=tm_proj, tn=1024)

        rw = _pad_lanes(router_w[l]).astype(BF16)
        rb = _pad_lanes(router_b[l][None], NEG)
        z, h2, gates = _resid_norm(z, norm_gain[l, 1], mod, 3, 4, batch, seq_rows, ctx_rows, delta=delta1,
                                   gate_mod=mod, gate_slot=2, router=(rw, rb))
        delta = _moe(h2, gates, w1p, b1p_all[l], w2_bf, b2p_all[l], l * (n_experts // 2), tm=tm_moe, pairs=2)

    out = _final_norm(z, delta, mod_all[depth - 1], 5, final_gain, batch, seq_rows, ctx_rows)
    return out.reshape(batch, n_latent, d)
```

```python
import functools

import jax
import jax.numpy as jnp
import numpy as np
from jax import lax
from jax.experimental import pallas as pl
from jax.experimental.pallas import tpu as pltpu

F32 = jnp.float32
BF16 = jnp.bfloat16

GRID_W = 64
EPS = 1e-6
NEG = -1e30
ROPE_BASE = 10000.0
N_BRANCH = 4
A_HEADS = 4
B_HEADS = 4
C_HEADS = 16
C_KV_HEADS = 2
D_HEADS = 16
WINDOW = 128
NA_ROWS = 8
NA_COLS = 16
TOP_K = 4
SWIGLU_LIMIT = 7.0
SWIGLU_ALPHA = 1.702

LANES = 128
VMEM_LIMIT = 56 * 1024 * 1024
ROW_TILE = 256
NBR_ROWS = 4
NBR_UNION = NBR_ROWS + NA_ROWS - 1
NBR_SLABS_PER_STEP = 2


def _row_tile(rows, target):
    best = ROW_TILE
    for tile in range(ROW_TILE, target + 1, ROW_TILE):
        if rows % tile == 0:
            best = tile
    assert rows % best == 0
    return best


def _params(sem, vmem=VMEM_LIMIT):
    return pltpu.CompilerParams(dimension_semantics=sem, vmem_limit_bytes=vmem)


def _dot(a, b):
    return jnp.dot(a, b, preferred_element_type=F32)


def _dot_nt(a, b):
    return lax.dot_general(a, b, (((1,), (1,)), ((), ())), preferred_element_type=F32)


def _dot_tn(a, b):
    return lax.dot_general(a, b, (((0,), (0,)), ((), ())), preferred_element_type=F32)


def _sigmoid(x):
    return 1.0 / (1.0 + jnp.exp(-x))


def _softplus(x):
    return jnp.maximum(x, 0.0) + jnp.log1p(jnp.exp(-jnp.abs(x)))


def _log_sigmoid(x):
    return -_softplus(-x)


PREP_ROWS = ROW_TILE
F32_SUBLANES = 8


def _segment_blocks(segments):
    assert all(a % F32_SUBLANES == 0 and (b - a) % PREP_ROWS == 0 for a, b in segments)
    return sum((b - a) // PREP_ROWS for a, b in segments)


def _segment_row(step, segments):
    s = jnp.int32(0)
    base = 0
    for a, b in segments:
        n = (b - a) // PREP_ROWS
        s = jnp.where((step >= base) & (step < base + n),
                      a // F32_SUBLANES + (step - base) * (PREP_ROWS // F32_SUBLANES), s)
        base += n
    return s * F32_SUBLANES


def _win_prep_kernel(w_ref, cat_ref):
    cat_ref[...] = w_ref[0].astype(BF16)


def _win_prep(w_in_t, layer, segments):
    d = w_in_t.shape[2]
    n_blk = _segment_blocks(segments)
    return pl.pallas_call(
        _win_prep_kernel, grid=(n_blk,),
        in_specs=[pl.BlockSpec((pl.Element(1), pl.Element(PREP_ROWS), pl.Element(d)),
                               lambda j: (layer, _segment_row(j, segments), 0))],
        out_specs=pl.BlockSpec((PREP_ROWS, d), lambda j: (j, 0)),
        out_shape=jax.ShapeDtypeStruct((n_blk * PREP_ROWS, d), BF16),
        compiler_params=_params(("parallel",)), name="win_prep",
    )(w_in_t)


def _gate_prep_kernel(g_ref, gt_ref):
    g = g_ref[0].astype(BF16)
    gt_ref[...] = jnp.concatenate([g, jnp.zeros((LANES - g.shape[0], g.shape[1]), BF16)], axis=0)


def _gate_prep(w_in_t, gate_seg):
    depth, _, d = w_in_t.shape
    ga, gb = gate_seg
    return pl.pallas_call(
        _gate_prep_kernel, grid=(depth,),
        in_specs=[pl.BlockSpec((pl.Element(1), pl.Element(gb - ga), pl.Element(d)), lambda l: (l, ga, 0))],
        out_specs=pl.BlockSpec((None, LANES, d), lambda l: (l, 0, 0)),
        out_shape=jax.ShapeDtypeStruct((depth, LANES, d), BF16),
        compiler_params=_params(("parallel",)), name="gate_prep",
    )(w_in_t)


def _moe_w1_prep_kernel(w_ref, perm_ref, o_ref, *, ff):
    parts = [_dot(w_ref[0, s].astype(BF16), perm_ref[...]) for s in range(2)]
    o_ref[0] = jnp.concatenate([parts[0][:, :ff], parts[1][:, :ff], parts[0][:, ff:], parts[1][:, ff:]],
                               axis=1).astype(BF16)


def _moe_w1_prep(moe_w1):
    depth, n_exp, d, two_ff = moe_w1.shape
    ff = two_ff // 2
    rows = 1024
    src = np.concatenate([np.arange(0, two_ff, 2), np.arange(1, two_ff, 2)])
    perm = jnp.asarray(np.arange(two_ff)[:, None] == src[None, :], BF16)
    kern = functools.partial(_moe_w1_prep_kernel, ff=ff)
    return pl.pallas_call(
        kern, grid=(depth * n_exp // 2, d // rows),
        in_specs=[pl.BlockSpec((1, 2, rows, two_ff), lambda e, i: (e, 0, i, 0)),
                  pl.BlockSpec((two_ff, two_ff), lambda e, i: (0, 0))],
        out_specs=pl.BlockSpec((1, rows, 2 * two_ff), lambda e, i: (e, i, 0)),
        out_shape=jax.ShapeDtypeStruct((depth * n_exp // 2, d, 2 * two_ff), BF16),
        compiler_params=_params(("parallel", "parallel")), name="moe_w1_prep",
    )(moe_w1.reshape(depth * n_exp // 2, 2, d, two_ff), perm)


def _cast_kernel(*refs):
    n = len(refs) // 2
    for src, dst in zip(refs[:n], refs[n:]):
        dst[...] = src[...].astype(dst.dtype)


def _cast_bf16(*arrays):
    rows, cols = arrays[0].shape
    tm = ROW_TILE
    spec = pl.BlockSpec((tm, cols), lambda i: (i, 0))
    return pl.pallas_call(
        _cast_kernel, grid=(rows // tm,),
        in_specs=[spec] * len(arrays), out_specs=[spec] * len(arrays),
        out_shape=[jax.ShapeDtypeStruct((rows, cols), BF16)] * len(arrays),
        compiler_params=_params(("parallel",)), name="cast_bf16",
    )(*arrays)


def _ada_kernel(cv_ref, down_ref, up_ref, bias_ref, out_ref):
    cv = cv_ref[...]
    a = cv * _sigmoid(cv)
    z = _dot(a.astype(BF16), down_ref[0].astype(BF16))
    out_ref[0] = _dot(z.astype(BF16), up_ref[0].astype(BF16)) + bias_ref[0]


def _ada_modulation(cvecs, ada_down, ada_up, ada_bias):
    depth, d, r = ada_down.shape
    rows = cvecs.shape[0]
    return pl.pallas_call(
        _ada_kernel,
        grid=(depth, 6),
        in_specs=[
            pl.BlockSpec((rows, d), lambda l, j: (0, 0)),
            pl.BlockSpec((1, d, r), lambda l, j: (l, 0, 0)),
            pl.BlockSpec((1, r, d), lambda l, j: (l, 0, j)),
            pl.BlockSpec((1, 1, d), lambda l, j: (l, 0, j)),
        ],
        out_specs=pl.BlockSpec((1, rows, d), lambda l, j: (l, 0, j)),
        out_shape=jax.ShapeDtypeStruct((depth, rows, 6 * d), F32),
        compiler_params=_params(("parallel", "arbitrary")),
        name="ada_modulation",
    )(cvecs, ada_down, ada_up, ada_bias.reshape(depth, 1, 6 * d))


def _rmsnorm(x, gain):
    return x * lax.rsqrt(jnp.mean(x * x, axis=-1, keepdims=True) + EPS) * gain


def _resid_norm_kernel(*refs, split_src, has_delta, has_router, tiles_per_seq, ctx_tiles):
    refs = list(refs)
    tile = pl.program_id(0)
    b = tile // tiles_per_seq
    j = tile - b * tiles_per_seq
    is_ctx = j < ctx_tiles
    row = jnp.where(is_ctx, 0, 1 + b)
    if split_src:
        c_ref = refs.pop(0)
        x_ref = refs.pop(0)
        z = jnp.where(is_ctx, c_ref[...], x_ref[...])
    else:
        z = refs.pop(0)[...]
    if has_delta:
        delta_ref = refs.pop(0)
        gate_ref = refs.pop(0)
    gain_ref = refs.pop(0)
    shift_ref = refs.pop(0)
    scale_ref = refs.pop(0)
    if has_router:
        rw_ref = refs.pop(0)
        rb_ref = refs.pop(0)
    if has_delta:
        znew_ref = refs.pop(0)
    h_ref = refs.pop(0)
    if has_router:
        gates_ref = refs.pop(0)

    if has_delta:
        z = z + gate_ref[pl.ds(row, 1), :] * delta_ref[...].astype(F32)
        znew_ref[...] = z
    y = _rmsnorm(z, gain_ref[...])
    h = y * (1.0 + scale_ref[pl.ds(row, 1), :]) + shift_ref[pl.ds(row, 1), :]
    hb = h.astype(BF16)
    h_ref[...] = hb
    if has_router:
        logits = _dot(hb, rw_ref[...]) + rb_ref[...]
        lane = lax.broadcasted_iota(jnp.int32, logits.shape, 1).astype(F32)
        gates = jnp.zeros_like(logits)
        den = jnp.zeros((logits.shape[0], 1), F32)
        top = None
        for _ in range(TOP_K):
            m = jnp.max(logits, axis=-1, keepdims=True)
            idx = jnp.min(jnp.where(logits == m, lane, float(LANES)), axis=-1, keepdims=True)
            sel = lane == idx
            if top is None:
                top = m
            e = jnp.exp(m - top)
            den = den + e
            gates = jnp.where(sel, e, gates)
            logits = jnp.where(sel, NEG * 2.0, logits)
        gates_ref[...] = gates / den


def _resid_norm(src, gain, mod, shift_slot, scale_slot, batch, seq_rows, ctx_rows, delta=None, gate_mod=None,
                gate_slot=None, router=None):
    split_src = isinstance(src, tuple)
    d = src[0].shape[1] if split_src else src.shape[1]
    t = batch * seq_rows
    tm = ROW_TILE
    tiles_per_seq = seq_rows // tm
    ctx_tiles = ctx_rows // tm
    lat_tiles = tiles_per_seq - ctx_tiles
    has_delta = delta is not None
    has_router = router is not None
    row_spec = pl.BlockSpec((tm, d), lambda i: (i, 0))
    vec_spec = pl.BlockSpec((1, d), lambda i: (0, 0))

    def mod_spec(slot):
        return pl.BlockSpec((mod.shape[0], d), lambda i: (0, slot))

    def ctx_map(i):
        b = i // tiles_per_seq
        return b * ctx_tiles + jnp.minimum(i - b * tiles_per_seq, ctx_tiles - 1), 0

    def lat_map(i):
        b = i // tiles_per_seq
        return b * lat_tiles + jnp.maximum(i - b * tiles_per_seq - ctx_tiles, 0), 0

    if split_src:
        operands, in_specs = list(src), [pl.BlockSpec((tm, d), ctx_map), pl.BlockSpec((tm, d), lat_map)]
    else:
        operands, in_specs = [src], [row_spec]
    if has_delta:
        operands += [delta, gate_mod]
        in_specs += [row_spec, mod_spec(gate_slot)]
    operands += [gain.reshape(1, d), mod, mod]
    in_specs += [vec_spec, mod_spec(shift_slot), mod_spec(scale_slot)]
    if has_router:
        rw, rb = router
        operands += [rw, rb]
        in_specs += [pl.BlockSpec(rw.shape, lambda i: (0, 0)), pl.BlockSpec(rb.shape, lambda i: (0, 0))]
    out_shape, out_specs = [], []
    if has_delta:
        out_shape.append(jax.ShapeDtypeStruct((t, d), F32))
        out_specs.append(row_spec)
    out_shape.append(jax.ShapeDtypeStruct((t, d), BF16))
    out_specs.append(row_spec)
    if has_router:
        out_shape.append(jax.ShapeDtypeStruct((t, LANES), F32))
        out_specs.append(pl.BlockSpec((tm, LANES), lambda i: (i, 0)))
    kern = functools.partial(_resid_norm_kernel, split_src=split_src, has_delta=has_delta, has_router=has_router,
                             tiles_per_seq=tiles_per_seq, ctx_tiles=ctx_tiles)
    return pl.pallas_call(
        kern, grid=(t // tm,), in_specs=in_specs, out_specs=out_specs, out_shape=out_shape,
        compiler_params=_params(("parallel",)), name="resid_norm",
    )(*operands)


def _final_norm_kernel(z_ref, delta_ref, gate_ref, gain_ref, out_ref):
    row = 1 + pl.program_id(0)
    z = z_ref[...] + gate_ref[pl.ds(row, 1), :] * delta_ref[...].astype(F32)
    out_ref[...] = _rmsnorm(z, gain_ref[...])


def _final_norm(z, delta, gate_mod, gate_slot, gain, batch, seq_rows, ctx_rows):
    t, d = z.shape
    tm = ROW_TILE
    lat_tiles = (seq_rows - ctx_rows) // tm
    tiles_per_seq = seq_rows // tm
    ctx_tiles = ctx_rows // tm
    row_spec = pl.BlockSpec((tm, d), lambda b, j: (b * tiles_per_seq + ctx_tiles + j, 0))
    return pl.pallas_call(
        _final_norm_kernel,
        grid=(batch, lat_tiles),
        in_specs=[row_spec, row_spec,
                  pl.BlockSpec((gate_mod.shape[0], d), lambda b, j: (0, gate_slot)),
                  pl.BlockSpec((1, d), lambda b, j: (0, 0))],
        out_specs=pl.BlockSpec((tm, d), lambda b, j: (b * lat_tiles + j, 0)),
        out_shape=jax.ShapeDtypeStruct((batch * lat_tiles * tm, d), F32),
        compiler_params=_params(("parallel", "parallel")), name="final_norm",
    )(z, delta, gate_mod, gain.reshape(1, d))


def _matmul_kernel(a_ref, w_ref, o_ref, *, w_transposed):
    dot = _dot_nt if w_transposed else _dot
    o_ref[...] = dot(a_ref[...], w_ref[...]).astype(o_ref.dtype)


def _matmul(a, w, layer, out_dtype, tm, tn, w_transposed=False):
    m, k = a.shape
    if w_transposed:
        n = w.shape[1]
        w_spec = pl.BlockSpec((None, tn, k), lambda i, j: (layer, j, 0))
    else:
        n = w.shape[2]
        w_spec = pl.BlockSpec((None, k, tn), lambda i, j: (layer, 0, j))
    return pl.pallas_call(
        functools.partial(_matmul_kernel, w_transposed=w_transposed),
        grid=(m // tm, n // tn),
        in_specs=[pl.BlockSpec((tm, k), lambda i, j: (i, 0)), w_spec],
        out_specs=pl.BlockSpec((tm, tn), lambda i, j: (i, j)),
        out_shape=jax.ShapeDtypeStruct((m, n), out_dtype),
        compiler_params=_params(("parallel", "arbitrary")), name="matmul",
    )(a, w)


def _proj_prep_kernel(a_ref, w_ref, src_ref, o_ref, dst_ref, *, n_prep):
    o_ref[...] = _dot_nt(a_ref[...], w_ref[...]).astype(o_ref.dtype)

    @pl.when(pl.program_id(0) * pl.num_programs(1) + pl.program_id(1) < n_prep)
    def _():
        dst_ref[...] = src_ref[0].astype(dst_ref.dtype)


def _proj_prep(a, w_t, w_in_t, next_layer, segments, tm, tn):
    m, k = a.shape
    n = w_t.shape[0]
    n_blk = _segment_blocks(segments)
    nj = n // tn
    assert (m // tm) * nj >= n_blk

    def prep_block(i, j):
        return jnp.minimum(i * nj + j, n_blk - 1)

    return pl.pallas_call(
        functools.partial(_proj_prep_kernel, n_prep=n_blk),
        grid=(m // tm, nj),
        in_specs=[pl.BlockSpec((tm, k), lambda i, j: (i, 0)),
                  pl.BlockSpec((tn, k), lambda i, j: (j, 0)),
                  pl.BlockSpec((pl.Element(1), pl.Element(PREP_ROWS), pl.Element(k)),
                               lambda i, j: (next_layer, _segment_row(prep_block(i, j), segments), 0))],
        out_specs=[pl.BlockSpec((tm, tn), lambda i, j: (i, j)),
                   pl.BlockSpec((PREP_ROWS, k), lambda i, j: (prep_block(i, j), 0))],
        out_shape=[jax.ShapeDtypeStruct((m, n), BF16), jax.ShapeDtypeStruct((n_blk * PREP_ROWS, k), BF16)],
        compiler_params=_params(("arbitrary", "arbitrary")), name="proj_prep",
    )(a, w_t, w_in_t)


def _gate_proj_kernel(a_ref, wt_ref, oc_ref, or_ref):
    a = a_ref[...]
    wt = wt_ref[...]
    oc_ref[...] = _dot_nt(a, wt)
    or_ref[...] = _dot_nt(wt, a)


def _gate_proj(a, wt, layer, tm):
    m, k = a.shape
    return pl.pallas_call(
        _gate_proj_kernel,
        grid=(m // tm,),
        in_specs=[pl.BlockSpec((tm, k), lambda i: (i, 0)),
                  pl.BlockSpec((None, LANES, k), lambda i: (layer, 0, 0))],
        out_specs=[pl.BlockSpec((tm, LANES), lambda i: (i, 0)), pl.BlockSpec((LANES, tm), lambda i: (0, i))],
        out_shape=[jax.ShapeDtypeStruct((m, LANES), F32), jax.ShapeDtypeStruct((LANES, m), F32)],
        compiler_params=_params(("parallel",)), name="gate_proj",
    )(a, wt)


def _tri_masks(length, reverse):
    row = lax.broadcasted_iota(jnp.int32, (length, length), 0)
    col = lax.broadcasted_iota(jnp.int32, (length, length), 1)
    if reverse:
        return col >= row, row >= col
    return col <= row, row <= col


def _mlstm_kernel(q_ref, k_ref, v_ref, gc_ref, gr_ref, bc_ref, br_ref, h_ref, ct_ref, n_ref, m_ref, *,
                  reverse, heads, dqk, dv):
    @pl.when(pl.program_id(1) == 0)
    def _():
        ct_ref[...] = jnp.zeros_like(ct_ref)
        n_ref[...] = jnp.zeros_like(n_ref)
        m_ref[...] = jnp.zeros_like(m_ref)

    length = q_ref.shape[0]
    valid, valid_t = _tri_masks(length, reverse)
    gc = gc_ref[...] + bc_ref[...]
    gr = gr_ref[...] + br_ref[...]
    d0 = 2 * heads if reverse else 0
    scale = dqk ** -0.5
    for h in range(heads):
        i_col = gc[:, d0 + h:d0 + h + 1]
        f_col = _log_sigmoid(gc[:, d0 + heads + h:d0 + heads + h + 1])
        i_row = gr[d0 + h:d0 + h + 1, :]
        f_row = _log_sigmoid(gr[d0 + heads + h:d0 + heads + h + 1, :])
        b_col = jnp.sum(jnp.where(valid, f_row, 0.0), axis=1, keepdims=True)
        b_row = jnp.sum(jnp.where(valid_t, f_col, 0.0), axis=0, keepdims=True)
        b_end = jnp.sum(f_row, axis=1, keepdims=True)
        m_prev = m_ref[h][:, :1]
        log_d = jnp.where(valid, b_col - b_row + i_row, NEG)
        m_inter = b_col + m_prev
        m_t = jnp.maximum(m_inter, jnp.max(log_d, axis=1, keepdims=True))
        w_intra = jnp.exp(log_d - m_t) * scale
        w_inter = jnp.exp(m_inter - m_t) * scale
        q = q_ref[:, h * dqk:(h + 1) * dqk]
        k = k_ref[:, h * dqk:(h + 1) * dqk]
        v = v_ref[:, h * dv:(h + 1) * dv]
        s = _dot_nt(q, k) * w_intra
        num = _dot(s.astype(BF16), v) + w_inter * _dot(q, ct_ref[h].astype(BF16))
        qn = jnp.sum(q.astype(F32) * n_ref[h], axis=1, keepdims=True)
        den = jnp.sum(s, axis=1, keepdims=True) + w_inter * qn
        h_ref[:, h * dv:(h + 1) * dv] = (num / jnp.maximum(jnp.abs(den), jnp.exp(-m_t))).astype(h_ref.dtype)
        log_w = b_end - b_col + i_col
        m_new = jnp.maximum(b_end + m_prev, jnp.max(log_w, axis=0, keepdims=True))
        w_end = jnp.exp(log_w - m_new)
        decay = jnp.exp(b_end + m_prev - m_new)
        wv = (w_end * v.astype(F32)).astype(BF16)
        ct_ref[h] = decay * ct_ref[h] + _dot_tn(k, wv)
        n_ref[h] = decay * n_ref[h] + jnp.sum(w_end * k.astype(F32), axis=0, keepdims=True)
        m_ref[h] = jnp.broadcast_to(m_new, (1, LANES))


def _chunk_index(i, n_chunks, reverse):
    if not reverse:
        return i
    return jnp.where(i == 0, 0, n_chunks - i)


def _mlstm(p, col0, dv, gates_c, gates_r, bias_c, bias_r, batch, seq_rows, reverse):
    t = p.shape[0]
    length = ROW_TILE
    n_chunks = seq_rows // length
    heads = A_HEADS
    dqk = dv // 2
    qw, vw = heads * dqk, heads * dv

    def rows(b, i):
        return b * n_chunks + _chunk_index(i, n_chunks, reverse)

    kern = functools.partial(_mlstm_kernel, reverse=reverse, heads=heads, dqk=dqk, dv=dv)
    return pl.pallas_call(
        kern,
        grid=(batch, n_chunks),
        in_specs=[
            pl.BlockSpec((length, qw), lambda b, i: (rows(b, i), col0 // qw)),
            pl.BlockSpec((length, qw), lambda b, i: (rows(b, i), col0 // qw + 1)),
            pl.BlockSpec((length, vw), lambda b, i: (rows(b, i), (col0 + 2 * qw) // vw)),
            pl.BlockSpec((length, LANES), lambda b, i: (rows(b, i), 0)),
            pl.BlockSpec((4 * heads, length), lambda b, i: (0, rows(b, i))),
            pl.BlockSpec((1, LANES), lambda b, i: (0, 0)),
            pl.BlockSpec((4 * heads, 1), lambda b, i: (0, 0)),
        ],
        out_specs=pl.BlockSpec((length, vw), lambda b, i: (rows(b, i), 0)),
        out_shape=jax.ShapeDtypeStruct((t, vw), BF16),
        scratch_shapes=[pltpu.VMEM((heads, dqk, dv), F32), pltpu.VMEM((heads, 1, dqk), F32),
                        pltpu.VMEM((heads, 1, LANES), F32)],
        compiler_params=_params(("parallel", "arbitrary")), name="mlstm_scan",
    )(p, p, p, gates_c, gates_r, bias_c, bias_r)


def _rope(t, cosf, sinf, half):
    if 2 * half == LANES:
        partner = pltpu.roll(t, half, axis=1)
    else:
        lane = lax.broadcasted_iota(jnp.int32, t.shape, 1)
        partner = jnp.where((lane & (2 * half - 1)) < half, pltpu.roll(t, LANES - half, axis=1),
                            pltpu.roll(t, half, axis=1))
    return t * cosf + partner * sinf


def _retention_kernel(q_ref, k_ref, v_ref, cos_ref, sin_ref, dp_ref, o_ref, s_ref, *, reverse, heads, dk, dv):
    @pl.when(pl.program_id(1) == 0)
    def _():
        s_ref[...] = jnp.zeros_like(s_ref)

    length = q_ref.shape[0]
    row = lax.broadcasted_iota(jnp.int32, (length, length), 0)
    col = lax.broadcasted_iota(jnp.int32, (length, length), 1)
    diff = (col - row if reverse else row - col).astype(F32)
    tpos = lax.broadcasted_iota(jnp.int32, (length, 1), 0).astype(F32)
    lg_all = -_softplus(dp_ref[...])
    cosf = cos_ref[...]
    sinf = sin_ref[...]
    scale = dk ** -0.5
    d = 1 if reverse else 0
    for h in range(heads):
        lg = lg_all[d:d + 1, h:h + 1]
        decay_mat = jnp.where(diff >= 0.0, jnp.exp(lg * jnp.maximum(diff, 0.0)), 0.0) * scale
        if reverse:
            read_w = jnp.exp(lg * (length - tpos))
            write_w = jnp.exp(lg * tpos)
        else:
            read_w = jnp.exp(lg * (tpos + 1.0))
            write_w = jnp.exp(lg * (length - 1.0 - tpos))
        chunk_decay = jnp.exp(lg * float(length))
        q = _rope(q_ref[:, h * dk:(h + 1) * dk].astype(F32), cosf, sinf, dk // 2).astype(BF16)
        k = _rope(k_ref[:, h * dk:(h + 1) * dk].astype(F32), cosf, sinf, dk // 2).astype(BF16)
        v = v_ref[:, h * dv:(h + 1) * dv]
        s = _dot_nt(q, k) * decay_mat
        o_ref[:, h * dv:(h + 1) * dv] = (_dot(s.astype(BF16), v)
                                         + read_w * _dot(q, s_ref[h].astype(BF16))).astype(o_ref.dtype)
        wv = (write_w * v.astype(F32)).astype(BF16)
        s_ref[h] = chunk_decay * s_ref[h] + scale * _dot_tn(k, wv)


def _retention(p, col0, dv, cosf, sinf, decay_p, batch, seq_rows, reverse):
    t = p.shape[0]
    length = ROW_TILE
    n_chunks = seq_rows // length
    heads = B_HEADS
    dk = dv // 2
    assert dk == LANES
    qw, vw = heads * dk, heads * dv

    def rows(b, i):
        return b * n_chunks + _chunk_index(i, n_chunks, reverse)

    kern = functools.partial(_retention_kernel, reverse=reverse, heads=heads, dk=dk, dv=dv)
    return pl.pallas_call(
        kern,
        grid=(batch, n_chunks),
        in_specs=[
            pl.BlockSpec((length, qw), lambda b, i: (rows(b, i), col0 // qw)),
            pl.BlockSpec((length, qw), lambda b, i: (rows(b, i), col0 // qw + 1)),
            pl.BlockSpec((length, vw), lambda b, i: (rows(b, i), (col0 + 2 * qw) // vw)),
            pl.BlockSpec((length, LANES), lambda b, i: (_chunk_index(i, n_chunks, reverse), 0)),
            pl.BlockSpec((length, LANES), lambda b, i: (_chunk_index(i, n_chunks, reverse), 0)),
            pl.BlockSpec(decay_p.shape, lambda b, i: (0, 0)),
        ],
        out_specs=pl.BlockSpec((length, vw), lambda b, i: (rows(b, i), 0)),
        out_shape=jax.ShapeDtypeStruct((t, vw), BF16),
        scratch_shapes=[pltpu.VMEM((heads, dk, dv), F32)],
        compiler_params=_params(("parallel", "arbitrary")), name="retention_scan",
    )(p, p, p, cosf, sinf, decay_p)


def _headnorm_kernel(hf_ref, hb_ref, gate_ref, gain_ref, y_ref, *, heads, dv, center, silu_gate):
    g = gate_ref[...].astype(F32)
    sg = _sigmoid(g)
    gate = g * sg if silu_gate else sg
    gain = gain_ref[...]
    for h in range(heads):
        sl = slice(h * dv, (h + 1) * dv)
        x = hf_ref[:, sl].astype(F32) + hb_ref[:, sl].astype(F32)
        if center:
            x = x - jnp.mean(x, axis=-1, keepdims=True)
        y = x * lax.rsqrt(jnp.mean(x * x, axis=-1, keepdims=True) + EPS)
        y_ref[:, sl] = (gate[:, sl] * (y * gain[:, sl])).astype(y_ref.dtype)


def _headnorm(hf, hb, p, gate_col, gain, heads, center, silu_gate):
    t, w = hf.shape
    tm = _row_tile(t, 2 * ROW_TILE)
    kern = functools.partial(_headnorm_kernel, heads=heads, dv=w // heads, center=center, silu_gate=silu_gate)
    row_spec = pl.BlockSpec((tm, w), lambda i: (i, 0))
    return pl.pallas_call(
        kern, grid=(t // tm,),
        in_specs=[row_spec, row_spec, pl.BlockSpec((tm, w), lambda i: (i, gate_col // w)),
                  pl.BlockSpec((1, w), lambda i: (0, 0))],
        out_specs=row_spec,
        out_shape=jax.ShapeDtypeStruct((t, w), BF16),
        compiler_params=_params(("parallel",)), name="headnorm_gate",
    )(hf, hb, p, gain.reshape(1, w))


def _lane_low(shape, hd):
    return lax.broadcasted_iota(jnp.int32, shape, 1) < hd


def _stack_heads(q, hd):
    low = _lane_low(q.shape, hd)
    zero = jnp.zeros_like(q)
    return jnp.concatenate([jnp.where(low, q, zero), jnp.where(low, zero, q)], axis=0)


def _unstack_heads(o, hd):
    rows = o.shape[0] // 2
    return jnp.where(_lane_low((rows, LANES), hd), o[:rows], o[rows:])


def _window_kernel(sink_ref, q_ref, kp_ref, k0_ref, kn_ref, vp_ref, v0_ref, vn_ref, kc_ref, vc_ref,
                   cq_ref, sq_ref, cp_ref, sp_ref, cn_ref, sn_ref, o_ref, *, heads, kv_heads, hd, ctx_blocks,
                   n_latent, slabs_per_dot):
    i = pl.program_id(1)
    j = i - ctx_blocks
    w = q_ref.shape[0]
    half = hd // 2
    slabs_per_group = heads // kv_heads // 2

    def roped(x_ref, c_ref, s_ref):
        return _rope(x_ref[...].astype(F32), c_ref[...], s_ref[...], half)

    kb = jnp.concatenate([roped(kp_ref, cp_ref, sp_ref), roped(k0_ref, cq_ref, sq_ref),
                          roped(kn_ref, cn_ref, sn_ref)], axis=0)
    vb = jnp.concatenate([vp_ref[...], v0_ref[...], vn_ref[...]], axis=0).astype(F32)
    kc = kc_ref[...].astype(F32)
    vc = vc_ref[...].astype(F32)

    def group_copy(x, g):
        keep = _lane_low(x.shape, hd) if g == 0 else jnp.logical_not(_lane_low(x.shape, hd))
        return jnp.where(keep, x, pltpu.roll(x, hd, axis=1)).astype(BF16)

    c = lax.broadcasted_iota(jnp.int32, (3 * w, w), 0)
    t = lax.broadcasted_iota(jnp.int32, (3 * w, w), 1)
    lower = jnp.maximum(t + (w - WINDOW), (1 - j) * w)
    upper = jnp.minimum(t + (w + WINDOW), n_latent - 1 - (j - 1) * w)
    upper = jnp.where(j < 0, -1, upper)
    mask_bias = jnp.where((c >= lower) & (c <= upper), 0.0, NEG)
    mask_bias = jnp.concatenate([mask_bias] * (2 * slabs_per_dot), axis=1)
    scale = hd ** -0.5
    cq = cq_ref[...]
    sq = sq_ref[...]
    for g in range(kv_heads):
        kk, vv, kkc, vvc = group_copy(kb, g), group_copy(vb, g), group_copy(kc, g), group_copy(vc, g)
        for s0 in range(0, slabs_per_group, slabs_per_dot):
            slabs = [g * slabs_per_group + s0 + s for s in range(slabs_per_dot)]
            qs = []
            sinks = []
            for slab in slabs:
                qr = _rope(q_ref[:, slab * LANES:(slab + 1) * LANES].astype(F32), cq, sq, half)
                qs.append(_stack_heads((qr * scale).astype(BF16), hd))
                sinks += [jnp.full((1, w), sink_ref[2 * slab], F32), jnp.full((1, w), sink_ref[2 * slab + 1], F32)]
            qq = jnp.concatenate(qs, axis=0)
            sink = jnp.concatenate(sinks, axis=1)
            s_loc = _dot_nt(kk, qq) + mask_bias
            s_ctx = _dot_nt(kkc, qq)
            m = jnp.maximum(jnp.maximum(jnp.max(s_loc, axis=0, keepdims=True),
                                        jnp.max(s_ctx, axis=0, keepdims=True)), sink)
            e_loc = jnp.exp(s_loc - m)
            e_ctx = jnp.exp(s_ctx - m)
            den = (jnp.exp(sink - m) + jnp.sum(e_loc, axis=0, keepdims=True)
                   + jnp.sum(e_ctx, axis=0, keepdims=True))
            o_t = (_dot_tn(vv, e_loc.astype(BF16)) + _dot_tn(vvc, e_ctx.astype(BF16))) / den
            o = o_t.T
            for n, slab in enumerate(slabs):
                o_ref[:, slab * LANES:(slab + 1) * LANES] = _unstack_heads(
                    o[2 * n * w:2 * (n + 1) * w], hd).astype(o_ref.dtype)


def _window_attention(p, col0, cosf, sinf, sink, batch, seq_rows, ctx_rows):
    t = p.shape[0]
    w = WINDOW
    heads, kv_heads = C_HEADS, C_KV_HEADS
    hd = LANES // 2
    assert kv_heads * hd == LANES
    qw = heads * hd
    nblk = seq_rows // w
    ctx_blocks = ctx_rows // w
    kcol = (col0 + qw) // LANES
    vcol = kcol + 1

    def cur(b, i):
        return b * nblk + i

    def prev(b, i):
        return b * nblk + jnp.maximum(i - 1, 0)

    def nxt(b, i):
        return b * nblk + jnp.minimum(i + 1, nblk - 1)

    def kv_spec(fn, colblk):
        return pl.BlockSpec((w, LANES), lambda b, i: (fn(b, i), colblk))

    def tab_spec(fn):
        return pl.BlockSpec((w, LANES), lambda b, i: (fn(0, i), 0))

    kern = functools.partial(_window_kernel, heads=heads, kv_heads=kv_heads, hd=hd, ctx_blocks=ctx_blocks,
                             n_latent=seq_rows - ctx_rows, slabs_per_dot=4)
    ctx_spec_k = pl.BlockSpec((ctx_rows, LANES), lambda b, i: (b * (seq_rows // ctx_rows), kcol))
    ctx_spec_v = pl.BlockSpec((ctx_rows, LANES), lambda b, i: (b * (seq_rows // ctx_rows), vcol))
    return pl.pallas_call(
        kern,
        grid=(batch, nblk),
        in_specs=[
            pl.BlockSpec(memory_space=pltpu.SMEM),
            pl.BlockSpec((w, qw), lambda b, i: (cur(b, i), col0 // qw)),
            kv_spec(prev, kcol), kv_spec(cur, kcol), kv_spec(nxt, kcol),
            kv_spec(prev, vcol), kv_spec(cur, vcol), kv_spec(nxt, vcol),
            ctx_spec_k, ctx_spec_v,
            tab_spec(cur), tab_spec(cur), tab_spec(prev), tab_spec(prev), tab_spec(nxt), tab_spec(nxt),
        ],
        out_specs=pl.BlockSpec((w, qw), lambda b, i: (cur(b, i), 0)),
        out_shape=jax.ShapeDtypeStruct((t, qw), BF16),
        compiler_params=_params(("parallel", "arbitrary")), name="window_attention",
    )(sink, p, p, p, p, p, p, p, p, p, cosf, sinf, cosf, sinf, cosf, sinf)


def _nbr_layout():
    masked = 2 * NA_ROWS - 1
    variants = [(0, [0] * NBR_ROWS), (NA_ROWS // 2, list(range(NBR_ROWS))),
                (NBR_UNION - NBR_ROWS, [NBR_UNION - NA_ROWS] * NBR_ROWS)]
    pairs, index = [], []
    for delta, rel_start in variants:
        per_kr = []
        for kr in range(NBR_UNION):
            codes = [kr - (delta + rr) + NA_ROWS - 1 if rel_start[rr] <= kr < rel_start[rr] + NA_ROWS else masked
                     for rr in range(NBR_ROWS)]
            row = []
            for a in range(NBR_ROWS // 2):
                pair = (codes[2 * a], codes[2 * a + 1])
                if pair not in pairs:
                    pairs.append(pair)
                row.append(pairs.index(pair))
            per_kr.append(row)
        index.append(per_kr)
    return pairs, index


def _nbr_kernel(q_ref, k_ref, v_ref, tab_ref, o_ref, bias_ref, *, hd, ctx_rows, grid_rows, index, slabs_per_step):
    i = pl.program_id(2)
    scale = hd ** -0.5
    tq = q_ref.shape[0]
    lanes = [slice(sl * LANES, (sl + 1) * LANES) for sl in range(slabs_per_step)]
    qqs = [_stack_heads(q_ref[:, ln] * scale, hd) for ln in lanes]
    kcs = [k_ref[0:ctx_rows, ln] for ln in lanes]
    vcs = [v_ref[0:ctx_rows, ln] for ln in lanes]
    s_ctxs = [_dot_nt(kc, qq) for kc, qq in zip(kcs, qqs)]
    m_ctxs = [jnp.max(s, axis=0, keepdims=True) for s in s_ctxs]

    @pl.when(i == 0)
    def _():
        for sl, ln in enumerate(lanes):
            for variant, per_kr in enumerate(index):
                for kr, row in enumerate(per_kr):
                    for sub in range(2):
                        for a, u in enumerate(row):
                            bias_ref[sl, variant, kr * GRID_W:(kr + 1) * GRID_W,
                                     sub * tq + a * LANES:sub * tq + (a + 1) * LANES] = tab_ref[2 * sl + sub, u]
            e = jnp.exp(s_ctxs[sl] - m_ctxs[sl])
            o_t = _dot_tn(vcs[sl], e.astype(BF16)) / jnp.sum(e, axis=0, keepdims=True)
            o_ref[:, ln] = _unstack_heads(o_t.T, hd).astype(o_ref.dtype)

    @pl.when(i > 0)
    def _():
        blk = i - 1
        n_blk = grid_rows // NBR_ROWS
        u0 = jnp.clip(blk * NBR_ROWS - NA_ROWS // 2, 0, grid_rows - NBR_UNION)
        variant = jnp.where(blk == 0, 0, jnp.where(blk == n_blk - 1, 2, 1))
        base = pl.multiple_of(ctx_rows + u0 * GRID_W, GRID_W)
        for sl, ln in enumerate(lanes):
            kw = k_ref[pl.ds(base, NBR_UNION * GRID_W), ln]
            vw = v_ref[pl.ds(base, NBR_UNION * GRID_W), ln]
            s_loc = _dot_nt(kw, qqs[sl]) + bias_ref[sl, variant]
            m = jnp.maximum(jnp.max(s_loc, axis=0, keepdims=True), m_ctxs[sl])
            e_loc = jnp.exp(s_loc - m)
            e_ctx = jnp.exp(s_ctxs[sl] - m)
            den = jnp.sum(e_loc, axis=0, keepdims=True) + jnp.sum(e_ctx, axis=0, keepdims=True)
            o_t = (_dot_tn(vw, e_loc.astype(BF16)) + _dot_tn(vcs[sl], e_ctx.astype(BF16))) / den
            o_ref[:, ln] = _unstack_heads(o_t.T, hd).astype(o_ref.dtype)


def _nbr_table(na_bias_l, pairs):
    col = jnp.arange(GRID_W)
    col_start = jnp.clip(col - NA_COLS // 2, 0, GRID_W - NA_COLS)
    col_valid = (col[None, :] >= col_start[:, None]) & (col[None, :] < col_start[:, None] + NA_COLS)
    dc = jnp.clip(col[None, :] - col[:, None], -(NA_COLS - 1), NA_COLS - 1) + (NA_COLS - 1)
    blocks = jnp.where(col_valid.T[None, None], na_bias_l[:, :, dc.T], NEG)
    blocks = jnp.concatenate([blocks, jnp.full_like(blocks[:, :1], NEG)], axis=1)
    left = blocks[:, np.array([p[0] for p in pairs])]
    right = blocks[:, np.array([p[1] for p in pairs])]
    return jnp.concatenate([left, right], axis=-1).astype(F32)


def _nbr_attention(p, col0, na_bias_l, batch, seq_rows, ctx_rows):
    t = p.shape[0]
    heads = D_HEADS
    hd = LANES // 2
    qw = heads * hd
    slabs = heads // 2
    tq = NBR_ROWS * GRID_W
    n_latent = seq_rows - ctx_rows
    grid_rows = n_latent // GRID_W
    assert tq == ctx_rows and grid_rows % NBR_ROWS == 0 and grid_rows >= NBR_UNION + 1
    assert slabs % NBR_SLABS_PER_STEP == 0 and (col0 // LANES) % NBR_SLABS_PER_STEP == 0
    steps = seq_rows // tq
    q0 = col0 // LANES
    pairs, index = _nbr_layout()
    table = _nbr_table(na_bias_l, pairs)
    sps = NBR_SLABS_PER_STEP
    width = sps * LANES
    kern = functools.partial(_nbr_kernel, hd=hd, ctx_rows=ctx_rows, grid_rows=grid_rows, index=index,
                             slabs_per_step=sps)
    return pl.pallas_call(
        kern,
        grid=(batch, slabs // sps, steps),
        in_specs=[
            pl.BlockSpec((tq, width), lambda b, s, i: (b * steps + i, q0 // sps + s)),
            pl.BlockSpec((seq_rows, width), lambda b, s, i: (b, (q0 + slabs) // sps + s)),
            pl.BlockSpec((seq_rows, width), lambda b, s, i: (b, (q0 + 2 * slabs) // sps + s)),
            pl.BlockSpec((2 * sps,) + table.shape[1:], lambda b, s, i: (s, 0, 0, 0)),
        ],
        out_specs=pl.BlockSpec((tq, width), lambda b, s, i: (b * steps + i, s)),
        out_shape=jax.ShapeDtypeStruct((t, qw), BF16),
        scratch_shapes=[pltpu.VMEM((sps, len(index), NBR_UNION * GRID_W, 2 * tq), F32)],
        compiler_params=_params(("parallel", "parallel", "arbitrary")), name="nbr_attention",
    )(p, p, p, table)


def _merge_kernel(*refs):
    ys = refs[0:N_BRANCH]
    gs = refs[N_BRANCH:2 * N_BRANCH]
    ws = refs[2 * N_BRANCH:3 * N_BRANCH]
    o_ref = refs[3 * N_BRANCH]
    acc = None
    for y_ref, g_ref, w_ref in zip(ys, gs, ws):
        term = _sigmoid(g_ref[...].astype(F32)) * _dot(y_ref[...], w_ref[...])
        acc = term if acc is None else acc + term
    o_ref[...] = acc.astype(o_ref.dtype)


def _merge(ys, p, gate_col0, w_branch, row0, tm, tn):
    t, bw = ys[0].shape
    d = w_branch.shape[1]
    nt = d // tn
    y_spec = pl.BlockSpec((tm, bw), lambda i, j: (i, 0))
    g_specs = [pl.BlockSpec((tm, tn), functools.partial(lambda i, j, br: (i, (gate_col0 + br * d) // tn + j), br=br))
               for br in range(N_BRANCH)]
    w_specs = [pl.BlockSpec((bw, tn), functools.partial(lambda i, j, br: (row0 // bw + br, j), br=br))
               for br in range(N_BRANCH)]
    return pl.pallas_call(
        _merge_kernel,
        grid=(t // tm, nt),
        in_specs=[y_spec] * N_BRANCH + g_specs + w_specs,
        out_specs=pl.BlockSpec((tm, tn), lambda i, j: (i, j)),
        out_shape=jax.ShapeDtypeStruct((t, d), BF16),
        compiler_params=_params(("parallel", "arbitrary")), name="merge_branches",
    )(*ys, *([p] * N_BRANCH), *([w_branch] * N_BRANCH))


def _moe_kernel(t_ref, gates_ref, w1_ref, b1_ref, w2_ref, b2_ref, o_ref, acc_ref, *, ff, pairs):
    e = pl.program_id(1)
    gates = gates_ref[...]

    @pl.when(e == 0)
    def _():
        acc_ref[...] = _dot(gates.astype(BF16), b2_ref[...])

    tokens = t_ref[...]
    lane = lax.broadcasted_iota(jnp.int32, gates.shape, 1)
    acts = []
    for pr in range(pairs):
        hid = _dot(tokens, w1_ref[pr]) + b1_ref[pr]
        g_h = jnp.minimum(hid[:, :2 * ff], SWIGLU_LIMIT)
        u_h = jnp.clip(hid[:, 2 * ff:], -SWIGLU_LIMIT, SWIGLU_LIMIT)
        cols = []
        for s in range(2):
            expert = (e * pairs + pr) * 2 + s
            gsel = jnp.sum(jnp.where(lane == expert, gates, 0.0), axis=-1, keepdims=True)
            cols.append(jnp.broadcast_to(gsel, (gates.shape[0], ff)))
        act = g_h * _sigmoid(SWIGLU_ALPHA * g_h) * (u_h + 1.0) * jnp.concatenate(cols, axis=-1)
        acts.append(act.astype(BF16))
    acc_ref[...] += _dot(jnp.concatenate(acts, axis=-1), w2_ref[...])

    @pl.when(e == pl.num_programs(1) - 1)
    def _():
        o_ref[...] = acc_ref[...].astype(o_ref.dtype)


def _moe(h, gates, w1p, b1p, w2, b2p, pair0, tm, pairs):
    t, d = h.shape
    width = w1p.shape[2]
    ff = width // 4
    steps = b1p.shape[0] // pairs
    kern = functools.partial(_moe_kernel, ff=ff, pairs=pairs)
    return pl.pallas_call(
        kern,
        grid=(t // tm, steps),
        in_specs=[
            pl.BlockSpec((tm, d), lambda i, e: (i, 0)),
            pl.BlockSpec((tm, LANES), lambda i, e: (i, 0)),
            pl.BlockSpec((pairs, d, width), lambda i, e: (pair0 // pairs + e, 0, 0)),
            pl.BlockSpec((pairs, 1, width), lambda i, e: (e, 0, 0)),
            pl.BlockSpec((pairs * 2 * ff, d), lambda i, e: (pair0 // pairs + e, 0)),
            pl.BlockSpec(b2p.shape, lambda i, e: (0, 0)),
        ],
        out_specs=pl.BlockSpec((tm, d), lambda i, e: (i, 0)),
        out_shape=jax.ShapeDtypeStruct((t, d), BF16),
        scratch_shapes=[pltpu.VMEM((tm, d), F32)],
        compiler_params=_params(("parallel", "arbitrary")), name="moe_experts",
    )(h, gates, w1p, b1p, w2, b2p)


def _rope_tables(n_latent, ctx_rows, hd):
    pos = np.arange(n_latent)
    row = (pos // GRID_W).astype(np.float32)
    col = (pos % GRID_W).astype(np.float32)
    n_freq = hd // 4
    inv = jnp.asarray(ROPE_BASE, F32) ** (-jnp.arange(n_freq, dtype=F32) / n_freq)
    ang = jnp.concatenate([jnp.asarray(row)[:, None] * inv, jnp.asarray(col)[:, None] * inv], axis=-1)
    ang = jnp.concatenate([jnp.zeros((ctx_rows, hd // 2), F32), ang], axis=0)
    cos, sin = jnp.cos(ang), jnp.sin(ang)
    reps = LANES // hd
    cosf = jnp.tile(jnp.concatenate([cos, cos], axis=-1), (1, reps))
    sinf = jnp.tile(jnp.concatenate([-sin, sin], axis=-1), (1, reps))
    return cosf, sinf


def _pad_lanes(a, value=0.0):
    pad = LANES - a.shape[-1]
    return jnp.pad(a, [(0, 0)] * (a.ndim - 1) + [(0, pad)], constant_values=value)


def kernel(x, c, ctx, c_ctx, ada_down, ada_up, ada_bias, norm_gain, w_in, mlstm_gate_bias, mlstm_norm_gain,
           ret_decay, ret_norm_gain, sink, na_bias, w_branch, w_out, router_w, router_b, moe_w1, moe_b1,
           moe_w2, moe_b2, final_gain):
    batch, n_latent, d = x.shape
    ctx_rows = ctx.shape[1]
    depth = w_in.shape[0]
    seq_rows = ctx_rows + n_latent
    t = batch * seq_rows
    bw = d // N_BRANCH
    n_experts = router_w.shape[2]
    ff = moe_w2.shape[2]

    a_dv, b_dv = bw // A_HEADS, bw // B_HEADS
    c_hd, d_hd = bw // C_HEADS, bw // D_HEADS
    assert c_hd == LANES // 2 and d_hd == LANES // 2
    a_sz = (bw // 2, bw // 2, bw, bw)
    n_gate = 4 * A_HEADS
    b_sz = (bw // 2, bw // 2, bw, bw)
    c_sz = (bw, C_KV_HEADS * c_hd, C_KV_HEADS * c_hd)
    d_sz = (bw, bw, bw)
    sizes = a_sz + (n_gate,) + b_sz + c_sz + d_sz + (N_BRANCH * d,)
    offs = [int(o) for o in np.concatenate([[0], np.cumsum(sizes)])]
    a0, g0, b0, c0, d0, bg0, end = offs[0], offs[4], offs[5], offs[9], offs[12], offs[15], offs[16]
    col_gate = 0
    col_a = N_BRANCH * d
    col_b = col_a + sum(a_sz)
    col_d = col_b + sum(b_sz)
    col_c = col_d + sum(d_sz)

    w_in_t = jnp.swapaxes(w_in, 1, 2)
    segments = [(bg0, end), (a0, g0), (b0, c0), (d0, bg0), (c0, d0)]
    w_g_t = _gate_prep(w_in_t, (g0, b0))
    w_cat_t = _win_prep(w_in_t, 0, segments)
    w1p = _moe_w1_prep(moe_w1)
    wb_bf, wo_bf, w2_bf = _cast_bf16(w_branch.reshape(depth * N_BRANCH * bw, d), w_out.reshape(depth * d, d),
                                     moe_w2.reshape(depth * n_experts * ff, d))
    b1p_all = jnp.concatenate([moe_b1[..., 0::2].reshape(depth, n_experts // 2, 1, 2 * ff),
                               moe_b1[..., 1::2].reshape(depth, n_experts // 2, 1, 2 * ff)], axis=-1)
    b2p_all = jnp.pad(moe_b2, ((0, 0), (0, LANES - n_experts), (0, 0))).astype(BF16)

    mod_all = _ada_modulation(jnp.pad(jnp.concatenate([c_ctx[None], c], axis=0), ((0, 8 - 1 - batch), (0, 0))),
                              ada_down, ada_up, ada_bias)
    cos_b, sin_b = _rope_tables(n_latent, ctx_rows, b_dv // 2)
    cos_c, sin_c = _rope_tables(n_latent, ctx_rows, c_hd)
    decay_p = jnp.pad(ret_decay, ((0, 0), (0, 8 - ret_decay.shape[1]), (0, LANES - ret_decay.shape[2])))

    tm_proj = _row_tile(t, 768)
    tm_moe = _row_tile(t, 512)
    z = (ctx.reshape(batch * ctx_rows, d), x.reshape(batch * n_latent, d))
    delta = None
    for l in range(depth):
        mod = mod_all[l]
        if l == 0:
            h = _resid_norm(z, norm_gain[l, 0], mod, 0, 1, batch, seq_rows, ctx_rows)[0]
        else:
            z, h = _resid_norm(z, norm_gain[l, 0], mod, 0, 1, batch, seq_rows, ctx_rows, delta=delta,
                               gate_mod=mod_all[l - 1], gate_slot=5)
        if l + 1 < depth:
            p, w_cat_t = _proj_prep(h, w_cat_t, w_in_t, l + 1, segments, tm=tm_proj, tn=1280)
        else:
            p = _matmul(h, w_cat_t[None], 0, BF16, tm=tm_proj, tn=1280, w_transposed=True)
        gates_c, gates_r = _gate_proj(h, w_g_t, l, tm_proj)
        gates_r = gates_r[:n_gate]
        bias_c = _pad_lanes(mlstm_gate_bias[l][None])
        bias_r = mlstm_gate_bias[l][:, None]

        hf = _mlstm(p, col_a, a_dv, gates_c, gates_r, bias_c, bias_r, batch, seq_rows, False)
        hb = _mlstm(p, col_a, a_dv, gates_c, gates_r, bias_c, bias_r, batch, seq_rows, True)
        y_a = _headnorm(hf, hb, p, col_a + 2 * bw, mlstm_norm_gain[l], A_HEADS, False, False)
        of = _retention(p, col_b, b_dv, cos_b, sin_b, decay_p[l], batch, seq_rows, False)
        ob = _retention(p, col_b, b_dv, cos_b, sin_b, decay_p[l], batch, seq_rows, True)
        y_b = _headnorm(of, ob, p, col_b + 2 * bw, ret_norm_gain[l], B_HEADS, True, True)
        y_c = _window_attention(p, col_c, cos_c, sin_c, sink[l], batch, seq_rows, ctx_rows)
        y_d = _nbr_attention(p, col_d, na_bias[l], batch, seq_rows, ctx_rows)

        acc = _merge([y_a, y_b, y_c, y_d], p, col_gate, wb_bf, l * N_BRANCH * bw, tm=tm_moe, tn=1024)
        delta1 = _matmul(acc, wo_bf.reshape(depth, d, d), l, BF16, tm=tm_proj, tn=1024)

        rw = _pad_lanes(router_w[l]).astype(BF16)
        rb = _pad_lanes(router_b[l][None], NEG)
        z, h2, gates = _resid_norm(z, norm_gain[l, 1], mod, 3, 4, batch, seq_rows, ctx_rows, delta=delta1,
                                   gate_mod=mod, gate_slot=2, router=(rw, rb))
        delta = _moe(h2, gates, w1p, b1p_all[l], w2_bf, b2p_all[l], l * (n_experts // 2), tm=tm_moe, pairs=2)

    out = _final_norm(z, delta, mod_all[depth - 1], 5, final_gain, batch, seq_rows, ctx_rows)
    return out.reshape(batch, n_latent, d)
```

```python
import functools

import jax
import jax.numpy as jnp
import numpy as np
from jax import lax
from jax.experimental import pallas as pl
from jax.experimental.pallas import tpu as pltpu

F32 = jnp.float32
BF16 = jnp.bfloat16

GRID_W = 64
EPS = 1e-6
NEG = -1e30
ROPE_BASE = 10000.0
N_BRANCH = 4
A_HEADS = 4
B_HEADS = 4
C_HEADS = 16
C_KV_HEADS = 2
D_HEADS = 16
WINDOW = 128
NA_ROWS = 8
NA_COLS = 16
TOP_K = 4
SWIGLU_LIMIT = 7.0
SWIGLU_ALPHA = 1.702

LANES = 128
VMEM_LIMIT = 56 * 1024 * 1024
ROW_TILE = 256
NBR_ROWS = 4
NBR_UNION = NBR_ROWS + NA_ROWS - 1
NBR_SLABS_PER_STEP = 2


def _row_tile(rows, target):
    best = ROW_TILE
    for tile in range(ROW_TILE, target + 1, ROW_TILE):
        if rows % tile == 0:
            best = tile
    assert rows % best == 0
    return best


def _params(sem, vmem=VMEM_LIMIT):
    return pltpu.CompilerParams(dimension_semantics=sem, vmem_limit_bytes=vmem)


def _dot(a, b):
    return jnp.dot(a, b, preferred_element_type=F32)


def _dot_nt(a, b):
    return lax.dot_general(a, b, (((1,), (1,)), ((), ())), preferred_element_type=F32)


def _dot_tn(a, b):
    return lax.dot_general(a, b, (((0,), (0,)), ((), ())), preferred_element_type=F32)


def _sigmoid(x):
    return 1.0 / (1.0 + jnp.exp(-x))


def _softplus(x):
    return jnp.maximum(x, 0.0) + jnp.log1p(jnp.exp(-jnp.abs(x)))


def _log_sigmoid(x):
    return -_softplus(-x)


PREP_ROWS = ROW_TILE
CAST_ROWS = 64
F32_SUBLANES = 8


def _segment_blocks(segments):
    assert all(a % F32_SUBLANES == 0 and (b - a) % PREP_ROWS == 0 for a, b in segments)
    return sum((b - a) // PREP_ROWS for a, b in segments)


def _segment_row(step, segments):
    s = jnp.int32(0)
    base = 0
    for a, b in segments:
        n = (b - a) // PREP_ROWS
        s = jnp.where((step >= base) & (step < base + n),
                      a // F32_SUBLANES + (step - base) * (PREP_ROWS // F32_SUBLANES), s)
        base += n
    return s * F32_SUBLANES


def _win_prep_kernel(w_ref, cat_ref):
    cat_ref[...] = w_ref[0].astype(BF16)


def _win_prep(w_in_t, layer, segments):
    d = w_in_t.shape[2]
    n_blk = _segment_blocks(segments)
    return pl.pallas_call(
        _win_prep_kernel, grid=(n_blk,),
        in_specs=[pl.BlockSpec((pl.Element(1), pl.Element(PREP_ROWS), pl.Element(d)),
                               lambda j: (layer, _segment_row(j, segments), 0))],
        out_specs=pl.BlockSpec((PREP_ROWS, d), lambda j: (j, 0)),
        out_shape=jax.ShapeDtypeStruct((n_blk * PREP_ROWS, d), BF16),
        compiler_params=_params(("parallel",)), name="win_prep",
    )(w_in_t)


def _gate_prep_kernel(g_ref, gt_ref):
    g = g_ref[0].astype(BF16)
    gt_ref[...] = jnp.concatenate([g, jnp.zeros((LANES - g.shape[0], g.shape[1]), BF16)], axis=0)


def _gate_prep(w_in_t, gate_seg):
    depth, _, d = w_in_t.shape
    ga, gb = gate_seg
    return pl.pallas_call(
        _gate_prep_kernel, grid=(depth,),
        in_specs=[pl.BlockSpec((pl.Element(1), pl.Element(gb - ga), pl.Element(d)), lambda l: (l, ga, 0))],
        out_specs=pl.BlockSpec((None, LANES, d), lambda l: (l, 0, 0)),
        out_shape=jax.ShapeDtypeStruct((depth, LANES, d), BF16),
        compiler_params=_params(("parallel",)), name="gate_prep",
    )(w_in_t)


def _moe_w1_prep_kernel(w_ref, perm_ref, o_ref, *, ff):
    parts = [_dot(w_ref[0, s].astype(BF16), perm_ref[...]) for s in range(2)]
    o_ref[0] = jnp.concatenate([parts[0][:, :ff], parts[1][:, :ff], parts[0][:, ff:], parts[1][:, ff:]],
                               axis=1).astype(BF16)


def _moe_w1_prep(moe_w1):
    depth, n_exp, d, two_ff = moe_w1.shape
    ff = two_ff // 2
    rows = 1024
    src = np.concatenate([np.arange(0, two_ff, 2), np.arange(1, two_ff, 2)])
    perm = jnp.asarray(np.arange(two_ff)[:, None] == src[None, :], BF16)
    kern = functools.partial(_moe_w1_prep_kernel, ff=ff)
    return pl.pallas_call(
        kern, grid=(depth * n_exp // 2, d // rows),
        in_specs=[pl.BlockSpec((1, 2, rows, two_ff), lambda e, i: (e, 0, i, 0)),
                  pl.BlockSpec((two_ff, two_ff), lambda e, i: (0, 0))],
        out_specs=pl.BlockSpec((1, rows, 2 * two_ff), lambda e, i: (e, i, 0)),
        out_shape=jax.ShapeDtypeStruct((depth * n_exp // 2, d, 2 * two_ff), BF16),
        compiler_params=_params(("parallel", "parallel")), name="moe_w1_prep",
    )(moe_w1.reshape(depth * n_exp // 2, 2, d, two_ff), perm)


def _cast_kernel(*refs):
    n = len(refs) // 2
    for src, dst in zip(refs[:n], refs[n:]):
        dst[...] = src[...].astype(dst.dtype)


def _cast_bf16(layer, *arrays):
    _, rows, cols = arrays[0].shape
    tm = ROW_TILE
    return pl.pallas_call(
        _cast_kernel, grid=(rows // tm,),
        in_specs=[pl.BlockSpec((None, tm, cols), lambda i: (layer, i, 0))] * len(arrays),
        out_specs=[pl.BlockSpec((tm, cols), lambda i: (i, 0))] * len(arrays),
        out_shape=[jax.ShapeDtypeStruct((rows, cols), BF16)] * len(arrays),
        compiler_params=_params(("parallel",)), name="cast_bf16",
    )(*arrays)


def _ada_kernel(cv_ref, down_ref, up_ref, bias_ref, out_ref):
    cv = cv_ref[...]
    a = cv * _sigmoid(cv)
    z = _dot(a.astype(BF16), down_ref[0].astype(BF16))
    out_ref[0] = _dot(z.astype(BF16), up_ref[0].astype(BF16)) + bias_ref[0]


def _ada_modulation(cvecs, ada_down, ada_up, ada_bias):
    depth, d, r = ada_down.shape
    rows = cvecs.shape[0]
    return pl.pallas_call(
        _ada_kernel,
        grid=(depth, 6),
        in_specs=[
            pl.BlockSpec((rows, d), lambda l, j: (0, 0)),
            pl.BlockSpec((1, d, r), lambda l, j: (l, 0, 0)),
            pl.BlockSpec((1, r, d), lambda l, j: (l, 0, j)),
            pl.BlockSpec((1, 1, d), lambda l, j: (l, 0, j)),
        ],
        out_specs=pl.BlockSpec((1, rows, d), lambda l, j: (l, 0, j)),
        out_shape=jax.ShapeDtypeStruct((depth, rows, 6 * d), F32),
        compiler_params=_params(("parallel", "arbitrary")),
        name="ada_modulation",
    )(cvecs, ada_down, ada_up, ada_bias.reshape(depth, 1, 6 * d))


def _rmsnorm(x, gain):
    return x * lax.rsqrt(jnp.mean(x * x, axis=-1, keepdims=True) + EPS) * gain


def _resid_norm_kernel(*refs, split_src, has_delta, has_router, tiles_per_seq, ctx_tiles):
    refs = list(refs)
    tile = pl.program_id(0)
    b = tile // tiles_per_seq
    j = tile - b * tiles_per_seq
    is_ctx = j < ctx_tiles
    row = jnp.where(is_ctx, 0, 1 + b)
    if split_src:
        c_ref = refs.pop(0)
        x_ref = refs.pop(0)
        z = jnp.where(is_ctx, c_ref[...], x_ref[...])
    else:
        z = refs.pop(0)[...]
    if has_delta:
        delta_ref = refs.pop(0)
        gate_ref = refs.pop(0)
    gain_ref = refs.pop(0)
    shift_ref = refs.pop(0)
    scale_ref = refs.pop(0)
    if has_router:
        rw_ref = refs.pop(0)
        rb_ref = refs.pop(0)
    if has_delta:
        znew_ref = refs.pop(0)
    h_ref = refs.pop(0)
    if has_router:
        gates_ref = refs.pop(0)

    if has_delta:
        z = z + gate_ref[pl.ds(row, 1), :] * delta_ref[...].astype(F32)
        znew_ref[...] = z
    y = _rmsnorm(z, gain_ref[...])
    h = y * (1.0 + scale_ref[pl.ds(row, 1), :]) + shift_ref[pl.ds(row, 1), :]
    hb = h.astype(BF16)
    h_ref[...] = hb
    if has_router:
        logits = _dot(hb, rw_ref[...]) + rb_ref[...]
        lane = lax.broadcasted_iota(jnp.int32, logits.shape, 1).astype(F32)
        gates = jnp.zeros_like(logits)
        den = jnp.zeros((logits.shape[0], 1), F32)
        top = None
        for _ in range(TOP_K):
            m = jnp.max(logits, axis=-1, keepdims=True)
            idx = jnp.min(jnp.where(logits == m, lane, float(LANES)), axis=-1, keepdims=True)
            sel = lane == idx
            if top is None:
                top = m
            e = jnp.exp(m - top)
            den = den + e
            gates = jnp.where(sel, e, gates)
            logits = jnp.where(sel, NEG * 2.0, logits)
        gates_ref[...] = gates / den


def _resid_norm(src, gain, mod, shift_slot, scale_slot, batch, seq_rows, ctx_rows, delta=None, gate_mod=None,
                gate_slot=None, router=None):
    split_src = isinstance(src, tuple)
    d = src[0].shape[1] if split_src else src.shape[1]
    t = batch * seq_rows
    tm = ROW_TILE
    tiles_per_seq = seq_rows // tm
    ctx_tiles = ctx_rows // tm
    lat_tiles = tiles_per_seq - ctx_tiles
    has_delta = delta is not None
    has_router = router is not None
    row_spec = pl.BlockSpec((tm, d), lambda i: (i, 0))
    vec_spec = pl.BlockSpec((1, d), lambda i: (0, 0))

    def mod_spec(slot):
        return pl.BlockSpec((mod.shape[0], d), lambda i: (0, slot))

    def ctx_map(i):
        b = i // tiles_per_seq
        return b * ctx_tiles + jnp.minimum(i - b * tiles_per_seq, ctx_tiles - 1), 0

    def lat_map(i):
        b = i // tiles_per_seq
        return b * lat_tiles + jnp.maximum(i - b * tiles_per_seq - ctx_tiles, 0), 0

    if split_src:
        operands, in_specs = list(src), [pl.BlockSpec((tm, d), ctx_map), pl.BlockSpec((tm, d), lat_map)]
    else:
        operands, in_specs = [src], [row_spec]
    if has_delta:
        operands += [delta, gate_mod]
        in_specs += [row_spec, mod_spec(gate_slot)]
    operands += [gain.reshape(1, d), mod, mod]
    in_specs += [vec_spec, mod_spec(shift_slot), mod_spec(scale_slot)]
    if has_router:
        rw, rb = router
        operands += [rw, rb]
        in_specs += [pl.BlockSpec(rw.shape, lambda i: (0, 0)), pl.BlockSpec(rb.shape, lambda i: (0, 0))]
    out_shape, out_specs = [], []
    if has_delta:
        out_shape.append(jax.ShapeDtypeStruct((t, d), F32))
        out_specs.append(row_spec)
    out_shape.append(jax.ShapeDtypeStruct((t, d), BF16))
    out_specs.append(row_spec)
    if has_router:
        out_shape.append(jax.ShapeDtypeStruct((t, LANES), F32))
        out_specs.append(pl.BlockSpec((tm, LANES), lambda i: (i, 0)))
    kern = functools.partial(_resid_norm_kernel, split_src=split_src, has_delta=has_delta, has_router=has_router,
                             tiles_per_seq=tiles_per_seq, ctx_tiles=ctx_tiles)
    return pl.pallas_call(
        kern, grid=(t // tm,), in_specs=in_specs, out_specs=out_specs, out_shape=out_shape,
        compiler_params=_params(("parallel",)), name="resid_norm",
    )(*operands)


def _final_norm_kernel(z_ref, delta_ref, gate_ref, gain_ref, out_ref):
    row = 1 + pl.program_id(0)
    z = z_ref[...] + gate_ref[pl.ds(row, 1), :] * delta_ref[...].astype(F32)
    out_ref[...] = _rmsnorm(z, gain_ref[...])


def _final_norm(z, delta, gate_mod, gate_slot, gain, batch, seq_rows, ctx_rows):
    t, d = z.shape
    tm = ROW_TILE
    lat_tiles = (seq_rows - ctx_rows) // tm
    tiles_per_seq = seq_rows // tm
    ctx_tiles = ctx_rows // tm
    row_spec = pl.BlockSpec((tm, d), lambda b, j: (b * tiles_per_seq + ctx_tiles + j, 0))
    return pl.pallas_call(
        _final_norm_kernel,
        grid=(batch, lat_tiles),
        in_specs=[row_spec, row_spec,
                  pl.BlockSpec((gate_mod.shape[0], d), lambda b, j: (0, gate_slot)),
                  pl.BlockSpec((1, d), lambda b, j: (0, 0))],
        out_specs=pl.BlockSpec((tm, d), lambda b, j: (b * lat_tiles + j, 0)),
        out_shape=jax.ShapeDtypeStruct((batch * lat_tiles * tm, d), F32),
        compiler_params=_params(("parallel", "parallel")), name="final_norm",
    )(z, delta, gate_mod, gain.reshape(1, d))


def _matmul_kernel(a_ref, w_ref, o_ref, *, w_transposed):
    dot = _dot_nt if w_transposed else _dot
    o_ref[...] = dot(a_ref[...], w_ref[...]).astype(o_ref.dtype)


def _matmul(a, w, layer, out_dtype, tm, tn, w_transposed=False):
    m, k = a.shape
    if w_transposed:
        n = w.shape[1]
        w_spec = pl.BlockSpec((None, tn, k), lambda i, j: (layer, j, 0))
    else:
        n = w.shape[2]
        w_spec = pl.BlockSpec((None, k, tn), lambda i, j: (layer, 0, j))
    return pl.pallas_call(
        functools.partial(_matmul_kernel, w_transposed=w_transposed),
        grid=(m // tm, n // tn),
        in_specs=[pl.BlockSpec((tm, k), lambda i, j: (i, 0)), w_spec],
        out_specs=pl.BlockSpec((tm, tn), lambda i, j: (i, j)),
        out_shape=jax.ShapeDtypeStruct((m, n), out_dtype),
        compiler_params=_params(("parallel", "arbitrary")), name="matmul",
    )(a, w)


def _proj_prep_kernel(a_ref, w_ref, src_ref, o_ref, dst_ref, *, n_prep):
    o_ref[...] = _dot_nt(a_ref[...], w_ref[...]).astype(o_ref.dtype)

    @pl.when(pl.program_id(0) * pl.num_programs(1) + pl.program_id(1) < n_prep)
    def _():
        dst_ref[...] = src_ref[0].astype(dst_ref.dtype)


def _proj_prep(a, w_t, w_in_t, next_layer, segments, tm, tn):
    m, k = a.shape
    n = w_t.shape[0]
    n_blk = _segment_blocks(segments)
    nj = n // tn
    assert (m // tm) * nj >= n_blk

    def prep_block(i, j):
        return jnp.minimum(i * nj + j, n_blk - 1)

    return pl.pallas_call(
        functools.partial(_proj_prep_kernel, n_prep=n_blk),
        grid=(m // tm, nj),
        in_specs=[pl.BlockSpec((tm, k), lambda i, j: (i, 0)),
                  pl.BlockSpec((tn, k), lambda i, j: (j, 0)),
                  pl.BlockSpec((pl.Element(1), pl.Element(PREP_ROWS), pl.Element(k)),
                               lambda i, j: (next_layer, _segment_row(prep_block(i, j), segments), 0))],
        out_specs=[pl.BlockSpec((tm, tn), lambda i, j: (i, j)),
                   pl.BlockSpec((PREP_ROWS, k), lambda i, j: (prep_block(i, j), 0))],
        out_shape=[jax.ShapeDtypeStruct((m, n), BF16), jax.ShapeDtypeStruct((n_blk * PREP_ROWS, k), BF16)],
        compiler_params=_params(("arbitrary", "arbitrary")), name="proj_prep",
    )(a, w_t, w_in_t)


def _gate_proj_kernel(a_ref, wt_ref, oc_ref, or_ref):
    a = a_ref[...]
    wt = wt_ref[...]
    oc_ref[...] = _dot_nt(a, wt)
    or_ref[...] = _dot_nt(wt, a)


def _gate_proj(a, wt, layer, tm):
    m, k = a.shape
    return pl.pallas_call(
        _gate_proj_kernel,
        grid=(m // tm,),
        in_specs=[pl.BlockSpec((tm, k), lambda i: (i, 0)),
                  pl.BlockSpec((None, LANES, k), lambda i: (layer, 0, 0))],
        out_specs=[pl.BlockSpec((tm, LANES), lambda i: (i, 0)), pl.BlockSpec((LANES, tm), lambda i: (0, i))],
        out_shape=[jax.ShapeDtypeStruct((m, LANES), F32), jax.ShapeDtypeStruct((LANES, m), F32)],
        compiler_params=_params(("parallel",)), name="gate_proj",
    )(a, wt)


def _tri_masks(length, reverse):
    row = lax.broadcasted_iota(jnp.int32, (length, length), 0)
    col = lax.broadcasted_iota(jnp.int32, (length, length), 1)
    if reverse:
        return col >= row, row >= col
    return col <= row, row <= col


def _mlstm_kernel(q_ref, k_ref, v_ref, gc_ref, gr_ref, bc_ref, br_ref, h_ref, ct_ref, n_ref, m_ref, *,
                  reverse, heads, dqk, dv):
    @pl.when(pl.program_id(1) == 0)
    def _():
        ct_ref[...] = jnp.zeros_like(ct_ref)
        n_ref[...] = jnp.zeros_like(n_ref)
        m_ref[...] = jnp.zeros_like(m_ref)

    length = q_ref.shape[0]
    valid, valid_t = _tri_masks(length, reverse)
    gc = gc_ref[...] + bc_ref[...]
    gr = gr_ref[...] + br_ref[...]
    d0 = 2 * heads if reverse else 0
    scale = dqk ** -0.5
    for h in range(heads):
        i_col = gc[:, d0 + h:d0 + h + 1]
        f_col = _log_sigmoid(gc[:, d0 + heads + h:d0 + heads + h + 1])
        i_row = gr[d0 + h:d0 + h + 1, :]
        f_row = _log_sigmoid(gr[d0 + heads + h:d0 + heads + h + 1, :])
        b_col = jnp.sum(jnp.where(valid, f_row, 0.0), axis=1, keepdims=True)
        b_row = jnp.sum(jnp.where(valid_t, f_col, 0.0), axis=0, keepdims=True)
        b_end = jnp.sum(f_row, axis=1, keepdims=True)
        m_prev = m_ref[h][:, :1]
        log_d = jnp.where(valid, b_col - b_row + i_row, NEG)
        m_inter = b_col + m_prev
        m_t = jnp.maximum(m_inter, jnp.max(log_d, axis=1, keepdims=True))
        w_intra = jnp.exp(log_d - m_t) * scale
        w_inter = jnp.exp(m_inter - m_t) * scale
        q = q_ref[:, h * dqk:(h + 1) * dqk]
        k = k_ref[:, h * dqk:(h + 1) * dqk]
        v = v_ref[:, h * dv:(h + 1) * dv]
        s = _dot_nt(q, k) * w_intra
        num = _dot(s.astype(BF16), v) + w_inter * _dot(q, ct_ref[h].astype(BF16))
        qn = jnp.sum(q.astype(F32) * n_ref[h], axis=1, keepdims=True)
        den = jnp.sum(s, axis=1, keepdims=True) + w_inter * qn
        h_ref[:, h * dv:(h + 1) * dv] = (num / jnp.maximum(jnp.abs(den), jnp.exp(-m_t))).astype(h_ref.dtype)
        log_w = b_end - b_col + i_col
        m_new = jnp.maximum(b_end + m_prev, jnp.max(log_w, axis=0, keepdims=True))
        w_end = jnp.exp(log_w - m_new)
        decay = jnp.exp(b_end + m_prev - m_new)
        wv = (w_end * v.astype(F32)).astype(BF16)
        ct_ref[h] = decay * ct_ref[h] + _dot_tn(k, wv)
        n_ref[h] = decay * n_ref[h] + jnp.sum(w_end * k.astype(F32), axis=0, keepdims=True)
        m_ref[h] = jnp.broadcast_to(m_new, (1, LANES))


def _chunk_index(i, n_chunks, reverse):
    if not reverse:
        return i
    return jnp.where(i == 0, 0, n_chunks - i)


def _mlstm(p, col0, dv, gates_c, gates_r, bias_c, bias_r, batch, seq_rows, reverse):
    t = p.shape[0]
    length = ROW_TILE
    n_chunks = seq_rows // length
    heads = A_HEADS
    dqk = dv // 2
    qw, vw = heads * dqk, heads * dv

    def rows(b, i):
        return b * n_chunks + _chunk_index(i, n_chunks, reverse)

    kern = functools.partial(_mlstm_kernel, reverse=reverse, heads=heads, dqk=dqk, dv=dv)
    return pl.pallas_call(
        kern,
        grid=(batch, n_chunks),
        in_specs=[
            pl.BlockSpec((length, qw), lambda b, i: (rows(b, i), col0 // qw)),
            pl.BlockSpec((length, qw), lambda b, i: (rows(b, i), col0 // qw + 1)),
            pl.BlockSpec((length, vw), lambda b, i: (rows(b, i), (col0 + 2 * qw) // vw)),
            pl.BlockSpec((length, LANES), lambda b, i: (rows(b, i), 0)),
            pl.BlockSpec((4 * heads, length), lambda b, i: (0, rows(b, i))),
            pl.BlockSpec((1, LANES), lambda b, i: (0, 0)),
            pl.BlockSpec((4 * heads, 1), lambda b, i: (0, 0)),
        ],
        out_specs=pl.BlockSpec((length, vw), lambda b, i: (rows(b, i), 0)),
        out_shape=jax.ShapeDtypeStruct((t, vw), BF16),
        scratch_shapes=[pltpu.VMEM((heads, dqk, dv), F32), pltpu.VMEM((heads, 1, dqk), F32),
                        pltpu.VMEM((heads, 1, LANES), F32)],
        compiler_params=_params(("parallel", "arbitrary")), name="mlstm_scan",
    )(p, p, p, gates_c, gates_r, bias_c, bias_r)


def _rope(t, cosf, sinf, half):
    if 2 * half == LANES:
        partner = pltpu.roll(t, half, axis=1)
    else:
        lane = lax.broadcasted_iota(jnp.int32, t.shape, 1)
        partner = jnp.where((lane & (2 * half - 1)) < half, pltpu.roll(t, LANES - half, axis=1),
                            pltpu.roll(t, half, axis=1))
    return t * cosf + partner * sinf


def _retention_kernel(q_ref, k_ref, v_ref, cos_ref, sin_ref, dp_ref, o_ref, s_ref, *, reverse, heads, dk, dv):
    @pl.when(pl.program_id(1) == 0)
    def _():
        s_ref[...] = jnp.zeros_like(s_ref)

    length = q_ref.shape[0]
    row = lax.broadcasted_iota(jnp.int32, (length, length), 0)
    col = lax.broadcasted_iota(jnp.int32, (length, length), 1)
    diff = (col - row if reverse else row - col).astype(F32)
    tpos = lax.broadcasted_iota(jnp.int32, (length, 1), 0).astype(F32)
    lg_all = -_softplus(dp_ref[...])
    cosf = cos_ref[...]
    sinf = sin_ref[...]
    scale = dk ** -0.5
    d = 1 if reverse else 0
    for h in range(heads):
        lg = lg_all[d:d + 1, h:h + 1]
        decay_mat = jnp.where(diff >= 0.0, jnp.exp(lg * jnp.maximum(diff, 0.0)), 0.0) * scale
        if reverse:
            read_w = jnp.exp(lg * (length - tpos))
            write_w = jnp.exp(lg * tpos)
        else:
            read_w = jnp.exp(lg * (tpos + 1.0))
            write_w = jnp.exp(lg * (length - 1.0 - tpos))
        chunk_decay = jnp.exp(lg * float(length))
        q = _rope(q_ref[:, h * dk:(h + 1) * dk].astype(F32), cosf, sinf, dk // 2).astype(BF16)
        k = _rope(k_ref[:, h * dk:(h + 1) * dk].astype(F32), cosf, sinf, dk // 2).astype(BF16)
        v = v_ref[:, h * dv:(h + 1) * dv]
        s = _dot_nt(q, k) * decay_mat
        o_ref[:, h * dv:(h + 1) * dv] = (_dot(s.astype(BF16), v)
                                         + read_w * _dot(q, s_ref[h].astype(BF16))).astype(o_ref.dtype)
        wv = (write_w * v.astype(F32)).astype(BF16)
        s_ref[h] = chunk_decay * s_ref[h] + scale * _dot_tn(k, wv)


def _retention(p, col0, dv, cosf, sinf, decay_p, batch, seq_rows, reverse):
    t = p.shape[0]
    length = ROW_TILE
    n_chunks = seq_rows // length
    heads = B_HEADS
    dk = dv // 2
    assert dk == LANES
    qw, vw = heads * dk, heads * dv

    def rows(b, i):
        return b * n_chunks + _chunk_index(i, n_chunks, reverse)

    kern = functools.partial(_retention_kernel, reverse=reverse, heads=heads, dk=dk, dv=dv)
    return pl.pallas_call(
        kern,
        grid=(batch, n_chunks),
        in_specs=[
            pl.BlockSpec((length, qw), lambda b, i: (rows(b, i), col0 // qw)),
            pl.BlockSpec((length, qw), lambda b, i: (rows(b, i), col0 // qw + 1)),
            pl.BlockSpec((length, vw), lambda b, i: (rows(b, i), (col0 + 2 * qw) // vw)),
            pl.BlockSpec((length, LANES), lambda b, i: (_chunk_index(i, n_chunks, reverse), 0)),
            pl.BlockSpec((length, LANES), lambda b, i: (_chunk_index(i, n_chunks, reverse), 0)),
            pl.BlockSpec(decay_p.shape, lambda b, i: (0, 0)),
        ],
        out_specs=pl.BlockSpec((length, vw), lambda b, i: (rows(b, i), 0)),
        out_shape=jax.ShapeDtypeStruct((t, vw), BF16),
        scratch_shapes=[pltpu.VMEM((heads, dk, dv), F32)],
        compiler_params=_params(("parallel", "arbitrary")), name="retention_scan",
    )(p, p, p, cosf, sinf, decay_p)


def _headnorm_kernel(hf_ref, hb_ref, gate_ref, gain_ref, y_ref, *, heads, dv, center, silu_gate):
    g = gate_ref[...].astype(F32)
    sg = _sigmoid(g)
    gate = g * sg if silu_gate else sg
    gain = gain_ref[...]
    for h in range(heads):
        sl = slice(h * dv, (h + 1) * dv)
        x = hf_ref[:, sl].astype(F32) + hb_ref[:, sl].astype(F32)
        if center:
            x = x - jnp.mean(x, axis=-1, keepdims=True)
        y = x * lax.rsqrt(jnp.mean(x * x, axis=-1, keepdims=True) + EPS)
        y_ref[:, sl] = (gate[:, sl] * (y * gain[:, sl])).astype(y_ref.dtype)


def _headnorm(hf, hb, p, gate_col, gain, heads, center, silu_gate):
    t, w = hf.shape
    tm = _row_tile(t, 2 * ROW_TILE)
    kern = functools.partial(_headnorm_kernel, heads=heads, dv=w // heads, center=center, silu_gate=silu_gate)
    row_spec = pl.BlockSpec((tm, w), lambda i: (i, 0))
    return pl.pallas_call(
        kern, grid=(t // tm,),
        in_specs=[row_spec, row_spec, pl.BlockSpec((tm, w), lambda i: (i, gate_col // w)),
                  pl.BlockSpec((1, w), lambda i: (0, 0))],
        out_specs=row_spec,
        out_shape=jax.ShapeDtypeStruct((t, w), BF16),
        compiler_params=_params(("parallel",)), name="headnorm_gate",
    )(hf, hb, p, gain.reshape(1, w))


def _lane_low(shape, hd):
    return lax.broadcasted_iota(jnp.int32, shape, 1) < hd


def _stack_heads(q, hd):
    low = _lane_low(q.shape, hd)
    zero = jnp.zeros_like(q)
    return jnp.concatenate([jnp.where(low, q, zero), jnp.where(low, zero, q)], axis=0)


def _unstack_heads(o, hd):
    rows = o.shape[0] // 2
    return jnp.where(_lane_low((rows, LANES), hd), o[:rows], o[rows:])


def _window_kernel(sink_ref, q_ref, kp_ref, k0_ref, kn_ref, vp_ref, v0_ref, vn_ref, kc_ref, vc_ref,
                   cq_ref, sq_ref, cp_ref, sp_ref, cn_ref, sn_ref, o_ref, *, heads, kv_heads, hd, ctx_blocks,
                   n_latent, slabs_per_dot):
    i = pl.program_id(1)
    j = i - ctx_blocks
    w = q_ref.shape[0]
    half = hd // 2
    slabs_per_group = heads // kv_heads // 2

    def roped(x_ref, c_ref, s_ref):
        return _rope(x_ref[...].astype(F32), c_ref[...], s_ref[...], half)

    kb = jnp.concatenate([roped(kp_ref, cp_ref, sp_ref), roped(k0_ref, cq_ref, sq_ref),
                          roped(kn_ref, cn_ref, sn_ref)], axis=0)
    vb = jnp.concatenate([vp_ref[...], v0_ref[...], vn_ref[...]], axis=0).astype(F32)
    kc = kc_ref[...].astype(F32)
    vc = vc_ref[...].astype(F32)

    def group_copy(x, g):
        keep = _lane_low(x.shape, hd) if g == 0 else jnp.logical_not(_lane_low(x.shape, hd))
        return jnp.where(keep, x, pltpu.roll(x, hd, axis=1)).astype(BF16)

    c = lax.broadcasted_iota(jnp.int32, (3 * w, w), 0)
    t = lax.broadcasted_iota(jnp.int32, (3 * w, w), 1)
    lower = jnp.maximum(t + (w - WINDOW), (1 - j) * w)
    upper = jnp.minimum(t + (w + WINDOW), n_latent - 1 - (j - 1) * w)
    upper = jnp.where(j < 0, -1, upper)
    mask_bias = jnp.where((c >= lower) & (c <= upper), 0.0, NEG)
    mask_bias = jnp.concatenate([mask_bias] * (2 * slabs_per_dot), axis=1)
    scale = hd ** -0.5
    cq = cq_ref[...]
    sq = sq_ref[...]
    for g in range(kv_heads):
        kk, vv, kkc, vvc = group_copy(kb, g), group_copy(vb, g), group_copy(kc, g), group_copy(vc, g)
        for s0 in range(0, slabs_per_group, slabs_per_dot):
            slabs = [g * slabs_per_group + s0 + s for s in range(slabs_per_dot)]
            qs = []
            sinks = []
            for slab in slabs:
                qr = _rope(q_ref[:, slab * LANES:(slab + 1) * LANES].astype(F32), cq, sq, half)
                qs.append(_stack_heads((qr * scale).astype(BF16), hd))
                sinks += [jnp.full((1, w), sink_ref[2 * slab], F32), jnp.full((1, w), sink_ref[2 * slab + 1], F32)]
            qq = jnp.concatenate(qs, axis=0)
            sink = jnp.concatenate(sinks, axis=1)
            s_loc = _dot_nt(kk, qq) + mask_bias
            s_ctx = _dot_nt(kkc, qq)
            m = jnp.maximum(jnp.maximum(jnp.max(s_loc, axis=0, keepdims=True),
                                        jnp.max(s_ctx, axis=0, keepdims=True)), sink)
            e_loc = jnp.exp(s_loc - m)
            e_ctx = jnp.exp(s_ctx - m)
            den = (jnp.exp(sink - m) + jnp.sum(e_loc, axis=0, keepdims=True)
                   + jnp.sum(e_ctx, axis=0, keepdims=True))
            o_t = (_dot_tn(vv, e_loc.astype(BF16)) + _dot_tn(vvc, e_ctx.astype(BF16))) / den
            o = o_t.T
            for n, slab in enumerate(slabs):
                o_ref[:, slab * LANES:(slab + 1) * LANES] = _unstack_heads(
                    o[2 * n * w:2 * (n + 1) * w], hd).astype(o_ref.dtype)


def _window_attention(p, col0, cosf, sinf, sink, batch, seq_rows, ctx_rows):
    t = p.shape[0]
    w = WINDOW
    heads, kv_heads = C_HEADS, C_KV_HEADS
    hd = LANES // 2
    assert kv_heads * hd == LANES
    qw = heads * hd
    nblk = seq_rows // w
    ctx_blocks = ctx_rows // w
    kcol = (col0 + qw) // LANES
    vcol = kcol + 1

    def cur(b, i):
        return b * nblk + i

    def prev(b, i):
        return b * nblk + jnp.maximum(i - 1, 0)

    def nxt(b, i):
        return b * nblk + jnp.minimum(i + 1, nblk - 1)

    def kv_spec(fn, colblk):
        return pl.BlockSpec((w, LANES), lambda b, i: (fn(b, i), colblk))

    def tab_spec(fn):
        return pl.BlockSpec((w, LANES), lambda b, i: (fn(0, i), 0))

    kern = functools.partial(_window_kernel, heads=heads, kv_heads=kv_heads, hd=hd, ctx_blocks=ctx_blocks,
                             n_latent=seq_rows - ctx_rows, slabs_per_dot=4)
    ctx_spec_k = pl.BlockSpec((ctx_rows, LANES), lambda b, i: (b * (seq_rows // ctx_rows), kcol))
    ctx_spec_v = pl.BlockSpec((ctx_rows, LANES), lambda b, i: (b * (seq_rows // ctx_rows), vcol))
    return pl.pallas_call(
        kern,
        grid=(batch, nblk),
        in_specs=[
            pl.BlockSpec(memory_space=pltpu.SMEM),
            pl.BlockSpec((w, qw), lambda b, i: (cur(b, i), col0 // qw)),
            kv_spec(prev, kcol), kv_spec(cur, kcol), kv_spec(nxt, kcol),
            kv_spec(prev, vcol), kv_spec(cur, vcol), kv_spec(nxt, vcol),
            ctx_spec_k, ctx_spec_v,
            tab_spec(cur), tab_spec(cur), tab_spec(prev), tab_spec(prev), tab_spec(nxt), tab_spec(nxt),
        ],
        out_specs=pl.BlockSpec((w, qw), lambda b, i: (cur(b, i), 0)),
        out_shape=jax.ShapeDtypeStruct((t, qw), BF16),
        compiler_params=_params(("parallel", "arbitrary")), name="window_attention",
    )(sink, p, p, p, p, p, p, p, p, p, cosf, sinf, cosf, sinf, cosf, sinf)


def _nbr_layout():
    masked = 2 * NA_ROWS - 1
    variants = [(0, [0] * NBR_ROWS), (NA_ROWS // 2, list(range(NBR_ROWS))),
                (NBR_UNION - NBR_ROWS, [NBR_UNION - NA_ROWS] * NBR_ROWS)]
    pairs, index = [], []
    for delta, rel_start in variants:
        per_kr = []
        for kr in range(NBR_UNION):
            codes = [kr - (delta + rr) + NA_ROWS - 1 if rel_start[rr] <= kr < rel_start[rr] + NA_ROWS else masked
                     for rr in range(NBR_ROWS)]
            row = []
            for a in range(NBR_ROWS // 2):
                pair = (codes[2 * a], codes[2 * a + 1])
                if pair not in pairs:
                    pairs.append(pair)
                row.append(pairs.index(pair))
            per_kr.append(row)
        index.append(per_kr)
    return pairs, index


def _nbr_kernel(q_ref, k_ref, v_ref, tab_ref, o_ref, bias_ref, *, hd, ctx_rows, grid_rows, index, slabs_per_step):
    i = pl.program_id(2)
    scale = hd ** -0.5
    tq = q_ref.shape[0]
    lanes = [slice(sl * LANES, (sl + 1) * LANES) for sl in range(slabs_per_step)]
    qqs = [_stack_heads(q_ref[:, ln] * scale, hd) for ln in lanes]
    kcs = [k_ref[0:ctx_rows, ln] for ln in lanes]
    vcs = [v_ref[0:ctx_rows, ln] for ln in lanes]
    s_ctxs = [_dot_nt(kc, qq) for kc, qq in zip(kcs, qqs)]
    m_ctxs = [jnp.max(s, axis=0, keepdims=True) for s in s_ctxs]

    @pl.when(i == 0)
    def _():
        for sl, ln in enumerate(lanes):
            for variant, per_kr in enumerate(index):
                for kr, row in enumerate(per_kr):
                    for sub in range(2):
                        for a, u in enumerate(row):
                            bias_ref[sl, variant, kr * GRID_W:(kr + 1) * GRID_W,
                                     sub * tq + a * LANES:sub * tq + (a + 1) * LANES] = tab_ref[2 * sl + sub, u]
            e = jnp.exp(s_ctxs[sl] - m_ctxs[sl])
            o_t = _dot_tn(vcs[sl], e.astype(BF16)) / jnp.sum(e, axis=0, keepdims=True)
            o_ref[:, ln] = _unstack_heads(o_t.T, hd).astype(o_ref.dtype)

    @pl.when(i > 0)
    def _():
        blk = i - 1
        n_blk = grid_rows // NBR_ROWS
        u0 = jnp.clip(blk * NBR_ROWS - NA_ROWS // 2, 0, grid_rows - NBR_UNION)
        variant = jnp.where(blk == 0, 0, jnp.where(blk == n_blk - 1, 2, 1))
        base = pl.multiple_of(ctx_rows + u0 * GRID_W, GRID_W)
        for sl, ln in enumerate(lanes):
            kw = k_ref[pl.ds(base, NBR_UNION * GRID_W), ln]
            vw = v_ref[pl.ds(base, NBR_UNION * GRID_W), ln]
            s_loc = _dot_nt(kw, qqs[sl]) + bias_ref[sl, variant]
            m = jnp.maximum(jnp.max(s_loc, axis=0, keepdims=True), m_ctxs[sl])
            e_loc = jnp.exp(s_loc - m)
            e_ctx = jnp.exp(s_ctxs[sl] - m)
            den = jnp.sum(e_loc, axis=0, keepdims=True) + jnp.sum(e_ctx, axis=0, keepdims=True)
            o_t = (_dot_tn(vw, e_loc.astype(BF16)) + _dot_tn(vcs[sl], e_ctx.astype(BF16))) / den
            o_ref[:, ln] = _unstack_heads(o_t.T, hd).astype(o_ref.dtype)


def _nbr_table(na_bias_l, pairs):
    col = jnp.arange(GRID_W)
    col_start = jnp.clip(col - NA_COLS // 2, 0, GRID_W - NA_COLS)
    col_valid = (col[None, :] >= col_start[:, None]) & (col[None, :] < col_start[:, None] + NA_COLS)
    dc = jnp.clip(col[None, :] - col[:, None], -(NA_COLS - 1), NA_COLS - 1) + (NA_COLS - 1)
    blocks = jnp.where(col_valid.T[None, None], na_bias_l[:, :, dc.T], NEG)
    blocks = jnp.concatenate([blocks, jnp.full_like(blocks[:, :1], NEG)], axis=1)
    left = blocks[:, np.array([p[0] for p in pairs])]
    right = blocks[:, np.array([p[1] for p in pairs])]
    return jnp.concatenate([left, right], axis=-1).astype(F32)


def _nbr_attention(p, col0, na_bias_l, batch, seq_rows, ctx_rows):
    t = p.shape[0]
    heads = D_HEADS
    hd = LANES // 2
    qw = heads * hd
    slabs = heads // 2
    tq = NBR_ROWS * GRID_W
    n_latent = seq_rows - ctx_rows
    grid_rows = n_latent // GRID_W
    assert tq == ctx_rows and grid_rows % NBR_ROWS == 0 and grid_rows >= NBR_UNION + 1
    assert slabs % NBR_SLABS_PER_STEP == 0 and (col0 // LANES) % NBR_SLABS_PER_STEP == 0
    steps = seq_rows // tq
    q0 = col0 // LANES
    pairs, index = _nbr_layout()
    table = _nbr_table(na_bias_l, pairs)
    sps = NBR_SLABS_PER_STEP
    width = sps * LANES
    kern = functools.partial(_nbr_kernel, hd=hd, ctx_rows=ctx_rows, grid_rows=grid_rows, index=index,
                             slabs_per_step=sps)
    return pl.pallas_call(
        kern,
        grid=(batch, slabs // sps, steps),
        in_specs=[
            pl.BlockSpec((tq, width), lambda b, s, i: (b * steps + i, q0 // sps + s)),
            pl.BlockSpec((seq_rows, width), lambda b, s, i: (b, (q0 + slabs) // sps + s)),
            pl.BlockSpec((seq_rows, width), lambda b, s, i: (b, (q0 + 2 * slabs) // sps + s)),
            pl.BlockSpec((2 * sps,) + table.shape[1:], lambda b, s, i: (s, 0, 0, 0)),
        ],
        out_specs=pl.BlockSpec((tq, width), lambda b, s, i: (b * steps + i, s)),
        out_shape=jax.ShapeDtypeStruct((t, qw), BF16),
        scratch_shapes=[pltpu.VMEM((sps, len(index), NBR_UNION * GRID_W, 2 * tq), F32)],
        compiler_params=_params(("parallel", "parallel", "arbitrary")), name="nbr_attention",
    )(p, p, p, table)


def _merge_kernel(*refs, n_cast, cast_steps):
    ys = refs[0:N_BRANCH]
    gs = refs[N_BRANCH:2 * N_BRANCH]
    ws = refs[2 * N_BRANCH:3 * N_BRANCH]
    srcs = refs[3 * N_BRANCH:3 * N_BRANCH + n_cast]
    o_ref = refs[3 * N_BRANCH + n_cast]
    dsts = refs[3 * N_BRANCH + n_cast + 1:]
    acc = None
    for y_ref, g_ref, w_ref in zip(ys, gs, ws):
        term = _sigmoid(g_ref[...].astype(F32)) * _dot(y_ref[...], w_ref[...])
        acc = term if acc is None else acc + term
    o_ref[...] = acc.astype(o_ref.dtype)

    if n_cast:
        @pl.when(pl.program_id(0) * pl.num_programs(1) + pl.program_id(1) < cast_steps)
        def _():
            for src, dst in zip(srcs, dsts):
                dst[...] = src[...].astype(dst.dtype)


def _merge(ys, p, gate_col0, w_branch, tm, tn, cast_next=None):
    t, bw = ys[0].shape
    d = w_branch.shape[1]
    nt = d // tn
    steps = (t // tm) * nt
    y_spec = pl.BlockSpec((tm, bw), lambda i, j: (i, 0))
    g_specs = [pl.BlockSpec((tm, tn), functools.partial(lambda i, j, br: (i, (gate_col0 + br * d) // tn + j), br=br))
               for br in range(N_BRANCH)]
    w_specs = [pl.BlockSpec((bw, tn), functools.partial(lambda i, j, br: (br, j), br=br))
               for br in range(N_BRANCH)]
    in_specs = [y_spec] * N_BRANCH + g_specs + w_specs
    out_specs = [pl.BlockSpec((tm, tn), lambda i, j: (i, j))]
    out_shape = [jax.ShapeDtypeStruct((t, d), BF16)]
    operands = [*ys, *([p] * N_BRANCH), *([w_branch] * N_BRANCH)]
    n_cast = cast_steps = 0
    if cast_next is not None:
        layer, arrays = cast_next
        _, rows, cols = arrays[0].shape
        n_cast = len(arrays)
        cast_rows = CAST_ROWS
        while rows // cast_rows > steps:
            cast_rows *= 2
        cast_steps = rows // cast_rows

        def blk(i, j):
            return jnp.minimum(i * nt + j, cast_steps - 1)

        in_specs += [pl.BlockSpec((None, cast_rows, cols), lambda i, j: (layer, blk(i, j), 0))] * n_cast
        out_specs += [pl.BlockSpec((cast_rows, cols), lambda i, j: (blk(i, j), 0))] * n_cast
        out_shape += [jax.ShapeDtypeStruct((rows, cols), BF16)] * n_cast
        operands += list(arrays)
    outs = pl.pallas_call(
        functools.partial(_merge_kernel, n_cast=n_cast, cast_steps=cast_steps),
        grid=(t // tm, nt),
        in_specs=in_specs, out_specs=out_specs, out_shape=out_shape,
        compiler_params=_params(("arbitrary", "arbitrary")), name="merge_branches",
    )(*operands)
    return outs[0], outs[1:]


def _moe_kernel(t_ref, gates_ref, w1_ref, b1_ref, w2_ref, b2_ref, o_ref, acc_ref, *, ff, pairs):
    e = pl.program_id(1)
    gates = gates_ref[...]

    @pl.when(e == 0)
    def _():
        acc_ref[...] = _dot(gates.astype(BF16), b2_ref[...])

    tokens = t_ref[...]
    lane = lax.broadcasted_iota(jnp.int32, gates.shape, 1)
    acts = []
    for pr in range(pairs):
        hid = _dot(tokens, w1_ref[pr]) + b1_ref[pr]
        g_h = jnp.minimum(hid[:, :2 * ff], SWIGLU_LIMIT)
        u_h = jnp.clip(hid[:, 2 * ff:], -SWIGLU_LIMIT, SWIGLU_LIMIT)
        cols = []
        for s in range(2):
            expert = (e * pairs + pr) * 2 + s
            gsel = jnp.sum(jnp.where(lane == expert, gates, 0.0), axis=-1, keepdims=True)
            cols.append(jnp.broadcast_to(gsel, (gates.shape[0], ff)))
        act = g_h * _sigmoid(SWIGLU_ALPHA * g_h) * (u_h + 1.0) * jnp.concatenate(cols, axis=-1)
        acts.append(act.astype(BF16))
    acc_ref[...] += _dot(jnp.concatenate(acts, axis=-1), w2_ref[...])

    @pl.when(e == pl.num_programs(1) - 1)
    def _():
        o_ref[...] = acc_ref[...].astype(o_ref.dtype)


def _moe(h, gates, w1p, b1p, w2, b2p, pair0, tm, pairs):
    t, d = h.shape
    width = w1p.shape[2]
    ff = width // 4
    steps = b1p.shape[0] // pairs
    kern = functools.partial(_moe_kernel, ff=ff, pairs=pairs)
    return pl.pallas_call(
        kern,
        grid=(t // tm, steps),
        in_specs=[
            pl.BlockSpec((tm, d), lambda i, e: (i, 0)),
            pl.BlockSpec((tm, LANES), lambda i, e: (i, 0)),
            pl.BlockSpec((pairs, d, width), lambda i, e: (pair0 // pairs + e, 0, 0)),
            pl.BlockSpec((pairs, 1, width), lambda i, e: (e, 0, 0)),
            pl.BlockSpec((pairs * 2 * ff, d), lambda i, e: (e, 0)),
            pl.BlockSpec(b2p.shape, lambda i, e: (0, 0)),
        ],
        out_specs=pl.BlockSpec((tm, d), lambda i, e: (i, 0)),
        out_shape=jax.ShapeDtypeStruct((t, d), BF16),
        scratch_shapes=[pltpu.VMEM((tm, d), F32)],
        compiler_params=_params(("parallel", "arbitrary")), name="moe_experts",
    )(h, gates, w1p, b1p, w2, b2p)


def _rope_tables(n_latent, ctx_rows, hd):
    pos = np.arange(n_latent)
    row = (pos // GRID_W).astype(np.float32)
    col = (pos % GRID_W).astype(np.float32)
    n_freq = hd // 4
    inv = jnp.asarray(ROPE_BASE, F32) ** (-jnp.arange(n_freq, dtype=F32) / n_freq)
    ang = jnp.concatenate([jnp.asarray(row)[:, None] * inv, jnp.asarray(col)[:, None] * inv], axis=-1)
    ang = jnp.concatenate([jnp.zeros((ctx_rows, hd // 2), F32), ang], axis=0)
    cos, sin = jnp.cos(ang), jnp.sin(ang)
    reps = LANES // hd
    cosf = jnp.tile(jnp.concatenate([cos, cos], axis=-1), (1, reps))
    sinf = jnp.tile(jnp.concatenate([-sin, sin], axis=-1), (1, reps))
    return cosf, sinf


def _pad_lanes(a, value=0.0):
    pad = LANES - a.shape[-1]
    return jnp.pad(a, [(0, 0)] * (a.ndim - 1) + [(0, pad)], constant_values=value)


def kernel(x, c, ctx, c_ctx, ada_down, ada_up, ada_bias, norm_gain, w_in, mlstm_gate_bias, mlstm_norm_gain,
           ret_decay, ret_norm_gain, sink, na_bias, w_branch, w_out, router_w, router_b, moe_w1, moe_b1,
           moe_w2, moe_b2, final_gain):
    batch, n_latent, d = x.shape
    ctx_rows = ctx.shape[1]
    depth = w_in.shape[0]
    seq_rows = ctx_rows + n_latent
    t = batch * seq_rows
    bw = d // N_BRANCH
    n_experts = router_w.shape[2]
    ff = moe_w2.shape[2]

    a_dv, b_dv = bw // A_HEADS, bw // B_HEADS
    c_hd, d_hd = bw // C_HEADS, bw // D_HEADS
    assert c_hd == LANES // 2 and d_hd == LANES // 2
    a_sz = (bw // 2, bw // 2, bw, bw)
    n_gate = 4 * A_HEADS
    b_sz = (bw // 2, bw // 2, bw, bw)
    c_sz = (bw, C_KV_HEADS * c_hd, C_KV_HEADS * c_hd)
    d_sz = (bw, bw, bw)
    sizes = a_sz + (n_gate,) + b_sz + c_sz + d_sz + (N_BRANCH * d,)
    offs = [int(o) for o in np.concatenate([[0], np.cumsum(sizes)])]
    a0, g0, b0, c0, d0, bg0, end = offs[0], offs[4], offs[5], offs[9], offs[12], offs[15], offs[16]
    col_gate = 0
    col_a = N_BRANCH * d
    col_b = col_a + sum(a_sz)
    col_d = col_b + sum(b_sz)
    col_c = col_d + sum(d_sz)

    w_in_t = jnp.swapaxes(w_in, 1, 2)
    segments = [(bg0, end), (a0, g0), (b0, c0), (d0, bg0), (c0, d0)]
    w_g_t = _gate_prep(w_in_t, (g0, b0))
    w_cat_t = _win_prep(w_in_t, 0, segments)
    w1p = _moe_w1_prep(moe_w1)
    cast_srcs = (w_branch.reshape(depth, N_BRANCH * bw, d), w_out, moe_w2.reshape(depth, n_experts * ff, d))
    assert cast_srcs[0].shape == cast_srcs[1].shape == cast_srcs[2].shape
    wb_bf, wo_bf, w2_bf = _cast_bf16(0, *cast_srcs)
    b1p_all = jnp.concatenate([moe_b1[..., 0::2].reshape(depth, n_experts // 2, 1, 2 * ff),
                               moe_b1[..., 1::2].reshape(depth, n_experts // 2, 1, 2 * ff)], axis=-1)
    b2p_all = jnp.pad(moe_b2, ((0, 0), (0, LANES - n_experts), (0, 0))).astype(BF16)

    mod_all = _ada_modulation(jnp.pad(jnp.concatenate([c_ctx[None], c], axis=0), ((0, 8 - 1 - batch), (0, 0))),
                              ada_down, ada_up, ada_bias)
    cos_b, sin_b = _rope_tables(n_latent, ctx_rows, b_dv // 2)
    cos_c, sin_c = _rope_tables(n_latent, ctx_rows, c_hd)
    decay_p = jnp.pad(ret_decay, ((0, 0), (0, 8 - ret_decay.shape[1]), (0, LANES - ret_decay.shape[2])))

    tm_proj = _row_tile(t, 768)
    tm_moe = _row_tile(t, 512)
    z = (ctx.reshape(batch * ctx_rows, d), x.reshape(batch * n_latent, d))
    delta = None
    for l in range(depth):
        mod = mod_all[l]
        if l == 0:
            h = _resid_norm(z, norm_gain[l, 0], mod, 0, 1, batch, seq_rows, ctx_rows)[0]
        else:
            z, h = _resid_norm(z, norm_gain[l, 0], mod, 0, 1, batch, seq_rows, ctx_rows, delta=delta,
                               gate_mod=mod_all[l - 1], gate_slot=5)
        if l + 1 < depth:
            p, w_cat_t = _proj_prep(h, w_cat_t, w_in_t, l + 1, segments, tm=tm_proj, tn=1280)
        else:
            p = _matmul(h, w_cat_t[None], 0, BF16, tm=tm_proj, tn=1280, w_transposed=True)
        gates_c, gates_r = _gate_proj(h, w_g_t, l, tm_proj)
        gates_r = gates_r[:n_gate]
        bias_c = _pad_lanes(mlstm_gate_bias[l][None])
        bias_r = mlstm_gate_bias[l][:, None]

        hf = _mlstm(p, col_a, a_dv, gates_c, gates_r, bias_c, bias_r, batch, seq_rows, False)
        hb = _mlstm(p, col_a, a_dv, gates_c, gates_r, bias_c, bias_r, batch, seq_rows, True)
        y_a = _headnorm(hf, hb, p, col_a + 2 * bw, mlstm_norm_gain[l], A_HEADS, False, False)
        of = _retention(p, col_b, b_dv, cos_b, sin_b, decay_p[l], batch, seq_rows, False)
        ob = _retention(p, col_b, b_dv, cos_b, sin_b, decay_p[l], batch, seq_rows, True)
        y_b = _headnorm(of, ob, p, col_b + 2 * bw, ret_norm_gain[l], B_HEADS, True, True)
        y_c = _window_attention(p, col_c, cos_c, sin_c, sink[l], batch, seq_rows, ctx_rows)
        y_d = _nbr_attention(p, col_d, na_bias[l], batch, seq_rows, ctx_rows)

        acc, casts = _merge([y_a, y_b, y_c, y_d], p, col_gate, wb_bf, tm=tm_moe, tn=1024,
                            cast_next=(l + 1, cast_srcs) if l + 1 < depth else None)
        delta1 = _matmul(acc, wo_bf[None], 0, BF16, tm=tm_proj, tn=1024)

        rw = _pad_lanes(router_w[l]).astype(BF16)
        rb = _pad_lanes(router_b[l][None], NEG)
        z, h2, gates = _resid_norm(z, norm_gain[l, 1], mod, 3, 4, batch, seq_rows, ctx_rows, delta=delta1,
                                   gate_mod=mod, gate_slot=2, router=(rw, rb))
        delta = _moe(h2, gates, w1p, b1p_all[l], w2_bf, b2p_all[l], l * (n_experts // 2), tm=tm_moe, pairs=2)
        if casts:
            wb_bf, wo_bf, w2_bf = casts

    out = _final_norm(z, delta, mod_all[depth - 1], 5, final_gain, batch, seq_rows, ctx_rows)
    return out.reshape(batch, n_latent, d)
```

```python
import functools

import jax
import jax.numpy as jnp
import numpy as np
from jax import lax
from jax.experimental import pallas as pl
from jax.experimental.pallas import tpu as pltpu

F32 = jnp.float32
BF16 = jnp.bfloat16

GRID_W = 64
EPS = 1e-6
NEG = -1e30
ROPE_BASE = 10000.0
N_BRANCH = 4
A_HEADS = 4
B_HEADS = 4
C_HEADS = 16
C_KV_HEADS = 2
D_HEADS = 16
WINDOW = 128
NA_ROWS = 8
NA_COLS = 16
TOP_K = 4
SWIGLU_LIMIT = 7.0
SWIGLU_ALPHA = 1.702

LANES = 128
VMEM_LIMIT = 56 * 1024 * 1024
ROW_TILE = 256
NBR_ROWS = 4
NBR_UNION = NBR_ROWS + NA_ROWS - 1
NBR_SLABS_PER_STEP = 2


def _row_tile(rows, target):
    best = ROW_TILE
    for tile in range(ROW_TILE, target + 1, ROW_TILE):
        if rows % tile == 0:
            best = tile
    assert rows % best == 0
    return best


def _params(sem, vmem=VMEM_LIMIT):
    return pltpu.CompilerParams(dimension_semantics=sem, vmem_limit_bytes=vmem)


def _dot(a, b):
    return jnp.dot(a, b, preferred_element_type=F32)


def _dot_nt(a, b):
    return lax.dot_general(a, b, (((1,), (1,)), ((), ())), preferred_element_type=F32)


def _dot_tn(a, b):
    return lax.dot_general(a, b, (((0,), (0,)), ((), ())), preferred_element_type=F32)


def _sigmoid(x):
    return 1.0 / (1.0 + jnp.exp(-x))


def _softplus(x):
    return jnp.maximum(x, 0.0) + jnp.log1p(jnp.exp(-jnp.abs(x)))


def _log_sigmoid(x):
    return -_softplus(-x)


PREP_ROWS = ROW_TILE
CAST_ROWS = 64
F32_SUBLANES = 8


def _segment_blocks(segments):
    assert all(a % F32_SUBLANES == 0 and (b - a) % PREP_ROWS == 0 for a, b in segments)
    return sum((b - a) // PREP_ROWS for a, b in segments)


def _segment_row(step, segments):
    s = jnp.int32(0)
    base = 0
    for a, b in segments:
        n = (b - a) // PREP_ROWS
        s = jnp.where((step >= base) & (step < base + n),
                      a // F32_SUBLANES + (step - base) * (PREP_ROWS // F32_SUBLANES), s)
        base += n
    return s * F32_SUBLANES


def _win_prep_kernel(w_ref, cat_ref):
    cat_ref[...] = w_ref[0].astype(BF16)


def _win_prep(w_in_t, layer, segments):
    d = w_in_t.shape[2]
    n_blk = _segment_blocks(segments)
    return pl.pallas_call(
        _win_prep_kernel, grid=(n_blk,),
        in_specs=[pl.BlockSpec((pl.Element(1), pl.Element(PREP_ROWS), pl.Element(d)),
                               lambda j: (layer, _segment_row(j, segments), 0))],
        out_specs=pl.BlockSpec((PREP_ROWS, d), lambda j: (j, 0)),
        out_shape=jax.ShapeDtypeStruct((n_blk * PREP_ROWS, d), BF16),
        compiler_params=_params(("parallel",)), name="win_prep",
    )(w_in_t)


def _gate_prep_kernel(g_ref, gt_ref):
    g = g_ref[0].astype(BF16)
    gt_ref[...] = jnp.concatenate([g, jnp.zeros((LANES - g.shape[0], g.shape[1]), BF16)], axis=0)


def _gate_prep(w_in_t, gate_seg):
    depth, _, d = w_in_t.shape
    ga, gb = gate_seg
    return pl.pallas_call(
        _gate_prep_kernel, grid=(depth,),
        in_specs=[pl.BlockSpec((pl.Element(1), pl.Element(gb - ga), pl.Element(d)), lambda l: (l, ga, 0))],
        out_specs=pl.BlockSpec((None, LANES, d), lambda l: (l, 0, 0)),
        out_shape=jax.ShapeDtypeStruct((depth, LANES, d), BF16),
        compiler_params=_params(("parallel",)), name="gate_prep",
    )(w_in_t)


def _moe_w1_prep_kernel(w_ref, perm_ref, o_ref, *, ff):
    parts = [_dot(w_ref[0, s].astype(BF16), perm_ref[...]) for s in range(2)]
    o_ref[0] = jnp.concatenate([parts[0][:, :ff], parts[1][:, :ff], parts[0][:, ff:], parts[1][:, ff:]],
                               axis=1).astype(BF16)


def _moe_w1_prep(moe_w1):
    depth, n_exp, d, two_ff = moe_w1.shape
    ff = two_ff // 2
    rows = 1024
    src = np.concatenate([np.arange(0, two_ff, 2), np.arange(1, two_ff, 2)])
    perm = jnp.asarray(np.arange(two_ff)[:, None] == src[None, :], BF16)
    kern = functools.partial(_moe_w1_prep_kernel, ff=ff)
    return pl.pallas_call(
        kern, grid=(depth * n_exp // 2, d // rows),
        in_specs=[pl.BlockSpec((1, 2, rows, two_ff), lambda e, i: (e, 0, i, 0)),
                  pl.BlockSpec((two_ff, two_ff), lambda e, i: (0, 0))],
        out_specs=pl.BlockSpec((1, rows, 2 * two_ff), lambda e, i: (e, i, 0)),
        out_shape=jax.ShapeDtypeStruct((depth * n_exp // 2, d, 2 * two_ff), BF16),
        compiler_params=_params(("parallel", "parallel")), name="moe_w1_prep",
    )(moe_w1.reshape(depth * n_exp // 2, 2, d, two_ff), perm)


def _cast_kernel(*refs):
    n = len(refs) // 2
    for src, dst in zip(refs[:n], refs[n:]):
        dst[...] = src[...].astype(dst.dtype)


def _cast_bf16(layer, *arrays):
    _, rows, cols = arrays[0].shape
    tm = ROW_TILE
    return pl.pallas_call(
        _cast_kernel, grid=(rows // tm,),
        in_specs=[pl.BlockSpec((None, tm, cols), lambda i: (layer, i, 0))] * len(arrays),
        out_specs=[pl.BlockSpec((tm, cols), lambda i: (i, 0))] * len(arrays),
        out_shape=[jax.ShapeDtypeStruct((rows, cols), BF16)] * len(arrays),
        compiler_params=_params(("parallel",)), name="cast_bf16",
    )(*arrays)


def _ada_kernel(cv_ref, down_ref, up_ref, bias_ref, out_ref):
    cv = cv_ref[...]
    a = cv * _sigmoid(cv)
    z = _dot(a.astype(BF16), down_ref[0].astype(BF16))
    out_ref[0] = _dot(z.astype(BF16), up_ref[0].astype(BF16)) + bias_ref[0]


def _ada_modulation(cvecs, ada_down, ada_up, ada_bias):
    depth, d, r = ada_down.shape
    rows = cvecs.shape[0]
    return pl.pallas_call(
        _ada_kernel,
        grid=(depth, 6),
        in_specs=[
            pl.BlockSpec((rows, d), lambda l, j: (0, 0)),
            pl.BlockSpec((1, d, r), lambda l, j: (l, 0, 0)),
            pl.BlockSpec((1, r, d), lambda l, j: (l, 0, j)),
            pl.BlockSpec((1, 1, d), lambda l, j: (l, 0, j)),
        ],
        out_specs=pl.BlockSpec((1, rows, d), lambda l, j: (l, 0, j)),
        out_shape=jax.ShapeDtypeStruct((depth, rows, 6 * d), F32),
        compiler_params=_params(("parallel", "arbitrary")),
        name="ada_modulation",
    )(cvecs, ada_down, ada_up, ada_bias.reshape(depth, 1, 6 * d))


def _rmsnorm(x, gain):
    return x * lax.rsqrt(jnp.mean(x * x, axis=-1, keepdims=True) + EPS) * gain


def _resid_norm_kernel(*refs, split_src, has_delta, has_router, tiles_per_seq, ctx_tiles):
    refs = list(refs)
    tile = pl.program_id(0)
    b = tile // tiles_per_seq
    j = tile - b * tiles_per_seq
    is_ctx = j < ctx_tiles
    row = jnp.where(is_ctx, 0, 1 + b)
    if split_src:
        c_ref = refs.pop(0)
        x_ref = refs.pop(0)
        z = jnp.where(is_ctx, c_ref[...], x_ref[...])
    else:
        z = refs.pop(0)[...]
    if has_delta:
        delta_ref = refs.pop(0)
        gate_ref = refs.pop(0)
    gain_ref = refs.pop(0)
    shift_ref = refs.pop(0)
    scale_ref = refs.pop(0)
    if has_router:
        rw_ref = refs.pop(0)
        rb_ref = refs.pop(0)
    if has_delta:
        znew_ref = refs.pop(0)
    h_ref = refs.pop(0)
    if has_router:
        gates_ref = refs.pop(0)

    if has_delta:
        z = z + gate_ref[pl.ds(row, 1), :] * delta_ref[...].astype(F32)
        znew_ref[...] = z
    y = _rmsnorm(z, gain_ref[...])
    h = y * (1.0 + scale_ref[pl.ds(row, 1), :]) + shift_ref[pl.ds(row, 1), :]
    hb = h.astype(BF16)
    h_ref[...] = hb
    if has_router:
        logits = _dot(hb, rw_ref[...]) + rb_ref[...]
        lane = lax.broadcasted_iota(jnp.int32, logits.shape, 1).astype(F32)
        gates = jnp.zeros_like(logits)
        den = jnp.zeros((logits.shape[0], 1), F32)
        top = None
        for _ in range(TOP_K):
            m = jnp.max(logits, axis=-1, keepdims=True)
            idx = jnp.min(jnp.where(logits == m, lane, float(LANES)), axis=-1, keepdims=True)
            sel = lane == idx
            if top is None:
                top = m
            e = jnp.exp(m - top)
            den = den + e
            gates = jnp.where(sel, e, gates)
            logits = jnp.where(sel, NEG * 2.0, logits)
        gates_ref[...] = gates / den


def _resid_norm(src, gain, mod, shift_slot, scale_slot, batch, seq_rows, ctx_rows, delta=None, gate_mod=None,
                gate_slot=None, router=None):
    split_src = isinstance(src, tuple)
    d = src[0].shape[1] if split_src else src.shape[1]
    t = batch * seq_rows
    tm = ROW_TILE
    tiles_per_seq = seq_rows // tm
    ctx_tiles = ctx_rows // tm
    lat_tiles = tiles_per_seq - ctx_tiles
    has_delta = delta is not None
    has_router = router is not None
    row_spec = pl.BlockSpec((tm, d), lambda i: (i, 0))
    vec_spec = pl.BlockSpec((1, d), lambda i: (0, 0))

    def mod_spec(slot):
        return pl.BlockSpec((mod.shape[0], d), lambda i: (0, slot))

    def ctx_map(i):
        b = i // tiles_per_seq
        return b * ctx_tiles + jnp.minimum(i - b * tiles_per_seq, ctx_tiles - 1), 0

    def lat_map(i):
        b = i // tiles_per_seq
        return b * lat_tiles + jnp.maximum(i - b * tiles_per_seq - ctx_tiles, 0), 0

    if split_src:
        operands, in_specs = list(src), [pl.BlockSpec((tm, d), ctx_map), pl.BlockSpec((tm, d), lat_map)]
    else:
        operands, in_specs = [src], [row_spec]
    if has_delta:
        operands += [delta, gate_mod]
        in_specs += [row_spec, mod_spec(gate_slot)]
    operands += [gain.reshape(1, d), mod, mod]
    in_specs += [vec_spec, mod_spec(shift_slot), mod_spec(scale_slot)]
    if has_router:
        rw, rb = router
        operands += [rw, rb]
        in_specs += [pl.BlockSpec(rw.shape, lambda i: (0, 0)), pl.BlockSpec(rb.shape, lambda i: (0, 0))]
    out_shape, out_specs = [], []
    if has_delta:
        out_shape.append(jax.ShapeDtypeStruct((t, d), F32))
        out_specs.append(row_spec)
    out_shape.append(jax.ShapeDtypeStruct((t, d), BF16))
    out_specs.append(row_spec)
    if has_router:
        out_shape.append(jax.ShapeDtypeStruct((t, LANES), F32))
        out_specs.append(pl.BlockSpec((tm, LANES), lambda i: (i, 0)))
    kern = functools.partial(_resid_norm_kernel, split_src=split_src, has_delta=has_delta, has_router=has_router,
                             tiles_per_seq=tiles_per_seq, ctx_tiles=ctx_tiles)
    return pl.pallas_call(
        kern, grid=(t // tm,), in_specs=in_specs, out_specs=out_specs, out_shape=out_shape,
        compiler_params=_params(("parallel",)), name="resid_norm",
    )(*operands)


def _final_norm_kernel(z_ref, delta_ref, gate_ref, gain_ref, out_ref):
    row = 1 + pl.program_id(0)
    z = z_ref[...] + gate_ref[pl.ds(row, 1), :] * delta_ref[...].astype(F32)
    out_ref[...] = _rmsnorm(z, gain_ref[...])


def _final_norm(z, delta, gate_mod, gate_slot, gain, batch, seq_rows, ctx_rows):
    t, d = z.shape
    tm = ROW_TILE
    lat_tiles = (seq_rows - ctx_rows) // tm
    tiles_per_seq = seq_rows // tm
    ctx_tiles = ctx_rows // tm
    row_spec = pl.BlockSpec((tm, d), lambda b, j: (b * tiles_per_seq + ctx_tiles + j, 0))
    return pl.pallas_call(
        _final_norm_kernel,
        grid=(batch, lat_tiles),
        in_specs=[row_spec, row_spec,
                  pl.BlockSpec((gate_mod.shape[0], d), lambda b, j: (0, gate_slot)),
                  pl.BlockSpec((1, d), lambda b, j: (0, 0))],
        out_specs=pl.BlockSpec((tm, d), lambda b, j: (b * lat_tiles + j, 0)),
        out_shape=jax.ShapeDtypeStruct((batch * lat_tiles * tm, d), F32),
        compiler_params=_params(("parallel", "parallel")), name="final_norm",
    )(z, delta, gate_mod, gain.reshape(1, d))


def _matmul_kernel(a_ref, w_ref, *refs, w_transposed, n_cast, cast_steps):
    srcs = refs[:n_cast]
    o_ref = refs[n_cast]
    dsts = refs[n_cast + 1:]
    dot = _dot_nt if w_transposed else _dot
    o_ref[...] = dot(a_ref[...], w_ref[...]).astype(o_ref.dtype)

    if n_cast:
        @pl.when(pl.program_id(0) * pl.num_programs(1) + pl.program_id(1) < cast_steps)
        def _():
            for src, dst in zip(srcs, dsts):
                dst[...] = src[...].astype(dst.dtype)


def _matmul(a, w, layer, out_dtype, tm, tn, w_transposed=False, cast_next=None):
    m, k = a.shape
    if w_transposed:
        n = w.shape[1]
        w_spec = pl.BlockSpec((None, tn, k), lambda i, j: (layer, j, 0))
    else:
        n = w.shape[2]
        w_spec = pl.BlockSpec((None, k, tn), lambda i, j: (layer, 0, j))
    nj = n // tn
    steps = (m // tm) * nj
    in_specs = [pl.BlockSpec((tm, k), lambda i, j: (i, 0)), w_spec]
    out_specs = [pl.BlockSpec((tm, tn), lambda i, j: (i, j))]
    out_shape = [jax.ShapeDtypeStruct((m, n), out_dtype)]
    operands = [a, w]
    n_cast = cast_steps = 0
    if cast_next is not None:
        src_layer, arrays = cast_next
        _, rows, cols = arrays[0].shape
        n_cast = len(arrays)
        cast_rows = CAST_ROWS
        while rows // cast_rows > steps:
            cast_rows *= 2
        cast_steps = rows // cast_rows

        def blk(i, j):
            return jnp.minimum(i * nj + j, cast_steps - 1)

        in_specs += [pl.BlockSpec((None, cast_rows, cols), lambda i, j: (src_layer, blk(i, j), 0))] * n_cast
        out_specs += [pl.BlockSpec((cast_rows, cols), lambda i, j: (blk(i, j), 0))] * n_cast
        out_shape += [jax.ShapeDtypeStruct((rows, cols), BF16)] * n_cast
        operands += list(arrays)
    outs = pl.pallas_call(
        functools.partial(_matmul_kernel, w_transposed=w_transposed, n_cast=n_cast, cast_steps=cast_steps),
        grid=(m // tm, nj),
        in_specs=in_specs, out_specs=out_specs, out_shape=out_shape,
        compiler_params=_params(("arbitrary", "arbitrary") if n_cast else ("parallel", "arbitrary")), name="matmul",
    )(*operands)
    return outs[0], outs[1:]


def _proj_prep_kernel(a_ref, w_ref, src_ref, o_ref, dst_ref, *, n_prep):
    o_ref[...] = _dot_nt(a_ref[...], w_ref[...]).astype(o_ref.dtype)

    @pl.when(pl.program_id(0) * pl.num_programs(1) + pl.program_id(1) < n_prep)
    def _():
        dst_ref[...] = src_ref[0].astype(dst_ref.dtype)


def _proj_prep(a, w_t, w_in_t, next_layer, segments, tm, tn):
    m, k = a.shape
    n = w_t.shape[0]
    n_blk = _segment_blocks(segments)
    nj = n // tn
    assert (m // tm) * nj >= n_blk

    def prep_block(i, j):
        return jnp.minimum(i * nj + j, n_blk - 1)

    return pl.pallas_call(
        functools.partial(_proj_prep_kernel, n_prep=n_blk),
        grid=(m // tm, nj),
        in_specs=[pl.BlockSpec((tm, k), lambda i, j: (i, 0)),
                  pl.BlockSpec((tn, k), lambda i, j: (j, 0)),
                  pl.BlockSpec((pl.Element(1), pl.Element(PREP_ROWS), pl.Element(k)),
                               lambda i, j: (next_layer, _segment_row(prep_block(i, j), segments), 0))],
        out_specs=[pl.BlockSpec((tm, tn), lambda i, j: (i, j)),
                   pl.BlockSpec((PREP_ROWS, k), lambda i, j: (prep_block(i, j), 0))],
        out_shape=[jax.ShapeDtypeStruct((m, n), BF16), jax.ShapeDtypeStruct((n_blk * PREP_ROWS, k), BF16)],
        compiler_params=_params(("arbitrary", "arbitrary")), name="proj_prep",
    )(a, w_t, w_in_t)


def _gate_proj_kernel(a_ref, wt_ref, oc_ref, or_ref):
    a = a_ref[...]
    wt = wt_ref[...]
    oc_ref[...] = _dot_nt(a, wt)
    or_ref[...] = _dot_nt(wt, a)


def _gate_proj(a, wt, layer, tm):
    m, k = a.shape
    return pl.pallas_call(
        _gate_proj_kernel,
        grid=(m // tm,),
        in_specs=[pl.BlockSpec((tm, k), lambda i: (i, 0)),
                  pl.BlockSpec((None, LANES, k), lambda i: (layer, 0, 0))],
        out_specs=[pl.BlockSpec((tm, LANES), lambda i: (i, 0)), pl.BlockSpec((LANES, tm), lambda i: (0, i))],
        out_shape=[jax.ShapeDtypeStruct((m, LANES), F32), jax.ShapeDtypeStruct((LANES, m), F32)],
        compiler_params=_params(("parallel",)), name="gate_proj",
    )(a, wt)


def _tri_masks(length, reverse):
    row = lax.broadcasted_iota(jnp.int32, (length, length), 0)
    col = lax.broadcasted_iota(jnp.int32, (length, length), 1)
    if reverse:
        return col >= row, row >= col
    return col <= row, row <= col


def _mlstm_kernel(q_ref, k_ref, v_ref, gc_ref, gr_ref, bc_ref, br_ref, h_ref, ct_ref, n_ref, m_ref, *,
                  reverse, heads, dqk, dv):
    @pl.when(pl.program_id(1) == 0)
    def _():
        ct_ref[...] = jnp.zeros_like(ct_ref)
        n_ref[...] = jnp.zeros_like(n_ref)
        m_ref[...] = jnp.zeros_like(m_ref)

    length = q_ref.shape[0]
    valid, valid_t = _tri_masks(length, reverse)
    gc = gc_ref[...] + bc_ref[...]
    gr = gr_ref[...] + br_ref[...]
    d0 = 2 * heads if reverse else 0
    scale = dqk ** -0.5
    for h in range(heads):
        i_col = gc[:, d0 + h:d0 + h + 1]
        f_col = _log_sigmoid(gc[:, d0 + heads + h:d0 + heads + h + 1])
        i_row = gr[d0 + h:d0 + h + 1, :]
        f_row = _log_sigmoid(gr[d0 + heads + h:d0 + heads + h + 1, :])
        b_col = jnp.sum(jnp.where(valid, f_row, 0.0), axis=1, keepdims=True)
        b_row = jnp.sum(jnp.where(valid_t, f_col, 0.0), axis=0, keepdims=True)
        b_end = jnp.sum(f_row, axis=1, keepdims=True)
        m_prev = m_ref[h][:, :1]
        log_d = jnp.where(valid, b_col - b_row + i_row, NEG)
        m_inter = b_col + m_prev
        m_t = jnp.maximum(m_inter, jnp.max(log_d, axis=1, keepdims=True))
        w_intra = jnp.exp(log_d - m_t) * scale
        w_inter = jnp.exp(m_inter - m_t) * scale
        q = q_ref[:, h * dqk:(h + 1) * dqk]
        k = k_ref[:, h * dqk:(h + 1) * dqk]
        v = v_ref[:, h * dv:(h + 1) * dv]
        s = _dot_nt(q, k) * w_intra
        num = _dot(s.astype(BF16), v) + w_inter * _dot(q, ct_ref[h].astype(BF16))
        qn = jnp.sum(q.astype(F32) * n_ref[h], axis=1, keepdims=True)
        den = jnp.sum(s, axis=1, keepdims=True) + w_inter * qn
        h_ref[:, h * dv:(h + 1) * dv] = (num / jnp.maximum(jnp.abs(den), jnp.exp(-m_t))).astype(h_ref.dtype)
        log_w = b_end - b_col + i_col
        m_new = jnp.maximum(b_end + m_prev, jnp.max(log_w, axis=0, keepdims=True))
        w_end = jnp.exp(log_w - m_new)
        decay = jnp.exp(b_end + m_prev - m_new)
        wv = (w_end * v.astype(F32)).astype(BF16)
        ct_ref[h] = decay * ct_ref[h] + _dot_tn(k, wv)
        n_ref[h] = decay * n_ref[h] + jnp.sum(w_end * k.astype(F32), axis=0, keepdims=True)
        m_ref[h] = jnp.broadcast_to(m_new, (1, LANES))


def _chunk_index(i, n_chunks, reverse):
    if not reverse:
        return i
    return jnp.where(i == 0, 0, n_chunks - i)


def _mlstm(p, col0, dv, gates_c, gates_r, bias_c, bias_r, batch, seq_rows, reverse):
    t = p.shape[0]
    length = ROW_TILE
    n_chunks = seq_rows // length
    heads = A_HEADS
    dqk = dv // 2
    qw, vw = heads * dqk, heads * dv

    def rows(b, i):
        return b * n_chunks + _chunk_index(i, n_chunks, reverse)

    kern = functools.partial(_mlstm_kernel, reverse=reverse, heads=heads, dqk=dqk, dv=dv)
    return pl.pallas_call(
        kern,
        grid=(batch, n_chunks),
        in_specs=[
            pl.BlockSpec((length, qw), lambda b, i: (rows(b, i), col0 // qw)),
            pl.BlockSpec((length, qw), lambda b, i: (rows(b, i), col0 // qw + 1)),
            pl.BlockSpec((length, vw), lambda b, i: (rows(b, i), (col0 + 2 * qw) // vw)),
            pl.BlockSpec((length, LANES), lambda b, i: (rows(b, i), 0)),
            pl.BlockSpec((4 * heads, length), lambda b, i: (0, rows(b, i))),
            pl.BlockSpec((1, LANES), lambda b, i: (0, 0)),
            pl.BlockSpec((4 * heads, 1), lambda b, i: (0, 0)),
        ],
        out_specs=pl.BlockSpec((length, vw), lambda b, i: (rows(b, i), 0)),
        out_shape=jax.ShapeDtypeStruct((t, vw), BF16),
        scratch_shapes=[pltpu.VMEM((heads, dqk, dv), F32), pltpu.VMEM((heads, 1, dqk), F32),
                        pltpu.VMEM((heads, 1, LANES), F32)],
        compiler_params=_params(("parallel", "arbitrary")), name="mlstm_scan",
    )(p, p, p, gates_c, gates_r, bias_c, bias_r)


def _rope(t, cosf, sinf, half):
    if 2 * half == LANES:
        partner = pltpu.roll(t, half, axis=1)
    else:
        lane = lax.broadcasted_iota(jnp.int32, t.shape, 1)
        partner = jnp.where((lane & (2 * half - 1)) < half, pltpu.roll(t, LANES - half, axis=1),
                            pltpu.roll(t, half, axis=1))
    return t * cosf + partner * sinf


def _retention_kernel(q_ref, k_ref, v_ref, cos_ref, sin_ref, dp_ref, o_ref, s_ref, *, reverse, heads, dk, dv):
    @pl.when(pl.program_id(1) == 0)
    def _():
        s_ref[...] = jnp.zeros_like(s_ref)

    length = q_ref.shape[0]
    row = lax.broadcasted_iota(jnp.int32, (length, length), 0)
    col = lax.broadcasted_iota(jnp.int32, (length, length), 1)
    diff = (col - row if reverse else row - col).astype(F32)
    tpos = lax.broadcasted_iota(jnp.int32, (length, 1), 0).astype(F32)
    lg_all = -_softplus(dp_ref[...])
    cosf = cos_ref[...]
    sinf = sin_ref[...]
    scale = dk ** -0.5
    d = 1 if reverse else 0
    for h in range(heads):
        lg = lg_all[d:d + 1, h:h + 1]
        decay_mat = jnp.where(diff >= 0.0, jnp.exp(lg * jnp.maximum(diff, 0.0)), 0.0) * scale
        if reverse:
            read_w = jnp.exp(lg * (length - tpos))
            write_w = jnp.exp(lg * tpos)
        else:
            read_w = jnp.exp(lg * (tpos + 1.0))
            write_w = jnp.exp(lg * (length - 1.0 - tpos))
        chunk_decay = jnp.exp(lg * float(length))
        q = _rope(q_ref[:, h * dk:(h + 1) * dk].astype(F32), cosf, sinf, dk // 2).astype(BF16)
        k = _rope(k_ref[:, h * dk:(h + 1) * dk].astype(F32), cosf, sinf, dk // 2).astype(BF16)
        v = v_ref[:, h * dv:(h + 1) * dv]
        s = _dot_nt(q, k) * decay_mat
        o_ref[:, h * dv:(h + 1) * dv] = (_dot(s.astype(BF16), v)
                                         + read_w * _dot(q, s_ref[h].astype(BF16))).astype(o_ref.dtype)
        wv = (write_w * v.astype(F32)).astype(BF16)
        s_ref[h] = chunk_decay * s_ref[h] + scale * _dot_tn(k, wv)


def _retention(p, col0, dv, cosf, sinf, decay_p, batch, seq_rows, reverse):
    t = p.shape[0]
    length = ROW_TILE
    n_chunks = seq_rows // length
    heads = B_HEADS
    dk = dv // 2
    assert dk == LANES
    qw, vw = heads * dk, heads * dv

    def rows(b, i):
        return b * n_chunks + _chunk_index(i, n_chunks, reverse)

    kern = functools.partial(_retention_kernel, reverse=reverse, heads=heads, dk=dk, dv=dv)
    return pl.pallas_call(
        kern,
        grid=(batch, n_chunks),
        in_specs=[
            pl.BlockSpec((length, qw), lambda b, i: (rows(b, i), col0 // qw)),
            pl.BlockSpec((length, qw), lambda b, i: (rows(b, i), col0 // qw + 1)),
            pl.BlockSpec((length, vw), lambda b, i: (rows(b, i), (col0 + 2 * qw) // vw)),
            pl.BlockSpec((length, LANES), lambda b, i: (_chunk_index(i, n_chunks, reverse), 0)),
            pl.BlockSpec((length, LANES), lambda b, i: (_chunk_index(i, n_chunks, reverse), 0)),
            pl.BlockSpec(decay_p.shape, lambda b, i: (0, 0)),
        ],
        out_specs=pl.BlockSpec((length, vw), lambda b, i: (rows(b, i), 0)),
        out_shape=jax.ShapeDtypeStruct((t, vw), BF16),
        scratch_shapes=[pltpu.VMEM((heads, dk, dv), F32)],
        compiler_params=_params(("parallel", "arbitrary")), name="retention_scan",
    )(p, p, p, cosf, sinf, decay_p)


def _headnorm_kernel(hf_ref, hb_ref, gate_ref, gain_ref, y_ref, *, heads, dv, center, silu_gate):
    g = gate_ref[...].astype(F32)
    sg = _sigmoid(g)
    gate = g * sg if silu_gate else sg
    gain = gain_ref[...]
    for h in range(heads):
        sl = slice(h * dv, (h + 1) * dv)
        x = hf_ref[:, sl].astype(F32) + hb_ref[:, sl].astype(F32)
        if center:
            x = x - jnp.mean(x, axis=-1, keepdims=True)
        y = x * lax.rsqrt(jnp.mean(x * x, axis=-1, keepdims=True) + EPS)
        y_ref[:, sl] = (gate[:, sl] * (y * gain[:, sl])).astype(y_ref.dtype)


def _headnorm(hf, hb, p, gate_col, gain, heads, center, silu_gate):
    t, w = hf.shape
    tm = _row_tile(t, 2 * ROW_TILE)
    kern = functools.partial(_headnorm_kernel, heads=heads, dv=w // heads, center=center, silu_gate=silu_gate)
    row_spec = pl.BlockSpec((tm, w), lambda i: (i, 0))
    return pl.pallas_call(
        kern, grid=(t // tm,),
        in_specs=[row_spec, row_spec, pl.BlockSpec((tm, w), lambda i: (i, gate_col // w)),
                  pl.BlockSpec((1, w), lambda i: (0, 0))],
        out_specs=row_spec,
        out_shape=jax.ShapeDtypeStruct((t, w), BF16),
        compiler_params=_params(("parallel",)), name="headnorm_gate",
    )(hf, hb, p, gain.reshape(1, w))


def _lane_low(shape, hd):
    return lax.broadcasted_iota(jnp.int32, shape, 1) < hd


def _stack_heads(q, hd):
    low = _lane_low(q.shape, hd)
    zero = jnp.zeros_like(q)
    return jnp.concatenate([jnp.where(low, q, zero), jnp.where(low, zero, q)], axis=0)


def _unstack_heads(o, hd):
    rows = o.shape[0] // 2
    return jnp.where(_lane_low((rows, LANES), hd), o[:rows], o[rows:])


def _window_kernel(sink_ref, q_ref, kp_ref, k0_ref, kn_ref, vp_ref, v0_ref, vn_ref, kc_ref, vc_ref,
                   cq_ref, sq_ref, cp_ref, sp_ref, cn_ref, sn_ref, o_ref, *, heads, kv_heads, hd, ctx_blocks,
                   n_latent, slabs_per_dot):
    i = pl.program_id(1)
    j = i - ctx_blocks
    w = q_ref.shape[0]
    half = hd // 2
    slabs_per_group = heads // kv_heads // 2

    def roped(x_ref, c_ref, s_ref):
        return _rope(x_ref[...].astype(F32), c_ref[...], s_ref[...], half)

    kb = jnp.concatenate([roped(kp_ref, cp_ref, sp_ref), roped(k0_ref, cq_ref, sq_ref),
                          roped(kn_ref, cn_ref, sn_ref)], axis=0)
    vb = jnp.concatenate([vp_ref[...], v0_ref[...], vn_ref[...]], axis=0).astype(F32)
    kc = kc_ref[...].astype(F32)
    vc = vc_ref[...].astype(F32)

    def group_copy(x, g):
        keep = _lane_low(x.shape, hd) if g == 0 else jnp.logical_not(_lane_low(x.shape, hd))
        return jnp.where(keep, x, pltpu.roll(x, hd, axis=1)).astype(BF16)

    c = lax.broadcasted_iota(jnp.int32, (3 * w, w), 0)
    t = lax.broadcasted_iota(jnp.int32, (3 * w, w), 1)
    lower = jnp.maximum(t + (w - WINDOW), (1 - j) * w)
    upper = jnp.minimum(t + (w + WINDOW), n_latent - 1 - (j - 1) * w)
    upper = jnp.where(j < 0, -1, upper)
    mask_bias = jnp.where((c >= lower) & (c <= upper), 0.0, NEG)
    mask_bias = jnp.concatenate([mask_bias] * (2 * slabs_per_dot), axis=1)
    scale = hd ** -0.5
    cq = cq_ref[...]
    sq = sq_ref[...]
    for g in range(kv_heads):
        kk, vv, kkc, vvc = group_copy(kb, g), group_copy(vb, g), group_copy(kc, g), group_copy(vc, g)
        for s0 in range(0, slabs_per_group, slabs_per_dot):
            slabs = [g * slabs_per_group + s0 + s for s in range(slabs_per_dot)]
            qs = []
            sinks = []
            for slab in slabs:
                qr = _rope(q_ref[:, slab * LANES:(slab + 1) * LANES].astype(F32), cq, sq, half)
                qs.append(_stack_heads((qr * scale).astype(BF16), hd))
                sinks += [jnp.full((1, w), sink_ref[2 * slab], F32), jnp.full((1, w), sink_ref[2 * slab + 1], F32)]
            qq = jnp.concatenate(qs, axis=0)
            sink = jnp.concatenate(sinks, axis=1)
            s_loc = _dot_nt(kk, qq) + mask_bias
            s_ctx = _dot_nt(kkc, qq)
            m = jnp.maximum(jnp.maximum(jnp.max(s_loc, axis=0, keepdims=True),
                                        jnp.max(s_ctx, axis=0, keepdims=True)), sink)
            e_loc = jnp.exp(s_loc - m)
            e_ctx = jnp.exp(s_ctx - m)
            den = (jnp.exp(sink - m) + jnp.sum(e_loc, axis=0, keepdims=True)
                   + jnp.sum(e_ctx, axis=0, keepdims=True))
            o_t = (_dot_tn(vv, e_loc.astype(BF16)) + _dot_tn(vvc, e_ctx.astype(BF16))) / den
            o = o_t.T
            for n, slab in enumerate(slabs):
                o_ref[:, slab * LANES:(slab + 1) * LANES] = _unstack_heads(
                    o[2 * n * w:2 * (n + 1) * w], hd).astype(o_ref.dtype)


def _window_attention(p, col0, cosf, sinf, sink, batch, seq_rows, ctx_rows):
    t = p.shape[0]
    w = WINDOW
    heads, kv_heads = C_HEADS, C_KV_HEADS
    hd = LANES // 2
    assert kv_heads * hd == LANES
    qw = heads * hd
    nblk = seq_rows // w
    ctx_blocks = ctx_rows // w
    kcol = (col0 + qw) // LANES
    vcol = kcol + 1

    def cur(b, i):
        return b * nblk + i

    def prev(b, i):
        return b * nblk + jnp.maximum(i - 1, 0)

    def nxt(b, i):
        return b * nblk + jnp.minimum(i + 1, nblk - 1)

    def kv_spec(fn, colblk):
        return pl.BlockSpec((w, LANES), lambda b, i: (fn(b, i), colblk))

    def tab_spec(fn):
        return pl.BlockSpec((w, LANES), lambda b, i: (fn(0, i), 0))

    kern = functools.partial(_window_kernel, heads=heads, kv_heads=kv_heads, hd=hd, ctx_blocks=ctx_blocks,
                             n_latent=seq_rows - ctx_rows, slabs_per_dot=4)
    ctx_spec_k = pl.BlockSpec((ctx_rows, LANES), lambda b, i: (b * (seq_rows // ctx_rows), kcol))
    ctx_spec_v = pl.BlockSpec((ctx_rows, LANES), lambda b, i: (b * (seq_rows // ctx_rows), vcol))
    return pl.pallas_call(
        kern,
        grid=(batch, nblk),
        in_specs=[
            pl.BlockSpec(memory_space=pltpu.SMEM),
            pl.BlockSpec((w, qw), lambda b, i: (cur(b, i), col0 // qw)),
            kv_spec(prev, kcol), kv_spec(cur, kcol), kv_spec(nxt, kcol),
            kv_spec(prev, vcol), kv_spec(cur, vcol), kv_spec(nxt, vcol),
            ctx_spec_k, ctx_spec_v,
            tab_spec(cur), tab_spec(cur), tab_spec(prev), tab_spec(prev), tab_spec(nxt), tab_spec(nxt),
        ],
        out_specs=pl.BlockSpec((w, qw), lambda b, i: (cur(b, i), 0)),
        out_shape=jax.ShapeDtypeStruct((t, qw), BF16),
        compiler_params=_params(("parallel", "arbitrary")), name="window_attention",
    )(sink, p, p, p, p, p, p, p, p, p, cosf, sinf, cosf, sinf, cosf, sinf)


def _nbr_layout():
    masked = 2 * NA_ROWS - 1
    variants = [(0, [0] * NBR_ROWS), (NA_ROWS // 2, list(range(NBR_ROWS))),
                (NBR_UNION - NBR_ROWS, [NBR_UNION - NA_ROWS] * NBR_ROWS)]
    pairs, index = [], []
    for delta, rel_start in variants:
        per_kr = []
        for kr in range(NBR_UNION):
            codes = [kr - (delta + rr) + NA_ROWS - 1 if rel_start[rr] <= kr < rel_start[rr] + NA_ROWS else masked
                     for rr in range(NBR_ROWS)]
            row = []
            for a in range(NBR_ROWS // 2):
                pair = (codes[2 * a], codes[2 * a + 1])
                if pair not in pairs:
                    pairs.append(pair)
                row.append(pairs.index(pair))
            per_kr.append(row)
        index.append(per_kr)
    return pairs, index


def _nbr_kernel(q_ref, k_ref, v_ref, tab_ref, o_ref, bias_ref, *, hd, ctx_rows, grid_rows, index, slabs_per_step):
    i = pl.program_id(2)
    scale = hd ** -0.5
    tq = q_ref.shape[0]
    lanes = [slice(sl * LANES, (sl + 1) * LANES) for sl in range(slabs_per_step)]
    qqs = [_stack_heads(q_ref[:, ln] * scale, hd) for ln in lanes]
    kcs = [k_ref[0:ctx_rows, ln] for ln in lanes]
    vcs = [v_ref[0:ctx_rows, ln] for ln in lanes]
    s_ctxs = [_dot_nt(kc, qq) for kc, qq in zip(kcs, qqs)]
    m_ctxs = [jnp.max(s, axis=0, keepdims=True) for s in s_ctxs]

    @pl.when(i == 0)
    def _():
        for sl, ln in enumerate(lanes):
            for variant, per_kr in enumerate(index):
                for kr, row in enumerate(per_kr):
                    for sub in range(2):
                        for a, u in enumerate(row):
                            bias_ref[sl, variant, kr * GRID_W:(kr + 1) * GRID_W,
                                     sub * tq + a * LANES:sub * tq + (a + 1) * LANES] = tab_ref[2 * sl + sub, u]
            e = jnp.exp(s_ctxs[sl] - m_ctxs[sl])
            o_t = _dot_tn(vcs[sl], e.astype(BF16)) / jnp.sum(e, axis=0, keepdims=True)
            o_ref[:, ln] = _unstack_heads(o_t.T, hd).astype(o_ref.dtype)

    @pl.when(i > 0)
    def _():
        blk = i - 1
        n_blk = grid_rows // NBR_ROWS
        u0 = jnp.clip(blk * NBR_ROWS - NA_ROWS // 2, 0, grid_rows - NBR_UNION)
        variant = jnp.where(blk == 0, 0, jnp.where(blk == n_blk - 1, 2, 1))
        base = pl.multiple_of(ctx_rows + u0 * GRID_W, GRID_W)
        for sl, ln in enumerate(lanes):
            kw = k_ref[pl.ds(base, NBR_UNION * GRID_W), ln]
            vw = v_ref[pl.ds(base, NBR_UNION * GRID_W), ln]
            s_loc = _dot_nt(kw, qqs[sl]) + bias_ref[sl, variant]
            m = jnp.maximum(jnp.max(s_loc, axis=0, keepdims=True), m_ctxs[sl])
            e_loc = jnp.exp(s_loc - m)
            e_ctx = jnp.exp(s_ctxs[sl] - m)
            den = jnp.sum(e_loc, axis=0, keepdims=True) + jnp.sum(e_ctx, axis=0, keepdims=True)
            o_t = (_dot_tn(vw, e_loc.astype(BF16)) + _dot_tn(vcs[sl], e_ctx.astype(BF16))) / den
            o_ref[:, ln] = _unstack_heads(o_t.T, hd).astype(o_ref.dtype)


def _nbr_table(na_bias_l, pairs):
    col = jnp.arange(GRID_W)
    col_start = jnp.clip(col - NA_COLS // 2, 0, GRID_W - NA_COLS)
    col_valid = (col[None, :] >= col_start[:, None]) & (col[None, :] < col_start[:, None] + NA_COLS)
    dc = jnp.clip(col[None, :] - col[:, None], -(NA_COLS - 1), NA_COLS - 1) + (NA_COLS - 1)
    blocks = jnp.where(col_valid.T[None, None], na_bias_l[:, :, dc.T], NEG)
    blocks = jnp.concatenate([blocks, jnp.full_like(blocks[:, :1], NEG)], axis=1)
    left = blocks[:, np.array([p[0] for p in pairs])]
    right = blocks[:, np.array([p[1] for p in pairs])]
    return jnp.concatenate([left, right], axis=-1).astype(F32)


def _nbr_attention(p, col0, na_bias_l, batch, seq_rows, ctx_rows):
    t = p.shape[0]
    heads = D_HEADS
    hd = LANES // 2
    qw = heads * hd
    slabs = heads // 2
    tq = NBR_ROWS * GRID_W
    n_latent = seq_rows - ctx_rows
    grid_rows = n_latent // GRID_W
    assert tq == ctx_rows and grid_rows % NBR_ROWS == 0 and grid_rows >= NBR_UNION + 1
    assert slabs % NBR_SLABS_PER_STEP == 0 and (col0 // LANES) % NBR_SLABS_PER_STEP == 0
    steps = seq_rows // tq
    q0 = col0 // LANES
    pairs, index = _nbr_layout()
    table = _nbr_table(na_bias_l, pairs)
    sps = NBR_SLABS_PER_STEP
    width = sps * LANES
    kern = functools.partial(_nbr_kernel, hd=hd, ctx_rows=ctx_rows, grid_rows=grid_rows, index=index,
                             slabs_per_step=sps)
    return pl.pallas_call(
        kern,
        grid=(batch, slabs // sps, steps),
        in_specs=[
            pl.BlockSpec((tq, width), lambda b, s, i: (b * steps + i, q0 // sps + s)),
            pl.BlockSpec((seq_rows, width), lambda b, s, i: (b, (q0 + slabs) // sps + s)),
            pl.BlockSpec((seq_rows, width), lambda b, s, i: (b, (q0 + 2 * slabs) // sps + s)),
            pl.BlockSpec((2 * sps,) + table.shape[1:], lambda b, s, i: (s, 0, 0, 0)),
        ],
        out_specs=pl.BlockSpec((tq, width), lambda b, s, i: (b * steps + i, s)),
        out_shape=jax.ShapeDtypeStruct((t, qw), BF16),
        scratch_shapes=[pltpu.VMEM((sps, len(index), NBR_UNION * GRID_W, 2 * tq), F32)],
        compiler_params=_params(("parallel", "parallel", "arbitrary")), name="nbr_attention",
    )(p, p, p, table)


def _merge_kernel(*refs):
    ys = refs[0:N_BRANCH]
    gs = refs[N_BRANCH:2 * N_BRANCH]
    ws = refs[2 * N_BRANCH:3 * N_BRANCH]
    o_ref = refs[3 * N_BRANCH]
    acc = None
    for y_ref, g_ref, w_ref in zip(ys, gs, ws):
        term = _sigmoid(g_ref[...].astype(F32)) * _dot(y_ref[...], w_ref[...])
        acc = term if acc is None else acc + term
    o_ref[...] = acc.astype(o_ref.dtype)


def _merge(ys, p, gate_col0, w_branch, tm, tn):
    t, bw = ys[0].shape
    d = w_branch.shape[1]
    nt = d // tn
    y_spec = pl.BlockSpec((tm, bw), lambda i, j: (i, 0))
    g_specs = [pl.BlockSpec((tm, tn), functools.partial(lambda i, j, br: (i, (gate_col0 + br * d) // tn + j), br=br))
               for br in range(N_BRANCH)]
    w_specs = [pl.BlockSpec((bw, tn), functools.partial(lambda i, j, br: (br, j), br=br))
               for br in range(N_BRANCH)]
    return pl.pallas_call(
        _merge_kernel,
        grid=(t // tm, nt),
        in_specs=[y_spec] * N_BRANCH + g_specs + w_specs,
        out_specs=pl.BlockSpec((tm, tn), lambda i, j: (i, j)),
        out_shape=jax.ShapeDtypeStruct((t, d), BF16),
        compiler_params=_params(("parallel", "arbitrary")), name="merge_branches",
    )(*ys, *([p] * N_BRANCH), *([w_branch] * N_BRANCH))


def _moe_kernel(t_ref, gates_ref, w1_ref, b1_ref, w2_ref, b2_ref, o_ref, acc_ref, *, ff, pairs):
    e = pl.program_id(1)
    gates = gates_ref[...]

    @pl.when(e == 0)
    def _():
        acc_ref[...] = _dot(gates.astype(BF16), b2_ref[...])

    tokens = t_ref[...]
    lane = lax.broadcasted_iota(jnp.int32, gates.shape, 1)
    acts = []
    for pr in range(pairs):
        hid = _dot(tokens, w1_ref[pr]) + b1_ref[pr]
        g_h = jnp.minimum(hid[:, :2 * ff], SWIGLU_LIMIT)
        u_h = jnp.clip(hid[:, 2 * ff:], -SWIGLU_LIMIT, SWIGLU_LIMIT)
        cols = []
        for s in range(2):
            expert = (e * pairs + pr) * 2 + s
            gsel = jnp.sum(jnp.where(lane == expert, gates, 0.0), axis=-1, keepdims=True)
            cols.append(jnp.broadcast_to(gsel, (gates.shape[0], ff)))
        act = g_h * _sigmoid(SWIGLU_ALPHA * g_h) * (u_h + 1.0) * jnp.concatenate(cols, axis=-1)
        acts.append(act.astype(BF16))
    acc_ref[...] += _dot(jnp.concatenate(acts, axis=-1), w2_ref[...])

    @pl.when(e == pl.num_programs(1) - 1)
    def _():
        o_ref[...] = acc_ref[...].astype(o_ref.dtype)


def _moe(h, gates, w1p, b1p, w2, b2p, pair0, tm, pairs):
    t, d = h.shape
    width = w1p.shape[2]
    ff = width // 4
    steps = b1p.shape[0] // pairs
    kern = functools.partial(_moe_kernel, ff=ff, pairs=pairs)
    return pl.pallas_call(
        kern,
        grid=(t // tm, steps),
        in_specs=[
            pl.BlockSpec((tm, d), lambda i, e: (i, 0)),
            pl.BlockSpec((tm, LANES), lambda i, e: (i, 0)),
            pl.BlockSpec((pairs, d, width), lambda i, e: (pair0 // pairs + e, 0, 0)),
            pl.BlockSpec((pairs, 1, width), lambda i, e: (e, 0, 0)),
            pl.BlockSpec((pairs * 2 * ff, d), lambda i, e: (e, 0)),
            pl.BlockSpec(b2p.shape, lambda i, e: (0, 0)),
        ],
        out_specs=pl.BlockSpec((tm, d), lambda i, e: (i, 0)),
        out_shape=jax.ShapeDtypeStruct((t, d), BF16),
        scratch_shapes=[pltpu.VMEM((tm, d), F32)],
        compiler_params=_params(("parallel", "arbitrary")), name="moe_experts",
    )(h, gates, w1p, b1p, w2, b2p)


def _rope_tables(n_latent, ctx_rows, hd):
    pos = np.arange(n_latent)
    row = (pos // GRID_W).astype(np.float32)
    col = (pos % GRID_W).astype(np.float32)
    n_freq = hd // 4
    inv = jnp.asarray(ROPE_BASE, F32) ** (-jnp.arange(n_freq, dtype=F32) / n_freq)
    ang = jnp.concatenate([jnp.asarray(row)[:, None] * inv, jnp.asarray(col)[:, None] * inv], axis=-1)
    ang = jnp.concatenate([jnp.zeros((ctx_rows, hd // 2), F32), ang], axis=0)
    cos, sin = jnp.cos(ang), jnp.sin(ang)
    reps = LANES // hd
    cosf = jnp.tile(jnp.concatenate([cos, cos], axis=-1), (1, reps))
    sinf = jnp.tile(jnp.concatenate([-sin, sin], axis=-1), (1, reps))
    return cosf, sinf


def _pad_lanes(a, value=0.0):
    pad = LANES - a.shape[-1]
    return jnp.pad(a, [(0, 0)] * (a.ndim - 1) + [(0, pad)], constant_values=value)


def kernel(x, c, ctx, c_ctx, ada_down, ada_up, ada_bias, norm_gain, w_in, mlstm_gate_bias, mlstm_norm_gain,
           ret_decay, ret_norm_gain, sink, na_bias, w_branch, w_out, router_w, router_b, moe_w1, moe_b1,
           moe_w2, moe_b2, final_gain):
    batch, n_latent, d = x.shape
    ctx_rows = ctx.shape[1]
    depth = w_in.shape[0]
    seq_rows = ctx_rows + n_latent
    t = batch * seq_rows
    bw = d // N_BRANCH
    n_experts = router_w.shape[2]
    ff = moe_w2.shape[2]

    a_dv, b_dv = bw // A_HEADS, bw // B_HEADS
    c_hd, d_hd = bw // C_HEADS, bw // D_HEADS
    assert c_hd == LANES // 2 and d_hd == LANES // 2
    a_sz = (bw // 2, bw // 2, bw, bw)
    n_gate = 4 * A_HEADS
    b_sz = (bw // 2, bw // 2, bw, bw)
    c_sz = (bw, C_KV_HEADS * c_hd, C_KV_HEADS * c_hd)
    d_sz = (bw, bw, bw)
    sizes = a_sz + (n_gate,) + b_sz + c_sz + d_sz + (N_BRANCH * d,)
    offs = [int(o) for o in np.concatenate([[0], np.cumsum(sizes)])]
    a0, g0, b0, c0, d0, bg0, end = offs[0], offs[4], offs[5], offs[9], offs[12], offs[15], offs[16]
    col_gate = 0
    col_a = N_BRANCH * d
    col_b = col_a + sum(a_sz)
    col_d = col_b + sum(b_sz)
    col_c = col_d + sum(d_sz)

    w_in_t = jnp.swapaxes(w_in, 1, 2)
    segments = [(bg0, end), (a0, g0), (b0, c0), (d0, bg0), (c0, d0)]
    w_g_t = _gate_prep(w_in_t, (g0, b0))
    w_cat_t = _win_prep(w_in_t, 0, segments)
    w1p = _moe_w1_prep(moe_w1)
    cast_srcs = (w_branch.reshape(depth, N_BRANCH * bw, d), w_out, moe_w2.reshape(depth, n_experts * ff, d))
    assert cast_srcs[0].shape == cast_srcs[1].shape == cast_srcs[2].shape
    wb_bf, wo_bf, w2_bf = _cast_bf16(0, *cast_srcs)
    b1p_all = jnp.concatenate([moe_b1[..., 0::2].reshape(depth, n_experts // 2, 1, 2 * ff),
                               moe_b1[..., 1::2].reshape(depth, n_experts // 2, 1, 2 * ff)], axis=-1)
    b2p_all = jnp.pad(moe_b2, ((0, 0), (0, LANES - n_experts), (0, 0))).astype(BF16)

    mod_all = _ada_modulation(jnp.pad(jnp.concatenate([c_ctx[None], c], axis=0), ((0, 8 - 1 - batch), (0, 0))),
                              ada_down, ada_up, ada_bias)
    cos_b, sin_b = _rope_tables(n_latent, ctx_rows, b_dv // 2)
    cos_c, sin_c = _rope_tables(n_latent, ctx_rows, c_hd)
    decay_p = jnp.pad(ret_decay, ((0, 0), (0, 8 - ret_decay.shape[1]), (0, LANES - ret_decay.shape[2])))

    tm_proj = _row_tile(t, 768)
    tm_moe = _row_tile(t, 512)
    z = (ctx.reshape(batch * ctx_rows, d), x.reshape(batch * n_latent, d))
    delta = None
    for l in range(depth):
        mod = mod_all[l]
        if l == 0:
            h = _resid_norm(z, norm_gain[l, 0], mod, 0, 1, batch, seq_rows, ctx_rows)[0]
        else:
            z, h = _resid_norm(z, norm_gain[l, 0], mod, 0, 1, batch, seq_rows, ctx_rows, delta=delta,
                               gate_mod=mod_all[l - 1], gate_slot=5)
        if l + 1 < depth:
            p, w_cat_t = _proj_prep(h, w_cat_t, w_in_t, l + 1, segments, tm=tm_proj, tn=1280)
        else:
            p = _matmul(h, w_cat_t[None], 0, BF16, tm=tm_proj, tn=1280, w_transposed=True)[0]
        gates_c, gates_r = _gate_proj(h, w_g_t, l, tm_proj)
        gates_r = gates_r[:n_gate]
        bias_c = _pad_lanes(mlstm_gate_bias[l][None])
        bias_r = mlstm_gate_bias[l][:, None]

        hf = _mlstm(p, col_a, a_dv, gates_c, gates_r, bias_c, bias_r, batch, seq_rows, False)
        hb = _mlstm(p, col_a, a_dv, gates_c, gates_r, bias_c, bias_r, batch, seq_rows, True)
        y_a = _headnorm(hf, hb, p, col_a + 2 * bw, mlstm_norm_gain[l], A_HEADS, False, False)
        of = _retention(p, col_b, b_dv, cos_b, sin_b, decay_p[l], batch, seq_rows, False)
        ob = _retention(p, col_b, b_dv, cos_b, sin_b, decay_p[l], batch, seq_rows, True)
        y_b = _headnorm(of, ob, p, col_b + 2 * bw, ret_norm_gain[l], B_HEADS, True, True)
        y_c = _window_attention(p, col_c, cos_c, sin_c, sink[l], batch, seq_rows, ctx_rows)
        y_d = _nbr_attention(p, col_d, na_bias[l], batch, seq_rows, ctx_rows)

        acc = _merge([y_a, y_b, y_c, y_d], p, col_gate, wb_bf, tm=tm_moe, tn=1024)
        delta1, casts = _matmul(acc, wo_bf[None], 0, BF16, tm=tm_proj, tn=1024,
                                cast_next=(l + 1, cast_srcs) if l + 1 < depth else None)

        rw = _pad_lanes(router_w[l]).astype(BF16)
        rb = _pad_lanes(router_b[l][None], NEG)
        z, h2, gates = _resid_norm(z, norm_gain[l, 1], mod, 3, 4, batch, seq_rows, ctx_rows, delta=delta1,
                                   gate_mod=mod, gate_slot=2, router=(rw, rb))
        delta = _moe(h2, gates, w1p, b1p_all[l], w2_bf, b2p_all[l], l * (n_experts // 2), tm=tm_moe, pairs=2)
        if casts:
            wb_bf, wo_bf, w2_bf = casts

    out = _final_norm(z, delta, mod_all[depth - 1], 5, final_gain, batch, seq_rows, ctx_rows)
    return out.reshape(batch, n_latent, d)
```
